```python
import math
import jax, jax.numpy as jnp
from jax import lax
import numpy as np

D_MODEL = 1024
BATCH = 16
SEQ = 256
DEPTH = 1
DEC_BATCH = 4
DEC_SEQ = 1024
PAST_LEN = 256

GRID_W = 64
EPS = 1e-6
SSM_WIDTH = D_MODEL // 2
SSM_GROUP = 16
SSM_GROUPS = SSM_WIDTH // SSM_GROUP
SSM_STATE = 64
MLA_HEADS = 8
QK_NOPE = 64
QK_ROPE = 32
V_DIM = 64
Q_RANK = 384
KV_RANK = 256
MLA_WIDTH = MLA_HEADS * V_DIM
MIX_WIDTH = SSM_WIDTH + MLA_WIDTH
IN_WIDTH = SSM_WIDTH + Q_RANK + KV_RANK + QK_ROPE
IN_SPLITS = (SSM_WIDTH, SSM_WIDTH + Q_RANK, SSM_WIDTH + Q_RANK + KV_RANK)
ROPE_BASE = 10000.0
Q_BLOCK = 128
N_KEYS = 128
N_EXPERTS = N_KEYS * N_KEYS
PEER_HEADS = 8
PEER_TOPK = 16
KEY_DIM = 128
TOKEN_BLOCK = 128

kernel_name = "hymba_s5_mla_peer_prefix_dit"

F32 = jnp.float32


def rmsnorm(x, g):
    xf = x.astype(F32)
    y = xf * lax.rsqrt(jnp.mean(xf * xf, axis=-1, keepdims=True) + EPS)
    return (y * g.astype(F32)).astype(x.dtype)


def modulation(cvec, w_mod, b_mod):
    m = jax.nn.silu(cvec) @ w_mod + b_mod
    return jnp.split(m[..., None, :], 6, axis=-1)


def modulate(x, g, shift, scale):
    return rmsnorm(x, g) * (1 + scale) + shift


def axial_rope_tables(n_tokens):
    n_rows = n_tokens // GRID_W
    rows = jnp.repeat(jnp.arange(n_rows, dtype=F32), GRID_W)
    cols = jnp.tile(jnp.arange(GRID_W, dtype=F32), n_rows)
    half = QK_ROPE // 2
    inv_freq = 1.0 / (ROPE_BASE ** (jnp.arange(0, half, 2, dtype=F32) / half))
    ang_r = rows[:, None] * inv_freq
    ang_c = cols[:, None] * inv_freq
    ang = jnp.concatenate([ang_r, ang_r, ang_c, ang_c], axis=-1)
    return jnp.cos(ang), jnp.sin(ang)


def apply_rope(x, cos, sin):
    xs = x.reshape(x.shape[:-1] + (2, 2, QK_ROPE // 4))
    rot = jnp.stack([-xs[..., 1, :], xs[..., 0, :]], axis=-2).reshape(x.shape)
    return (x * cos + rot * sin).astype(x.dtype)


def cmul(ar, ai, br, bi):
    return ar * br - ai * bi, ar * bi + ai * br


def s5_discretise(lam_re, lam_im, log_dt):
    dt = jnp.exp(log_dt.astype(F32))[:, None]
    lr, li = lam_re.astype(F32), lam_im.astype(F32)
    mag = jnp.exp(lr * dt)
    ab_re, ab_im = mag * jnp.cos(li * dt), mag * jnp.sin(li * dt)
    den = lr * lr + li * li
    f_re, f_im = cmul(ab_re - 1.0, ab_im, lr / den, -li / den)
    return ab_re, ab_im, f_re, f_im


def s5_combine(e1, e2):
    a1r, a1i, b1r, b1i = e1
    a2r, a2i, b2r, b2i = e2
    ar, ai = cmul(a2r, a2i, a1r, a1i)
    br, bi = cmul(a2r, a2i, b1r, b1i)
    return ar, ai, br + b2r, bi + b2i


def s5_scan(uf, b_re, b_im, lam_re, lam_im, log_dt, h0, reverse):
    ab_re, ab_im, f_re, f_im = s5_discretise(lam_re, lam_im, log_dt)
    bu_re = jnp.einsum('btgp,gnp->btgn', uf, b_re.astype(F32))
    bu_im = jnp.einsum('btgp,gnp->btgn', uf, b_im.astype(F32))
    bu_re, bu_im = cmul(f_re, f_im, bu_re, bu_im)
    if h0 is not None:
        h_re0, h_im0 = cmul(ab_re, ab_im, h0[0].astype(F32), h0[1].astype(F32))
        edge = -1 if reverse else 0
        bu_re = bu_re.at[:, edge].add(h_re0)
        bu_im = bu_im.at[:, edge].add(h_im0)
    a_re = jnp.broadcast_to(ab_re, bu_re.shape)
    a_im = jnp.broadcast_to(ab_im, bu_im.shape)
    _, _, h_re, h_im = lax.associative_scan(s5_combine, (a_re, a_im, bu_re, bu_im),
                                            reverse=reverse, axis=1)
    return h_re, h_im


def s5_mixer(u, p, h0_state):
    B, T, _ = u.shape
    uf = u.astype(F32).reshape(B, T, SSM_GROUPS, SSM_GROUP)
    y = uf * p['ssm_d'].astype(F32).reshape(SSM_GROUPS, SSM_GROUP)
    finals = []
    for d, reverse in ((0, False), (1, True)):
        init = None if h0_state is None else (h0_state[:, d, 0], h0_state[:, d, 1])
        h_re, h_im = s5_scan(uf, p['ssm_b_re'][d], p['ssm_b_im'][d], p['ssm_lam_re'][d],
                             p['ssm_lam_im'][d], p['ssm_log_dt'][d], init, reverse)
        y = y + (jnp.einsum('btgn,gpn->btgp', h_re, p['ssm_c_re'][d].astype(F32))
                 - jnp.einsum('btgn,gpn->btgp', h_im, p['ssm_c_im'][d].astype(F32)))
        edge = 0 if reverse else -1
        finals.append(jnp.stack([h_re[:, edge], h_im[:, edge]], axis=1))
    state = jnp.stack(finals, axis=1)
    yg = jax.nn.gelu(y.reshape(B, T, SSM_WIDTH))
    out = yg * jax.nn.sigmoid(yg @ p['w_glu'].astype(F32))
    return out.astype(u.dtype), state.astype(u.dtype)


def attend(q, k, v):
    B, T, H, dk = q.shape
    dv = v.shape[-1]
    nb = T // Q_BLOCK
    qb = q.reshape(B, nb, Q_BLOCK, H, dk).transpose(1, 0, 2, 3, 4)
    scale = dk ** -0.5

    def one_block(qblk):
        s = jnp.einsum('bqhd,bshd->bhqs', qblk, k).astype(F32) * scale
        w = jax.nn.softmax(s, axis=-1).astype(v.dtype)
        return jnp.einsum('bhqs,bshd->bqhd', w, v)

    o = lax.map(one_block, qb)
    return o.transpose(1, 0, 2, 3, 4).reshape(B, T, H * dv)


def peer(h, w_query, sub_keys, u_table, v_table):
    B, T, D = h.shape
    x = h.reshape(B * T, D)
    q = (x @ w_query).reshape(-1, PEER_HEADS, 2, KEY_DIM)
    s = jnp.einsum('nhid,hikd->nhik', q, sub_keys).astype(F32)
    top_s, top_i = lax.top_k(s, PEER_TOPK)
    cand_s = (top_s[..., 0, :, None] + top_s[..., 1, None, :]).reshape(-1, PEER_HEADS, PEER_TOPK * PEER_TOPK)
    cand_i = (top_i[..., 0, :, None] * N_KEYS + top_i[..., 1, None, :]).reshape(-1, PEER_HEADS, PEER_TOPK * PEER_TOPK)
    best_s, pos = lax.top_k(cand_s, PEER_TOPK)
    idx = jnp.take_along_axis(cand_i, pos, axis=-1)
    g = jax.nn.softmax(best_s, axis=-1)
    nblk = x.shape[0] // TOKEN_BLOCK

    def one_block(args):
        xb, ib, gb = args
        act = jax.nn.gelu(jnp.einsum('nd,nhkd->nhk', xb, u_table[ib]).astype(F32)) * gb
        return jnp.einsum('nhk,nhkd->nd', act.astype(xb.dtype), v_table[ib])

    out = lax.map(one_block, (x.reshape(nblk, TOKEN_BLOCK, D),
                              idx.reshape(nblk, TOKEN_BLOCK, PEER_HEADS, PEER_TOPK),
                              g.reshape(nblk, TOKEN_BLOCK, PEER_HEADS, PEER_TOPK)))
    return out.reshape(B, T, D)


def trunk_layer(x, cvec, p, rope=None, ctx=None):
    B, T, _ = x.shape
    sh1, sc1, g1, sh2, sc2, g2 = modulation(cvec, p['w_mod'], p['b_mod'])
    h = modulate(x, p['norm1'], sh1, sc1)
    u, c_q, c_kv, k_rope = jnp.split(h @ p['w_in'], IN_SPLITS, axis=-1)
    ssm_out, ssm_state = s5_mixer(u, p, None if ctx is None else ctx[2])
    q = (rmsnorm(c_q, p['q_norm']) @ p['w_uq']).reshape(B, T, MLA_HEADS, QK_NOPE + QK_ROPE)
    ckv = rmsnorm(c_kv, p['kv_norm'])
    if ctx is None:
        ckv_all, krope_all = ckv, k_rope
    else:
        cos, sin = rope
        q = jnp.concatenate([q[..., :QK_NOPE],
                             apply_rope(q[..., QK_NOPE:], cos[:, None, :], sin[:, None, :])], axis=-1)
        ckv_all = jnp.concatenate([ctx[0].astype(ckv.dtype), ckv], axis=1)
        krope_all = jnp.concatenate([ctx[1].astype(k_rope.dtype), apply_rope(k_rope, cos, sin)], axis=1)
    S = ckv_all.shape[1]
    k_nope = (ckv_all @ p['w_uk']).reshape(B, S, MLA_HEADS, QK_NOPE)
    v = (ckv_all @ p['w_uv']).reshape(B, S, MLA_HEADS, V_DIM)
    k = jnp.concatenate([k_nope, jnp.broadcast_to(krope_all[:, :, None, :], (B, S, MLA_HEADS, QK_ROPE))], axis=-1)
    attn = attend(q, k, v)
    mix = jnp.concatenate([ssm_out, attn], axis=-1) @ p['w_out']
    x = x + g1 * mix
    h2 = modulate(x, p['norm2'], sh2, sc2)
    x = x + g2 * peer(h2, p['w_query'], p['sub_keys'], p['u_table'], p['v_table'])
    return x, ckv, k_rope, ssm_state


def setup_inputs(seed: int = 0) -> dict:
    key = jax.random.key(seed)
    ks = list(jax.random.split(key, 40))
    cnt = [0]

    def nk():
        cnt[0] += 1
        return ks[cnt[0] - 1]

    def nrm(shape, s):
        return s * jax.random.normal(nk(), shape, F32)

    L, G, N, P, D = DEPTH, SSM_GROUPS, SSM_STATE, SSM_GROUP, D_MODEL
    return {
        'x_prompt': nrm((BATCH, SEQ, D), 1.0),
        'x_sample': nrm((DEC_BATCH, DEC_SEQ, D), 1.0),
        'c': nrm((DEC_BATCH, D), 1.0),
        'cache_ckv': nrm((DEC_BATCH, L, PAST_LEN, KV_RANK), 1.0),
        'cache_krope': nrm((DEC_BATCH, L, PAST_LEN, QK_ROPE), 1.0),
        'state_ssm': nrm((DEC_BATCH, L, 2, 2, G, N), 0.5),
        'c_ctx': nrm((D,), 1.0),
        'w_mod': nrm((L, D, 6 * D), 0.5 * D ** -0.5),
        'b_mod': nrm((L, 6 * D), 0.02),
        'norm1': 1.0 + nrm((L, D), 0.02),
        'w_in': nrm((L, D, IN_WIDTH), D ** -0.5),
        'ssm_lam_re': -0.5 + nrm((L, 2, G, N), 0.01),
        'ssm_lam_im': jnp.pi * jnp.arange(N, dtype=F32) + nrm((L, 2, G, N), 0.01),
        'ssm_log_dt': jax.random.uniform(nk(), (L, 2, G), F32, math.log(1e-3), math.log(1e-1)),
        'ssm_b_re': nrm((L, 2, G, N, P), (2 * P) ** -0.5),
        'ssm_b_im': nrm((L, 2, G, N, P), (2 * P) ** -0.5),
        'ssm_c_re': nrm((L, 2, G, P, N), (2 * N) ** -0.5),
        'ssm_c_im': nrm((L, 2, G, P, N), (2 * N) ** -0.5),
        'ssm_d': nrm((L, SSM_WIDTH), 1.0),
        'w_glu': nrm((L, SSM_WIDTH, SSM_WIDTH), SSM_WIDTH ** -0.5),
        'q_norm': 1.0 + nrm((L, Q_RANK), 0.02),
        'w_uq': nrm((L, Q_RANK, MLA_HEADS * (QK_NOPE + QK_ROPE)), Q_RANK ** -0.5),
        'kv_norm': 1.0 + nrm((L, KV_RANK), 0.02),
        'w_uk': nrm((L, KV_RANK, MLA_HEADS * QK_NOPE), KV_RANK ** -0.5),
        'w_uv': nrm((L, KV_RANK, MLA_HEADS * V_DIM), KV_RANK ** -0.5),
        'w_out': nrm((L, MIX_WIDTH, D), MIX_WIDTH ** -0.5),
        'norm2': 1.0 + nrm((L, D), 0.02),
        'w_query': nrm((L, D, PEER_HEADS * 2 * KEY_DIM), D ** -0.5),
        'sub_keys': nrm((L, PEER_HEADS, 2, N_KEYS, KEY_DIM), KEY_DIM ** -0.5),
        'u_table': nrm((L, N_EXPERTS, D), D ** -0.5),
        'v_table': nrm((L, N_EXPERTS, D), 0.5),
        'final_norm': 1.0 + nrm((D,), 0.02),
    }


def reference(x_prompt, x_sample, c, cache_ckv, cache_krope, state_ssm, c_ctx,
              w_mod, b_mod, norm1, w_in, ssm_lam_re, ssm_lam_im, ssm_log_dt,
              ssm_b_re, ssm_b_im, ssm_c_re, ssm_c_im, ssm_d, w_glu, q_norm, w_uq,
              kv_norm, w_uk, w_uv, w_out, norm2, w_query, sub_keys, u_table, v_table,
              final_norm):
    cos, sin = axial_rope_tables(x_sample.shape[1])
    yp, ys = x_prompt, x_sample
    ckv_list, krope_list, ssm_list = [], [], []
    for l in range(DEPTH):
        p = {
            'w_mod': w_mod[l], 'b_mod': b_mod[l], 'norm1': norm1[l], 'w_in': w_in[l],
            'ssm_lam_re': ssm_lam_re[l], 'ssm_lam_im': ssm_lam_im[l], 'ssm_log_dt': ssm_log_dt[l],
            'ssm_b_re': ssm_b_re[l], 'ssm_b_im': ssm_b_im[l], 'ssm_c_re': ssm_c_re[l],
            'ssm_c_im': ssm_c_im[l], 'ssm_d': ssm_d[l], 'w_glu': w_glu[l],
            'q_norm': q_norm[l], 'w_uq': w_uq[l], 'kv_norm': kv_norm[l], 'w_uk': w_uk[l],
            'w_uv': w_uv[l], 'w_out': w_out[l], 'norm2': norm2[l], 'w_query': w_query[l],
            'sub_keys': sub_keys[l], 'u_table': u_table[l], 'v_table': v_table[l],
        }
        yp, ckv_l, krope_l, ssm_l = trunk_layer(yp, c_ctx, p)
        ckv_list.append(ckv_l)
        krope_list.append(krope_l)
        ssm_list.append(ssm_l)
        ys, _, _, _ = trunk_layer(ys, c, p, rope=(cos, sin),
                                  ctx=(cache_ckv[:, l], cache_krope[:, l], state_ssm[:, l]))
    y_prompt = rmsnorm(yp, final_norm)
    y_sample = rmsnorm(ys, final_norm)
    new_ckv = jnp.stack(ckv_list, axis=1)
    new_krope = jnp.stack(krope_list, axis=1)
    new_ssm = jnp.stack(ssm_list, axis=1)
    return (y_prompt, y_sample, new_ckv, new_krope, new_ssm)
```

```python
import functools
import math

import jax
import jax.numpy as jnp
import numpy as np
from jax import lax
from jax.experimental import pallas as pl
from jax.experimental.pallas import tpu as pltpu

F32 = jnp.float32
BF16 = jnp.bfloat16

D_MODEL = 1024
GRID_W = 64
EPS = 1e-6
SSM_WIDTH = 512
SSM_GROUP = 16
SSM_GROUPS = 32
SSM_STATE = 64
MLA_HEADS = 8
QK_NOPE = 64
QK_ROPE = 32
V_DIM = 64
Q_RANK = 384
KV_RANK = 256
ROPE_BASE = 10000.0
N_KEYS = 128
N_EXPERTS = N_KEYS * N_KEYS
PEER_HEADS = 8
PEER_TOPK = 16
KEY_DIM = 128

LANES = 128
HEAD_PAD = 128
TOK_TILE = 256
S5_PAIR = 2
S5_BLOCKS = SSM_GROUPS // S5_PAIR
S5_HALF = SSM_WIDTH // 2
S5_BLOCKS_PER_HALF = S5_BLOCKS // 2
PEER_TN = 512
PEER_ROWS = 8
PEER_TE = PEER_ROWS * N_KEYS
VMEM_LIMIT = 48 * 1024 * 1024


def _cparams(sem):
    return pltpu.CompilerParams(dimension_semantics=sem, vmem_limit_bytes=VMEM_LIMIT)


def _rms(x, g):
    return x * lax.rsqrt(jnp.mean(x * x, axis=-1, keepdims=True) + EPS) * g


def _mod_kernel(c_ref, w_ref, b_ref, o_ref):
    o_ref[...] = jnp.dot(jax.nn.silu(c_ref[...]), w_ref[...], preferred_element_type=F32) + b_ref[...]


def _modulation(cvec, w_mod, b_mod):
    rows, d = cvec.shape
    n = w_mod.shape[1]
    tn = 1536
    return pl.pallas_call(
        _mod_kernel,
        grid=(n // tn,),
        in_specs=[pl.BlockSpec((rows, d), lambda j: (0, 0)),
                  pl.BlockSpec((d, tn), lambda j: (0, j)),
                  pl.BlockSpec((1, tn), lambda j: (0, j))],
        out_specs=pl.BlockSpec((rows, tn), lambda j: (0, j)),
        out_shape=jax.ShapeDtypeStruct((rows, n), F32),
        compiler_params=_cparams(("parallel",)),
        name="modulation",
    )(cvec, w_mod, b_mod)


def _pre_kernel(use_rope, x_ref, m_ref, n1_ref, win_ref, qn_ref, kvn_ref, wuq_ref, *rest):
    if use_rope:
        cos_ref, sin_ref, u_ref, q_ref, ckv_ref, kr_ref = rest
    else:
        u_ref, q_ref, ckv_ref, kr_ref = rest
    m = m_ref[0]
    h = _rms(x_ref[...], n1_ref[...]) * (1.0 + m[1:2]) + m[0:1]
    z = jnp.dot(h.astype(BF16), win_ref[...], preferred_element_type=F32)
    u_ref[...] = z[:, :SSM_WIDTH]
    o_q, o_kv, o_kr = SSM_WIDTH, SSM_WIDTH + Q_RANK, SSM_WIDTH + Q_RANK + KV_RANK
    cqn = _rms(z[:, o_q:o_kv], qn_ref[...])
    qq = jnp.dot(cqn.astype(BF16), wuq_ref[...], preferred_element_type=F32)
    ckv_ref[...] = _rms(z[:, o_kv:o_kr], kvn_ref[...])
    kr = z[:, o_kr:o_kr + HEAD_PAD]
    scale = (QK_NOPE + QK_ROPE) ** -0.5
    nq = MLA_HEADS * HEAD_PAD
    if use_rope:
        cos, sin = cos_ref[...], sin_ref[...]
        kr = kr * cos + z[:, o_kr + HEAD_PAD:o_kr + 2 * HEAD_PAD] * sin
        for hd in range(MLA_HEADS):
            sl = slice(hd * HEAD_PAD, (hd + 1) * HEAD_PAD)
            qh = qq[:, sl] * cos + qq[:, nq + hd * HEAD_PAD:nq + (hd + 1) * HEAD_PAD] * sin
            q_ref[:, sl] = (qh * scale).astype(BF16)
    else:
        q_ref[...] = (qq[:, :nq] * scale).astype(BF16)
    kr_ref[...] = kr


def _pre(x2d, m6, mrow, norm1, w_in_ext, q_norm, kv_norm, w_uq_ext, batch, seq, rope):
    ntok = batch * seq
    n_t = seq // TOK_TILE
    use_rope = rope is not None
    full = lambda a: pl.BlockSpec(a.shape, lambda i: (0,) * a.ndim)
    in_specs = [pl.BlockSpec((TOK_TILE, D_MODEL), lambda i: (i, 0)),
                pl.BlockSpec((1, 6, D_MODEL), lambda i: (mrow(i), 0, 0)),
                full(norm1), full(w_in_ext), full(q_norm), full(kv_norm), full(w_uq_ext)]
    args = [x2d, m6, norm1, w_in_ext, q_norm, kv_norm, w_uq_ext]
    if use_rope:
        in_specs += [pl.BlockSpec((TOK_TILE, HEAD_PAD), lambda i: (i % n_t, 0))] * 2
        args += list(rope)
    out_specs = [pl.BlockSpec((TOK_TILE, SSM_WIDTH), lambda i: (i % n_t, i // n_t)),
                 pl.BlockSpec((TOK_TILE, MLA_HEADS * HEAD_PAD), lambda i: (i, 0)),
                 pl.BlockSpec((TOK_TILE, KV_RANK), lambda i: (i, 0)),
                 pl.BlockSpec((TOK_TILE, HEAD_PAD), lambda i: (i, 0))]
    out_shape = [jax.ShapeDtypeStruct((seq, batch * SSM_WIDTH), F32),
                 jax.ShapeDtypeStruct((ntok, MLA_HEADS * HEAD_PAD), BF16),
                 jax.ShapeDtypeStruct((ntok, KV_RANK), F32),
                 jax.ShapeDtypeStruct((ntok, HEAD_PAD), F32)]
    return pl.pallas_call(
        functools.partial(_pre_kernel, use_rope),
        grid=(ntok // TOK_TILE,),
        in_specs=in_specs, out_specs=out_specs, out_shape=out_shape,
        compiler_params=_cparams(("parallel",)),
        name="pre_rope" if use_rope else "pre",
    )(*args)


def _s5_param_kernel(lr_ref, li_ref, ldt_ref, lrx_ref, lix_ref, ldtx_ref, bre_ref, bim_ref, cim_ref,
                     abr_ref, abi_ref, bfr_ref, bfi_ref, ncim_ref):
    def disc(lr, li, ldt):
        dt = jnp.exp(ldt)
        mag = jnp.exp(lr * dt)
        ab_re, ab_im = mag * jnp.cos(li * dt), mag * jnp.sin(li * dt)
        den = lr * lr + li * li
        br, bi = lr / den, -li / den
        ar = ab_re - 1.0
        return ab_re, ab_im, ar * br - ab_im * bi, ar * bi + ab_im * br

    ab_re, ab_im, _, _ = disc(lr_ref[...], li_ref[...], ldt_ref[...])
    abr_ref[...] = ab_re
    abi_ref[...] = ab_im
    _, _, f_re, f_im = disc(lrx_ref[...], lix_ref[...], ldtx_ref[...])
    b_re, b_im = bre_ref[...], bim_ref[...]
    bfr_ref[...] = f_re * b_re - f_im * b_im
    bfi_ref[...] = f_re * b_im + f_im * b_re
    ncim_ref[...] = -cim_ref[...]


def _s5_params(lam_re, lam_im, log_dt, b_re, b_im, c_re, c_im):
    dg = 2 * SSM_GROUPS
    n, p = SSM_STATE, SSM_GROUP
    lr = lam_re.reshape(dg, n)
    li = lam_im.reshape(dg, n)
    ldt = jnp.broadcast_to(log_dt.reshape(dg, 1), (dg, n))
    rep = lambda a: jnp.repeat(a, p, axis=1)
    args = [lr, li, ldt, rep(lr), rep(li), rep(ldt),
            b_re.reshape(dg, n * p), b_im.reshape(dg, n * p), c_im.reshape(dg, p * n)]
    small = jax.ShapeDtypeStruct((dg, n), F32)
    big = jax.ShapeDtypeStruct((dg, n * p), F32)
    ab_re, ab_im, bf_re, bf_im, ncim = pl.pallas_call(
        _s5_param_kernel, out_shape=[small, small, big, big, big], name="s5_params")(*args)

    nb, gp = S5_BLOCKS, S5_PAIR
    eye = jnp.eye(gp, dtype=F32)
    pos = np.arange(nb) % S5_BLOCKS_PER_HALF
    bf = jnp.stack([bf_re, bf_im], 0).reshape(2, 2, nb, gp, n, p)
    bc = jnp.einsum('adbgnp,gh->bdgpahn', bf, eye).reshape(nb, 2, gp * p, 2 * gp * n)
    bw = jnp.zeros((nb, 2, S5_BLOCKS_PER_HALF, gp * p, 2 * gp * n), F32)
    bw = bw.at[np.arange(nb), :, pos].set(bc).reshape(nb, 2, S5_HALF, 2 * gp * n)
    cc = jnp.stack([c_re.reshape(2, nb, gp, p, n), ncim.reshape(2, nb, gp, p, n)], 0)
    cc = jnp.einsum('adbgpn,gh->bdagnhp', cc, eye).reshape(nb, 2, 2 * gp * n, gp * p)
    cw = jnp.zeros((nb, 2, 2 * gp * n, S5_BLOCKS_PER_HALF, gp * p), F32)
    cw = cw.at[np.arange(nb), :, :, pos].set(cc).reshape(nb, 2, 2 * gp * n, S5_HALF)
    a = jnp.stack([ab_re, ab_im], 0).reshape(2, 2, nb, gp * n).transpose(2, 1, 0, 3)
    return bw.astype(BF16), cw.astype(BF16), a


def _s5_kernel(rows_per_step, steps, uf_ref, ub_ref, bw_ref, cw_ref, a_ref, h0_ref,
               yf_ref, yb_ref, st_ref, buf_f, buf_b, carry):
    r = rows_per_step
    tc, sb = pl.program_id(1), pl.program_id(2)

    @pl.when(tc == 0)
    def _():
        carry[sb] = h0_ref[0]

    for d, (u_ref, y_ref, buf) in enumerate(((uf_ref, yf_ref, buf_f), (ub_ref, yb_ref, buf_b))):
        buf[...] = jnp.dot(u_ref[...].astype(BF16), bw_ref[0, d], preferred_element_type=F32)
        a_re = jnp.broadcast_to(a_ref[0, d, 0:1, :], (r, LANES))
        a_im = jnp.broadcast_to(a_ref[0, d, 1:2, :], (r, LANES))

        def body(k, h, d=d, buf=buf, a_re=a_re, a_im=a_im):
            h_re, h_im = h
            t = k if d == 0 else steps - 1 - k
            r0 = pl.multiple_of(t * r, r)
            n_re = a_re * h_re - a_im * h_im + buf[pl.ds(r0, r), 0:LANES]
            n_im = a_re * h_im + a_im * h_re + buf[pl.ds(r0, r), LANES:2 * LANES]
            buf[pl.ds(r0, r), 0:LANES] = n_re
            buf[pl.ds(r0, r), LANES:2 * LANES] = n_im
            return n_re, n_im

        h_re, h_im = lax.fori_loop(0, steps, body, (carry[sb, d, 0], carry[sb, d, 1]), unroll=8)
        carry[sb, d, 0] = h_re
        carry[sb, d, 1] = h_im
        contrib = jnp.dot(buf[...].astype(BF16), cw_ref[0, d], preferred_element_type=F32)

        @pl.when(sb == 0)
        def _(y_ref=y_ref, contrib=contrib):
            y_ref[...] = contrib

        @pl.when(sb > 0)
        def _(y_ref=y_ref, contrib=contrib):
            y_ref[...] += contrib

    st_ref[pl.program_id(0) * S5_BLOCKS_PER_HALF + sb] = carry[sb]


def _s5_scan(u_tm, bw, cw, a, h0, rows_per_step, seq, steps):
    r = rows_per_step
    n_t = seq // steps
    rows = steps * r
    bph = S5_BLOCKS_PER_HALF
    blk = lambda h, t, s: (h * bph + s, 0, 0, 0)
    blk5 = lambda h, t, s: (h * bph + s, 0, 0, 0, 0)
    in_specs = [pl.BlockSpec((rows, S5_HALF), lambda h, t, s: (t, h)),
                pl.BlockSpec((rows, S5_HALF), lambda h, t, s: (n_t - 1 - t, h)),
                pl.BlockSpec((1, 2, S5_HALF, 2 * LANES), blk),
                pl.BlockSpec((1, 2, 2 * LANES, S5_HALF), blk),
                pl.BlockSpec((1, 2, 2, LANES), blk),
                pl.BlockSpec((1, 2, 2, r, LANES), blk5)]
    out_specs = [pl.BlockSpec((rows, S5_HALF), lambda h, t, s: (t, h)),
                 pl.BlockSpec((rows, S5_HALF), lambda h, t, s: (n_t - 1 - t, h)),
                 pl.BlockSpec((S5_BLOCKS, 2, 2, r, LANES), lambda h, t, s: (0, 0, 0, 0, 0))]
    out_shape =[jax.ShapeDtypeStruct(u_tm.shape, F32), jax.ShapeDtypeStruct(u_tm.shape, F32),
                 jax.ShapeDtypeStruct((S5_BLOCKS, 2, 2, r, LANES), F32)]
    return pl.pallas_call(
        functools.partial(_s5_kernel, r, steps),
        grid=(2, n_t, bph),
        in_specs=in_specs, out_specs=out_specs, out_shape=out_shape,
        scratch_shapes=[pltpu.VMEM((rows, 2 * LANES), F32), pltpu.VMEM((rows, 2 * LANES), F32),
                        pltpu.VMEM((bph, 2, 2, r, LANES), F32)],
        compiler_params=_cparams(("arbitrary", "arbitrary", "arbitrary")),
        name="s5_scan",
    )(u_tm, u_tm, bw, cw, a, h0)


def _attn_kernel(q_ref, ckv_ref, kr_ref, wuk_ref, wuv_ref, o_ref, k_s, v_s):
    @pl.when(pl.program_id(1) == 0)
    def _():
        kv = ckv_ref[0].astype(BF16)
        kn = jnp.dot(kv, wuk_ref[...], preferred_element_type=F32)
        kr = kr_ref[0]
        for hd in range(MLA_HEADS):
            k_s[hd] = (kn[:, hd * HEAD_PAD:(hd + 1) * HEAD_PAD] + kr).astype(BF16)
        v_s[...] = jnp.dot(kv, wuv_ref[...], preferred_element_type=F32).astype(BF16)

    for hd in range(MLA_HEADS):
        qh = q_ref[:, hd * HEAD_PAD:(hd + 1) * HEAD_PAD]
        s = lax.dot_general(qh, k_s[hd], (((1,), (1,)), ((), ())), preferred_element_type=F32)
        p = jnp.exp(s - jnp.max(s, axis=-1, keepdims=True))
        l = jnp.sum(p, axis=-1, keepdims=True)
        o = jnp.dot(p.astype(BF16), v_s[:, hd * V_DIM:(hd + 1) * V_DIM], preferred_element_type=F32)
        o_ref[:, hd * V_DIM:(hd + 1) * V_DIM] = (o / l).astype(BF16)


def _attention(q, ckv_all, kr_all, w_uk_ext, w_uv, batch, seq):
    s_len = ckv_all.shape[1]
    n_q = seq // TOK_TILE
    return pl.pallas_call(
        _attn_kernel,
        grid=(batch, n_q),
        in_specs=[pl.BlockSpec((TOK_TILE, MLA_HEADS * HEAD_PAD), lambda b, i: (b * n_q + i, 0)),
                  pl.BlockSpec((1, s_len, KV_RANK), lambda b, i: (b, 0, 0)),
                  pl.BlockSpec((1, s_len, HEAD_PAD), lambda b, i: (b, 0, 0)),
                  pl.BlockSpec(w_uk_ext.shape, lambda b, i: (0, 0)),
                  pl.BlockSpec(w_uv.shape, lambda b, i: (0, 0))],
        out_specs=pl.BlockSpec((TOK_TILE, MLA_HEADS * V_DIM), lambda b, i: (b * n_q + i, 0)),
        out_shape=jax.ShapeDtypeStruct((batch * seq, MLA_HEADS * V_DIM), BF16),
        scratch_shapes=[pltpu.VMEM((MLA_HEADS, s_len, HEAD_PAD), BF16),
                        pltpu.VMEM((s_len, MLA_HEADS * V_DIM), BF16)],
        compiler_params=_cparams(("parallel", "arbitrary")),
        name="attention",
    )(q, ckv_all, kr_all, w_uk_ext, w_uv)


def _post_kernel(x_ref, at_ref, u_ref, yf_ref, yb_ref, m_ref, d_ref, wglu_ref, wos_ref, woa_ref,
                 n2_ref, wq_ref, sk_ref, x1_ref, h2_ref, s1_ref, s2_ref):
    m = m_ref[0]
    y = yf_ref[...] + yb_ref[...] + u_ref[...] * d_ref[...]
    yg = jax.nn.gelu(y)
    gate = jax.nn.sigmoid(jnp.dot(yg.astype(BF16), wglu_ref[...], preferred_element_type=F32))
    mix = (jnp.dot((yg * gate).astype(BF16), wos_ref[...], preferred_element_type=F32)
           + jnp.dot(at_ref[...], woa_ref[...], preferred_element_type=F32))
    x1 = x_ref[...] + m[2:3] * mix
    x1_ref[...] = x1
    h2 = (_rms(x1, n2_ref[...]) * (1.0 + m[4:5]) + m[3:4]).astype(BF16)
    h2_ref[...] = h2
    qp = jnp.dot(h2, wq_ref[...], preferred_element_type=F32).astype(BF16)
    for hd in range(PEER_HEADS):
        for half, s_ref in enumerate((s1_ref, s2_ref)):
            c0 = (hd * 2 + half) * KEY_DIM
            s_ref[hd] = lax.dot_general(sk_ref[hd, half], qp[:, c0:c0 + KEY_DIM],
                                        (((1,), (1,)), ((), ())), preferred_element_type=F32)


def _post(x2d, attn, u_tm, yf, yb, m6, mrow, ssm_d, w_glu, w_out_s, w_out_a, norm2, w_query, sub_keys,
          batch, seq):
    ntok = batch * seq
    n_t = seq // TOK_TILE
    full = lambda a: pl.BlockSpec(a.shape, lambda i: (0,) * a.ndim)
    tm = pl.BlockSpec((TOK_TILE, SSM_WIDTH), lambda i: (i % n_t, i // n_t))
    tok = lambda w: pl.BlockSpec((TOK_TILE, w), lambda i: (i, 0))
    sc = pl.BlockSpec((PEER_HEADS, N_KEYS, TOK_TILE), lambda i: (0, 0, i))
    return pl.pallas_call(
        _post_kernel,
        grid=(ntok // TOK_TILE,),
        in_specs=[tok(D_MODEL), tok(MLA_HEADS * V_DIM), tm, tm, tm,
                  pl.BlockSpec((1, 6, D_MODEL), lambda i: (mrow(i), 0, 0)),
                  full(ssm_d), full(w_glu), full(w_out_s), full(w_out_a), full(norm2), full(w_query),
                  full(sub_keys)],
        out_specs=[tok(D_MODEL), tok(D_MODEL), sc, sc],
        out_shape=[jax.ShapeDtypeStruct((ntok, D_MODEL), F32), jax.ShapeDtypeStruct((ntok, D_MODEL), BF16),
                   jax.ShapeDtypeStruct((PEER_HEADS, N_KEYS, ntok), F32),
                   jax.ShapeDtypeStruct((PEER_HEADS, N_KEYS, ntok), F32)],
        compiler_params=_cparams(("parallel",)),
        name="post",
    )(x2d, attn, u_tm, yf, yb, m6, ssm_d, w_glu, w_out_s, w_out_a, norm2, w_query, sub_keys)


def _take_top(cur, count):
    rows = lax.broadcasted_iota(jnp.int32, cur.shape, 0).astype(F32)
    vals = []
    for _ in range(count):
        mx = jnp.max(cur, axis=0, keepdims=True)
        first = jnp.min(jnp.where(cur == mx, rows, float(cur.shape[0])), axis=0, keepdims=True)
        cur = jnp.where(rows == first, -jnp.inf, cur)
        vals.append(mx)
    return vals


def _stack_rows(vals):
    rows = lax.broadcasted_iota(jnp.int32, (len(vals), vals[0].shape[1]), 0)
    out = jnp.zeros((len(vals), vals[0].shape[1]), F32)
    for k, v in enumerate(vals):
        out = jnp.where(rows == k, v, out)
    return out


def _route_kernel(s1_ref, s2_ref, o_ref):
    k = PEER_TOPK
    half_k = k // 2

    def head(hd, _):
        t1 = _stack_rows(_take_top(s1_ref[hd], k))
        t2 = _stack_rows(_take_top(s2_ref[hd], k))
        rows8 = lax.broadcasted_iota(jnp.int32, (half_k, t1.shape[1]), 0)
        cands = [t1[0:1] + t2, t1[1:2] + t2[:half_k]]
        for a in range(2, half_k):
            cands.append(jnp.where(rows8 < k // (a + 1), t1[a:a + 1] + t2[:half_k], -jnp.inf))
        cands.append(t1[half_k:] + t2[0:1])
        best = _take_top(jnp.concatenate(cands, axis=0), k + 1)
        z = jnp.ones_like(best[0])
        for v in best[1:k]:
            z = z + jnp.exp(v - best[0])
        o_ref[0, hd] = 0.5 * (best[k - 1] + best[k])
        o_ref[1, hd] = t1[0:1]
        o_ref[2, hd] = t2[0:1]
        o_ref[3, hd] = 1.0 / z
        return 0

    lax.fori_loop(0, PEER_HEADS, head, 0)


def _route(s1t, s2t):
    ntok = s1t.shape[2]
    spec = pl.BlockSpec((PEER_HEADS, N_KEYS, PEER_TN), lambda i: (0, 0, i))
    return pl.pallas_call(
        _route_kernel,
        grid=(ntok // PEER_TN,),
        in_specs=[spec, spec],
        out_specs=pl.BlockSpec((4, PEER_HEADS, 1, PEER_TN), lambda i: (0, 0, 0, i)),
        out_shape=jax.ShapeDtypeStruct((4, PEER_HEADS, 1, ntok), F32),
        compiler_params=_cparams(("parallel",)),
        name="peer_route",
    )(s1t, s2t)


def _peer_kernel(h2_ref, s1_ref, s2_ref, st_ref, u_ref, v_ref, x1_ref, m_ref, fn_ref, y_ref,
                 e2_s, act_s, aw_s, acc_s):
    e = pl.program_id(1)

    @pl.when(e == 0)
    def _():
        for hd in range(PEER_HEADS):
            e2_s[hd] = jnp.exp(s2_ref[hd] - st_ref[2, hd])
        acc_s[...] = jnp.zeros_like(acc_s)

    act_s[...] = lax.dot_general(u_ref[...], h2_ref[...], (((1,), (1,)), ((), ())),
                                 preferred_element_type=F32)
    lim = [st_ref[0, hd] - s1_ref[hd] for hd in range(PEER_HEADS)]
    e1 = [jnp.exp(s1_ref[hd] - st_ref[1, hd]) * st_ref[3, hd] for hd in range(PEER_HEADS)]
    for ii in range(PEER_ROWS):
        r0 = ii * N_KEYS
        for c in range(PEER_TN // LANES):
            cs = slice(c * LANES, (c + 1) * LANES)
            w = jnp.zeros((N_KEYS, LANES), F32)
            for hd in range(PEER_HEADS):
                sel = jnp.where(s2_ref[hd, :, cs] >= lim[hd][ii:ii + 1, cs], e2_s[hd, :, cs], 0.0)
                w = w + sel * e1[hd][ii:ii + 1, cs]
            aw_s[r0:r0 + N_KEYS, cs] = (jax.nn.gelu(act_s[r0:r0 + N_KEYS, cs]) * w).astype(BF16)
    acc_s[...] += lax.dot_general(aw_s[...], v_ref[...], (((0,), (0,)), ((), ())),
                                  preferred_element_type=F32)

    @pl.when(e == pl.num_programs(1) - 1)
    def _():
        y_ref[...] = _rms(x1_ref[...] + m_ref[0][5:6] * acc_s[...], fn_ref[...])


def _peer(h2, s1t, s2t, stats, u_tab, v_tab, x1, m6, mrow, final_norm):
    ntok = h2.shape[0]
    tn, te = PEER_TN, PEER_TE
    tok = pl.BlockSpec((tn, D_MODEL), lambda t, e: (t, 0))
    tab = pl.BlockSpec((te, D_MODEL), lambda t, e: (e, 0))
    return pl.pallas_call(
        _peer_kernel,
        grid=(ntok // tn, N_EXPERTS // te),
        in_specs=[tok,
                  pl.BlockSpec((PEER_HEADS, PEER_ROWS, tn), lambda t, e: (0, e, t)),
                  pl.BlockSpec((PEER_HEADS, N_KEYS, tn), lambda t, e: (0, 0, t)),
                  pl.BlockSpec((4, PEER_HEADS, 1, tn), lambda t, e: (0, 0, 0, t)),
                  tab, tab, tok,
                  pl.BlockSpec((1, 6, D_MODEL), lambda t, e: (mrow(t), 0, 0)),
                  pl.BlockSpec((1, D_MODEL), lambda t, e: (0, 0))],
        out_specs=tok,
        out_shape=jax.ShapeDtypeStruct((ntok, D_MODEL), F32),
        scratch_shapes=[pltpu.VMEM((PEER_HEADS, N_KEYS, tn), F32), pltpu.VMEM((te, tn), F32),
                        pltpu.VMEM((te, tn), BF16), pltpu.VMEM((tn, D_MODEL), F32)],
        compiler_params=_cparams(("parallel", "arbitrary")),
        name="peer_dense",
    )(h2, s1t, s2t, stats, u_tab, v_tab, x1, m6, final_norm)


def _rot_cols(w):
    j = np.arange(QK_ROPE)
    first = (j % (QK_ROPE // 2)) < (QK_ROPE // 4)
    perm = np.where(first, j + QK_ROPE // 4, j - QK_ROPE // 4)
    sign = np.where(first, -1.0, 1.0).astype(np.float32)
    return w[..., perm] * sign


def _rope_slot(w_rope):
    pad = [(0, 0)] * (w_rope.ndim - 1)
    return jnp.pad(w_rope, pad + [(QK_NOPE, HEAD_PAD - QK_NOPE - QK_ROPE)])


def _rope_tables(n_tokens):
    n_rows = n_tokens // GRID_W
    rows = jnp.repeat(jnp.arange(n_rows, dtype=F32), GRID_W)
    cols = jnp.tile(jnp.arange(GRID_W, dtype=F32), n_rows)
    half = QK_ROPE // 2
    inv_freq = 1.0 / (ROPE_BASE ** (jnp.arange(0, half, 2, dtype=F32) / half))
    ang_r = rows[:, None] * inv_freq
    ang_c = cols[:, None] * inv_freq
    ang = jnp.concatenate([ang_r, ang_r, ang_c, ang_c], axis=-1)
    lead = (QK_NOPE, HEAD_PAD - QK_NOPE - QK_ROPE)
    cos = jnp.pad(jnp.cos(ang), [(0, 0), lead], constant_values=1.0)
    sin = jnp.pad(jnp.sin(ang), [(0, 0), lead])
    return cos, sin


def kernel(x_prompt, x_sample, c, cache_ckv, cache_krope, state_ssm, c_ctx, w_mod, b_mod, norm1, w_in,
           ssm_lam_re, ssm_lam_im, ssm_log_dt, ssm_b_re, ssm_b_im, ssm_c_re, ssm_c_im, ssm_d, w_glu,
           q_norm, w_uq, kv_norm, w_uk, w_uv, w_out, norm2, w_query, sub_keys, u_table, v_table,
           final_norm):
    bc, tc_len, _ = x_prompt.shape
    bl, tl_len, _ = x_sample.shape
    l = 0
    row = lambda a: a.reshape(1, -1)

    n_mod = 8
    cvec = jnp.concatenate([c_ctx[None], c, jnp.zeros((n_mod - 1 - bl, D_MODEL), F32)], 0)
    m6 = _modulation(cvec, w_mod[l], row(b_mod[l])).reshape(n_mod, 6, D_MODEL)

    wi = w_in[l]
    o_kr = SSM_WIDTH + Q_RANK + KV_RANK
    w_kr = wi[:, o_kr:]
    w_in_ext = jnp.concatenate([wi[:, :o_kr], _rope_slot(w_kr), _rope_slot(_rot_cols(w_kr))], 1).astype(BF16)
    wq3 = w_uq[l].reshape(Q_RANK, MLA_HEADS, QK_NOPE + QK_ROPE)
    wq_main = jnp.pad(wq3, [(0, 0), (0, 0), (0, HEAD_PAD - QK_NOPE - QK_ROPE)])
    wq_rot = _rope_slot(_rot_cols(wq3[..., QK_NOPE:]))
    w_uq_ext = jnp.concatenate([wq_main.reshape(Q_RANK, -1), wq_rot.reshape(Q_RANK, -1)], 1).astype(BF16)
    w_uk_ext = jnp.pad(w_uk[l].reshape(KV_RANK, MLA_HEADS, QK_NOPE),
                       [(0, 0), (0, 0), (0, HEAD_PAD - QK_NOPE)]).reshape(KV_RANK, -1).astype(BF16)
    w_uv_b = w_uv[l].astype(BF16)
    w_glu_b = w_glu[l].astype(BF16)
    w_out_s = w_out[l][:SSM_WIDTH].astype(BF16)
    w_out_a = w_out[l][SSM_WIDTH:].astype(BF16)
    w_query_b = w_query[l].astype(BF16)
    sub_keys_b = sub_keys[l].astype(BF16)
    u_tab = u_table[l].astype(BF16)
    v_tab = v_table[l].astype(BF16)
    cos, sin = _rope_tables(tl_len)

    bw, cw, a_blk = _s5_params(ssm_lam_re[l], ssm_lam_im[l], ssm_log_dt[l], ssm_b_re[l], ssm_b_im[l],
                               ssm_c_re[l], ssm_c_im[l])

    ctx_row = lambda i: 0
    lat_tiles = tl_len // TOK_TILE
    lat_row = lambda i: 1 + i // lat_tiles
    xc = x_prompt.reshape(bc * tc_len, D_MODEL)
    xl = x_sample.reshape(bl * tl_len, D_MODEL)

    u_c, q_c, ckv_c, kr_c = _pre(xc, m6, ctx_row, row(norm1[l]), w_in_ext, row(q_norm[l]), row(kv_norm[l]),
                                 w_uq_ext[:, :MLA_HEADS * HEAD_PAD], bc, tc_len, None)
    u_l, q_l, ckv_l, kr_l = _pre(xl, m6, lat_row, row(norm1[l]), w_in_ext, row(q_norm[l]), row(kv_norm[l]),
                                 w_uq_ext, bl, tl_len, (cos, sin))

    rl = 8
    h0_c = jnp.zeros((S5_BLOCKS, 2, 2, bc, LANES), F32)
    yf_c, yb_c, st_c = _s5_scan(u_c.reshape(tc_len * bc, SSM_WIDTH), bw, cw, a_blk, h0_c, bc, tc_len, 128)
    u_lp = jnp.pad(u_l.reshape(tl_len, bl, SSM_WIDTH), [(0, 0), (0, rl - bl), (0, 0)])
    h0_l = state_ssm[:, l].reshape(bl, 2, 2, S5_BLOCKS, LANES).transpose(3, 1, 2, 0, 4)
    h0_l = jnp.pad(h0_l, [(0, 0), (0, 0), (0, 0), (0, rl - bl), (0, 0)])
    yf_l, yb_l, _ = _s5_scan(u_lp.reshape(tl_len * rl, SSM_WIDTH), bw, cw, a_blk, h0_l, rl, tl_len, 256)

    at_c = _attention(q_c, ckv_c.reshape(bc, tc_len, KV_RANK), kr_c.reshape(bc, tc_len, HEAD_PAD),
                      w_uk_ext, w_uv_b, bc, tc_len)
    ckv_all = jnp.concatenate([cache_ckv[:, l], ckv_l.reshape(bl, tl_len, KV_RANK)], 1)
    kr_all = jnp.concatenate([_rope_slot(cache_krope[:, l]), kr_l.reshape(bl, tl_len, HEAD_PAD)], 1)
    at_l = _attention(q_l, ckv_all, kr_all, w_uk_ext, w_uv_b, bl, tl_len)

    post = functools.partial(_post, ssm_d=row(ssm_d[l]), w_glu=w_glu_b, w_out_s=w_out_s, w_out_a=w_out_a,
                             norm2=row(norm2[l]), w_query=w_query_b, sub_keys=sub_keys_b)
    tm_c = lambda a: a.reshape(tc_len, bc * SSM_WIDTH)
    tm_l = lambda a: a.reshape(tl_len, rl * SSM_WIDTH)
    x1_c, h2_c, s1_c, s2_c = post(xc, at_c, tm_c(u_c), tm_c(yf_c), tm_c(yb_c), m6, ctx_row,
                                  batch=bc, seq=tc_len)
    x1_l, h2_l, s1_l, s2_l = post(xl, at_l, tm_l(u_lp), tm_l(yf_l), tm_l(yb_l), m6, lat_row,
                                  batch=bl, seq=tl_len)

    x1 = jnp.concatenate([x1_c, x1_l], 0)
    h2 = jnp.concatenate([h2_c, h2_l], 0)
    s1t = jnp.concatenate([s1_c, s1_l], 2)
    s2t = jnp.concatenate([s2_c, s2_l], 2)
    stats = _route(s1t, s2t)
    ctx_peer_tiles = bc * tc_len // PEER_TN
    lat_peer_tiles = tl_len // PEER_TN
    peer_row = lambda t: jnp.where(t < ctx_peer_tiles, 0, 1 + (t - ctx_peer_tiles) // lat_peer_tiles)
    y = _peer(h2, s1t, s2t, stats, u_tab, v_tab, x1, m6, peer_row, row(final_norm))

    y_prompt = y[:bc * tc_len].reshape(bc, tc_len, D_MODEL)
    y_sample = y[bc * tc_len:].reshape(bl, tl_len, D_MODEL)
    new_ckv = ckv_c.reshape(bc, 1, tc_len, KV_RANK)
    new_krope = kr_c[:, QK_NOPE:QK_NOPE + QK_ROPE].reshape(bc, 1, tc_len, QK_ROPE)
    new_ssm = st_c.reshape(S5_BLOCKS, 2, 2, bc, S5_PAIR, SSM_STATE).transpose(3, 1, 2, 0, 4, 5)
    new_ssm = new_ssm.reshape(bc, 1, 2, 2, SSM_GROUPS, SSM_STATE)
    return (y_prompt, y_sample, new_ckv, new_krope, new_ssm)
```

```python
import functools
import math

import jax
import jax.numpy as jnp
import numpy as np
from jax import lax
from jax.experimental import pallas as pl
from jax.experimental.pallas import tpu as pltpu

F32 = jnp.float32
BF16 = jnp.bfloat16

D_MODEL = 1024
GRID_W = 64
EPS = 1e-6
SSM_WIDTH = 512
SSM_GROUP = 16
SSM_GROUPS = 32
SSM_STATE = 64
MLA_HEADS = 8
QK_NOPE = 64
QK_ROPE = 32
V_DIM = 64
Q_RANK = 384
KV_RANK = 256
ROPE_BASE = 10000.0
N_KEYS = 128
N_EXPERTS = N_KEYS * N_KEYS
PEER_HEADS = 8
PEER_TOPK = 16
KEY_DIM = 128

LANES = 128
HEAD_PAD = 128
TOK_TILE = 256
S5_PAIR = 2
S5_BLOCKS = SSM_GROUPS // S5_PAIR
S5_HALF = SSM_WIDTH // 2
S5_BLOCKS_PER_HALF = S5_BLOCKS // 2
S5_TILE_ROWS = 512
PEER_TN = 512
PEER_ROWS = 8
PEER_TE = PEER_ROWS * N_KEYS
VMEM_LIMIT = 48 * 1024 * 1024


def _cparams(sem):
    return pltpu.CompilerParams(dimension_semantics=sem, vmem_limit_bytes=VMEM_LIMIT)


def _rms(x, g):
    return x * lax.rsqrt(jnp.mean(x * x, axis=-1, keepdims=True) + EPS) * g


def _mod_kernel(c_ref, w_ref, b_ref, o_ref):
    o_ref[...] = jnp.dot(jax.nn.silu(c_ref[...]), w_ref[...], preferred_element_type=F32) + b_ref[...]


def _modulation(cvec, w_mod, b_mod):
    rows, d = cvec.shape
    n = w_mod.shape[1]
    tn = 1536
    return pl.pallas_call(
        _mod_kernel,
        grid=(n // tn,),
        in_specs=[pl.BlockSpec((rows, d), lambda j: (0, 0)),
                  pl.BlockSpec((d, tn), lambda j: (0, j)),
                  pl.BlockSpec((1, tn), lambda j: (0, j))],
        out_specs=pl.BlockSpec((rows, tn), lambda j: (0, j)),
        out_shape=jax.ShapeDtypeStruct((rows, n), F32),
        compiler_params=_cparams(("parallel",)),
        name="modulation",
    )(cvec, w_mod, b_mod)


def _pre_kernel(use_rope, x_ref, m_ref, n1_ref, win_ref, qn_ref, kvn_ref, wuq_ref, *rest):
    if use_rope:
        cos_ref, sin_ref, u_ref, q_ref, ckv_ref, kr_ref = rest
    else:
        u_ref, q_ref, ckv_ref, kr_ref = rest
    m = m_ref[0]
    h = _rms(x_ref[...], n1_ref[...]) * (1.0 + m[1:2]) + m[0:1]
    z = jnp.dot(h.astype(BF16), win_ref[...], preferred_element_type=F32)
    u_ref[...] = z[:, :SSM_WIDTH]
    o_q, o_kv, o_kr = SSM_WIDTH, SSM_WIDTH + Q_RANK, SSM_WIDTH + Q_RANK + KV_RANK
    cqn = _rms(z[:, o_q:o_kv], qn_ref[...])
    qq = jnp.dot(cqn.astype(BF16), wuq_ref[...], preferred_element_type=F32)
    ckv_ref[...] = _rms(z[:, o_kv:o_kr], kvn_ref[...])
    kr = z[:, o_kr:o_kr + HEAD_PAD]
    scale = (QK_NOPE + QK_ROPE) ** -0.5
    nq = MLA_HEADS * HEAD_PAD
    if use_rope:
        cos, sin = cos_ref[...], sin_ref[...]
        kr = kr * cos + z[:, o_kr + HEAD_PAD:o_kr + 2 * HEAD_PAD] * sin
        for hd in range(MLA_HEADS):
            sl = slice(hd * HEAD_PAD, (hd + 1) * HEAD_PAD)
            qh = qq[:, sl] * cos + qq[:, nq + hd * HEAD_PAD:nq + (hd + 1) * HEAD_PAD] * sin
            q_ref[:, sl] = (qh * scale).astype(BF16)
    else:
        q_ref[...] = (qq[:, :nq] * scale).astype(BF16)
    kr_ref[...] = kr


def _pre(x2d, m6, mrow, norm1, w_in_ext, q_norm, kv_norm, w_uq_ext, batch, seq, rope):
    ntok = batch * seq
    n_t = seq // TOK_TILE
    use_rope = rope is not None
    full = lambda a: pl.BlockSpec(a.shape, lambda i: (0,) * a.ndim)
    in_specs = [pl.BlockSpec((TOK_TILE, D_MODEL), lambda i: (i, 0)),
                pl.BlockSpec((1, 6, D_MODEL), lambda i: (mrow(i), 0, 0)),
                full(norm1), full(w_in_ext), full(q_norm), full(kv_norm), full(w_uq_ext)]
    args = [x2d, m6, norm1, w_in_ext, q_norm, kv_norm, w_uq_ext]
    if use_rope:
        in_specs += [pl.BlockSpec((TOK_TILE, HEAD_PAD), lambda i: (i % n_t, 0))] * 2
        args += list(rope)
    out_specs = [pl.BlockSpec((TOK_TILE, SSM_WIDTH), lambda i: (i % n_t, i // n_t)),
                 pl.BlockSpec((TOK_TILE, MLA_HEADS * HEAD_PAD), lambda i: (i, 0)),
                 pl.BlockSpec((TOK_TILE, KV_RANK), lambda i: (i, 0)),
                 pl.BlockSpec((TOK_TILE, HEAD_PAD), lambda i: (i, 0))]
    out_shape = [jax.ShapeDtypeStruct((seq, batch * SSM_WIDTH), F32),
                 jax.ShapeDtypeStruct((ntok, MLA_HEADS * HEAD_PAD), BF16),
                 jax.ShapeDtypeStruct((ntok, KV_RANK), F32),
                 jax.ShapeDtypeStruct((ntok, HEAD_PAD), F32)]
    return pl.pallas_call(
        functools.partial(_pre_kernel, use_rope),
        grid=(ntok // TOK_TILE,),
        in_specs=in_specs, out_specs=out_specs, out_shape=out_shape,
        compiler_params=_cparams(("parallel",)),
        name="pre_rope" if use_rope else "pre",
    )(*args)


def _s5_param_kernel(lr_ref, li_ref, ldt_ref, lrx_ref, lix_ref, ldtx_ref, bre_ref, bim_ref, cim_ref,
                     abr_ref, abi_ref, bfr_ref, bfi_ref, ncim_ref):
    def disc(lr, li, ldt):
        dt = jnp.exp(ldt)
        mag = jnp.exp(lr * dt)
        ab_re, ab_im = mag * jnp.cos(li * dt), mag * jnp.sin(li * dt)
        den = lr * lr + li * li
        br, bi = lr / den, -li / den
        ar = ab_re - 1.0
        return ab_re, ab_im, ar * br - ab_im * bi, ar * bi + ab_im * br

    ab_re, ab_im, _, _ = disc(lr_ref[...], li_ref[...], ldt_ref[...])
    abr_ref[...] = ab_re
    abi_ref[...] = ab_im
    _, _, f_re, f_im = disc(lrx_ref[...], lix_ref[...], ldtx_ref[...])
    b_re, b_im = bre_ref[...], bim_ref[...]
    bfr_ref[...] = f_re * b_re - f_im * b_im
    bfi_ref[...] = f_re * b_im + f_im * b_re
    ncim_ref[...] = -cim_ref[...]


def _s5_params(lam_re, lam_im, log_dt, b_re, b_im, c_re, c_im):
    dg = 2 * SSM_GROUPS
    n, p = SSM_STATE, SSM_GROUP
    lr = lam_re.reshape(dg, n)
    li = lam_im.reshape(dg, n)
    ldt = jnp.broadcast_to(log_dt.reshape(dg, 1), (dg, n))
    rep = lambda a: jnp.repeat(a, p, axis=1)
    args = [lr, li, ldt, rep(lr), rep(li), rep(ldt),
            b_re.reshape(dg, n * p), b_im.reshape(dg, n * p), c_im.reshape(dg, p * n)]
    small = jax.ShapeDtypeStruct((dg, n), F32)
    big = jax.ShapeDtypeStruct((dg, n * p), F32)
    ab_re, ab_im, bf_re, bf_im, ncim = pl.pallas_call(
        _s5_param_kernel, out_shape=[small, small, big, big, big], name="s5_params")(*args)

    nb, gp = S5_BLOCKS, S5_PAIR
    eye = jnp.eye(gp, dtype=F32)
    pos = np.arange(nb) % S5_BLOCKS_PER_HALF
    bf = jnp.stack([bf_re, bf_im], 0).reshape(2, 2, nb, gp, n, p)
    bc = jnp.einsum('adbgnp,gh->bdgpahn', bf, eye).reshape(nb, 2, gp * p, 2 * gp * n)
    bw = jnp.zeros((nb, 2, S5_BLOCKS_PER_HALF, gp * p, 2 * gp * n), F32)
    bw = bw.at[np.arange(nb), :, pos].set(bc).reshape(nb, 2, S5_HALF, 2 * gp * n)
    cc = jnp.stack([c_re.reshape(2, nb, gp, p, n), ncim.reshape(2, nb, gp, p, n)], 0)
    cc = jnp.einsum('adbgpn,gh->bdagnhp', cc, eye).reshape(nb, 2, 2 * gp * n, gp * p)
    cw = jnp.zeros((nb, 2, 2 * gp * n, S5_BLOCKS_PER_HALF, gp * p), F32)
    cw = cw.at[np.arange(nb), :, :, pos].set(cc).reshape(nb, 2, 2 * gp * n, S5_HALF)
    a = jnp.stack([ab_re, ab_im], 0).reshape(2, 2, nb, gp * n).transpose(2, 1, 0, 3)
    hb = S5_BLOCKS_PER_HALF
    bw = bw.reshape(2, hb, 2, S5_HALF, 2 * gp * n).transpose(0, 2, 3, 1, 4).reshape(2, 2, S5_HALF, -1)
    cw = cw.reshape(2, hb, 2, 2 * gp * n, S5_HALF).transpose(0, 2, 1, 3, 4).reshape(2, 2, -1, S5_HALF)
    a = a.reshape(2, hb, 2, 2, 1, gp * n).transpose(0, 2, 1, 3, 4, 5)
    return bw.astype(BF16), cw.astype(BF16), a


def _s5_state_in(h):
    bsz = h.shape[0]
    return h.reshape(bsz, 2, 2, 2, S5_BLOCKS_PER_HALF, LANES).transpose(3, 1, 4, 2, 0, 5)


def _s5_state_out(st):
    bsz = st.shape[4]
    return st.transpose(4, 1, 3, 0, 2, 5).reshape(bsz, 2, 2, SSM_GROUPS, SSM_STATE)


def _s5_kernel(rows_per_step, steps, uf_ref, ub_ref, bw_ref, cw_ref, a_ref, h0_ref,
               yf_ref, yb_ref, st_ref, buf, coef, carry):
    r = rows_per_step
    nb = S5_BLOCKS_PER_HALF
    width = 2 * LANES

    @pl.when(pl.program_id(1) == 0)
    def _():
        carry[...] = h0_ref[0]
        coef[...] = jnp.broadcast_to(a_ref[0], coef.shape)

    for d, u_ref in enumerate((uf_ref, ub_ref)):
        buf[d] = jnp.dot(u_ref[...].astype(BF16), bw_ref[0, d], preferred_element_type=F32)

    chains = [(d, b) for d in range(2) for b in range(nb)]
    for g in range(r // 8):
        rows8 = slice(g * 8, (g + 1) * 8)

        def body(k, hs, rows8=rows8):
            out = []
            for (d, b), (h_re, h_im) in zip(chains, hs):
                t = k if d == 0 else steps - 1 - k
                r0 = pl.multiple_of(t * r + g * 8, 8)
                a_re, a_im = coef[d, b, 0], coef[d, b, 1]
                re_l, im_l = slice(b * width, b * width + LANES), slice(b * width + LANES, (b + 1) * width)
                n_re = a_re * h_re - a_im * h_im + buf[d, pl.ds(r0, 8), re_l]
                n_im = a_re * h_im + a_im * h_re + buf[d, pl.ds(r0, 8), im_l]
                buf[d, pl.ds(r0, 8), re_l] = n_re
                buf[d, pl.ds(r0, 8), im_l] = n_im
                out.append((n_re, n_im))
            return tuple(out)

        init = tuple((carry[d, b, 0, rows8], carry[d, b, 1, rows8]) for d, b in chains)
        final = lax.fori_loop(0, steps, body, init)
        for (d, b), (h_re, h_im) in zip(chains, final):
            carry[d, b, 0, rows8] = h_re
            carry[d, b, 1, rows8] = h_im

    for d, y_ref in enumerate((yf_ref, yb_ref)):
        y_ref[...] = jnp.dot(buf[d].astype(BF16), cw_ref[0, d], preferred_element_type=F32)
    st_ref[0] = carry[...]


def _s5_scan(u_tm, bw, cw, a, h0, rows_per_step, seq, steps):
    r = rows_per_step
    n_t = seq // steps
    rows = steps * r
    nb = S5_BLOCKS_PER_HALF
    wide = nb * 2 * LANES
    half4 = lambda h, t: (h, 0, 0, 0)
    half6 = lambda h, t: (h, 0, 0, 0, 0, 0)
    state = (2, nb, 2, r, LANES)
    in_specs = [pl.BlockSpec((rows, S5_HALF), lambda h, t: (t, h)),
                pl.BlockSpec((rows, S5_HALF), lambda h, t: (n_t - 1 - t, h)),
                pl.BlockSpec((1, 2, S5_HALF, wide), half4),
                pl.BlockSpec((1, 2, wide, S5_HALF), half4),
                pl.BlockSpec((1, 2, nb, 2, 1, LANES), half6),
                pl.BlockSpec((1,) + state, half6)]
    out_specs = [pl.BlockSpec((rows, S5_HALF), lambda h, t: (t, h)),
                 pl.BlockSpec((rows, S5_HALF), lambda h, t: (n_t - 1 - t, h)),
                 pl.BlockSpec((1,) + state, half6)]
    out_shape = [jax.ShapeDtypeStruct(u_tm.shape, F32), jax.ShapeDtypeStruct(u_tm.shape, F32),
                 jax.ShapeDtypeStruct((2,) + state, F32)]
    return pl.pallas_call(
        functools.partial(_s5_kernel, r, steps),
        grid=(2, n_t),
        in_specs=in_specs, out_specs=out_specs, out_shape=out_shape,
        scratch_shapes=[pltpu.VMEM((2, rows, wide), F32), pltpu.VMEM((2, nb, 2, 8, LANES), F32),
                        pltpu.VMEM(state, F32)],
        compiler_params=_cparams(("arbitrary", "arbitrary")),
        name="s5_scan",
    )(u_tm, u_tm, bw, cw, a, h0)


def _attn_kernel(q_ref, ckv_ref, kr_ref, wuk_ref, wuv_ref, o_ref, k_s, v_s):
    @pl.when(pl.program_id(1) == 0)
    def _():
        kv = ckv_ref[0].astype(BF16)
        kn = jnp.dot(kv, wuk_ref[...], preferred_element_type=F32)
        kr = kr_ref[0]
        for hd in range(MLA_HEADS):
            k_s[hd] = (kn[:, hd * HEAD_PAD:(hd + 1) * HEAD_PAD] + kr).astype(BF16)
        v_s[...] = jnp.dot(kv, wuv_ref[...], preferred_element_type=F32).astype(BF16)

    for hd in range(MLA_HEADS):
        qh = q_ref[:, hd * HEAD_PAD:(hd + 1) * HEAD_PAD]
        s = lax.dot_general(qh, k_s[hd], (((1,), (1,)), ((), ())), preferred_element_type=F32)
        p = jnp.exp(s - jnp.max(s, axis=-1, keepdims=True))
        l = jnp.sum(p, axis=-1, keepdims=True)
        o = jnp.dot(p.astype(BF16), v_s[:, hd * V_DIM:(hd + 1) * V_DIM], preferred_element_type=F32)
        o_ref[:, hd * V_DIM:(hd + 1) * V_DIM] = (o / l).astype(BF16)


def _attention(q, ckv_all, kr_all, w_uk_ext, w_uv, batch, seq):
    s_len = ckv_all.shape[1]
    n_q = seq // TOK_TILE
    return pl.pallas_call(
        _attn_kernel,
        grid=(batch, n_q),
        in_specs=[pl.BlockSpec((TOK_TILE, MLA_HEADS * HEAD_PAD), lambda b, i: (b * n_q + i, 0)),
                  pl.BlockSpec((1, s_len, KV_RANK), lambda b, i: (b, 0, 0)),
                  pl.BlockSpec((1, s_len, HEAD_PAD), lambda b, i: (b, 0, 0)),
                  pl.BlockSpec(w_uk_ext.shape, lambda b, i: (0, 0)),
                  pl.BlockSpec(w_uv.shape, lambda b, i: (0, 0))],
        out_specs=pl.BlockSpec((TOK_TILE, MLA_HEADS * V_DIM), lambda b, i: (b * n_q + i, 0)),
        out_shape=jax.ShapeDtypeStruct((batch * seq, MLA_HEADS * V_DIM), BF16),
        scratch_shapes=[pltpu.VMEM((MLA_HEADS, s_len, HEAD_PAD), BF16),
                        pltpu.VMEM((s_len, MLA_HEADS * V_DIM), BF16)],
        compiler_params=_cparams(("parallel", "arbitrary")),
        name="attention",
    )(q, ckv_all, kr_all, w_uk_ext, w_uv)


def _post_kernel(x_ref, at_ref, u_ref, yf_ref, yb_ref, m_ref, d_ref, wglu_ref, wos_ref, woa_ref,
                 n2_ref, wq_ref, sk_ref, x1_ref, h2_ref, s1_ref, s2_ref):
    m = m_ref[0]
    y = yf_ref[...] + yb_ref[...] + u_ref[...] * d_ref[...]
    yg = jax.nn.gelu(y)
    gate = jax.nn.sigmoid(jnp.dot(yg.astype(BF16), wglu_ref[...], preferred_element_type=F32))
    mix = (jnp.dot((yg * gate).astype(BF16), wos_ref[...], preferred_element_type=F32)
           + jnp.dot(at_ref[...], woa_ref[...], preferred_element_type=F32))
    x1 = x_ref[...] + m[2:3] * mix
    x1_ref[...] = x1
    h2 = (_rms(x1, n2_ref[...]) * (1.0 + m[4:5]) + m[3:4]).astype(BF16)
    h2_ref[...] = h2
    qp = jnp.dot(h2, wq_ref[...], preferred_element_type=F32).astype(BF16)
    for hd in range(PEER_HEADS):
        for half, s_ref in enumerate((s1_ref, s2_ref)):
            c0 = (hd * 2 + half) * KEY_DIM
            s_ref[hd] = lax.dot_general(sk_ref[hd, half], qp[:, c0:c0 + KEY_DIM],
                                        (((1,), (1,)), ((), ())), preferred_element_type=F32)


def _post(x2d, attn, u_tm, yf, yb, m6, mrow, ssm_d, w_glu, w_out_s, w_out_a, norm2, w_query, sub_keys,
          batch, seq):
    ntok = batch * seq
    n_t = seq // TOK_TILE
    full = lambda a: pl.BlockSpec(a.shape, lambda i: (0,) * a.ndim)
    tm = pl.BlockSpec((TOK_TILE, SSM_WIDTH), lambda i: (i % n_t, i // n_t))
    tok = lambda w: pl.BlockSpec((TOK_TILE, w), lambda i: (i, 0))
    sc = pl.BlockSpec((PEER_HEADS, N_KEYS, TOK_TILE), lambda i: (0, 0, i))
    return pl.pallas_call(
        _post_kernel,
        grid=(ntok // TOK_TILE,),
        in_specs=[tok(D_MODEL), tok(MLA_HEADS * V_DIM), tm, tm, tm,
                  pl.BlockSpec((1, 6, D_MODEL), lambda i: (mrow(i), 0, 0)),
                  full(ssm_d), full(w_glu), full(w_out_s), full(w_out_a), full(norm2), full(w_query),
                  full(sub_keys)],
        out_specs=[tok(D_MODEL), tok(D_MODEL), sc, sc],
        out_shape=[jax.ShapeDtypeStruct((ntok, D_MODEL), F32), jax.ShapeDtypeStruct((ntok, D_MODEL), BF16),
                   jax.ShapeDtypeStruct((PEER_HEADS, N_KEYS, ntok), F32),
                   jax.ShapeDtypeStruct((PEER_HEADS, N_KEYS, ntok), F32)],
        compiler_params=_cparams(("parallel",)),
        name="post",
    )(x2d, attn, u_tm, yf, yb, m6, ssm_d, w_glu, w_out_s, w_out_a, norm2, w_query, sub_keys)


def _sort_pairs(lo, hi):
    def merge(lo, hi, r):
        step = r * 2
        if step < hi - lo:
            yield from merge(lo, hi, step)
            yield from merge(lo + r, hi, step)
            for i in range(lo + r, hi - r, step):
                yield (i, i + r)
        else:
            yield (lo, lo + r)

    if hi - lo >= 1:
        mid = lo + (hi - lo) // 2
        yield from _sort_pairs(lo, mid)
        yield from _sort_pairs(mid + 1, hi)
        yield from merge(lo, hi, 1)


def _sort_desc(vals):
    vals = list(vals)
    for i, j in _sort_pairs(0, len(vals) - 1):
        vals[i], vals[j] = jnp.maximum(vals[i], vals[j]), jnp.minimum(vals[i], vals[j])
    return vals


def _merge_top(a, b):
    n = len(a)
    c = [jnp.maximum(a[k], b[n - 1 - k]) for k in range(n)]
    stride = n // 2
    while stride:
        for i in range(n):
            if not i & stride:
                c[i], c[i + stride] = jnp.maximum(c[i], c[i + stride]), jnp.minimum(c[i], c[i + stride])
        stride //= 2
    return c


def _top_keys(ref):
    k = PEER_TOPK
    groups = [_sort_desc([ref[0, g * k + r] for r in range(k)]) for g in range(N_KEYS // k)]
    while len(groups) > 1:
        groups = [_merge_top(groups[i], groups[i + 1]) for i in range(0, len(groups), 2)]
    return groups[0]


def _route_kernel(s1_ref, s2_ref, o_ref, t2_ref):
    k = PEER_TOPK
    t1, t2 = _top_keys(s1_ref), _top_keys(s2_ref)
    cands = [t1[a] + t2[b] for a in range(k) for b in range(k) if (a + 1) * (b + 1) <= k]
    size = 1 << (len(cands) - 1).bit_length()
    cands += [jnp.full_like(t1[0], -jnp.inf)] * (size - len(cands))
    best = _sort_desc(cands)
    z = jnp.ones_like(best[0])
    for v in best[1:k]:
        z = z + jnp.exp(v - best[0])
    o_ref[0, 0] = best[k - 1]
    o_ref[1, 0] = t1[0]
    o_ref[2, 0] = t2[0]
    o_ref[3, 0] = 1.0 / z
    for b in range(k):
        t2_ref[0, b] = t2[b]


def _route(s1d, s2d):
    n_rows = s1d.shape[2]
    spec = pl.BlockSpec((1, N_KEYS, 8, LANES), lambda i, h: (h, 0, i, 0))
    return pl.pallas_call(
        _route_kernel,
        grid=(n_rows // 8, PEER_HEADS),
        in_specs=[spec, spec],
        out_specs=[pl.BlockSpec((4, 1, 8, LANES), lambda i, h: (0, h, i, 0)),
                   pl.BlockSpec((1, PEER_TOPK, 8, LANES), lambda i, h: (h, 0, i, 0))],
        out_shape=[jax.ShapeDtypeStruct((4, PEER_HEADS, n_rows, LANES), F32),
                   jax.ShapeDtypeStruct((PEER_HEADS, PEER_TOPK, n_rows, LANES), F32)],
        compiler_params=_cparams(("parallel", "parallel")),
        name="peer_route",
    )(s1d, s2d)


def _peer_kernel(h2_ref, s1_ref, s2_ref, t2_ref, st_ref, u_ref, v_ref, x1_ref, m_ref, fn_ref, y_ref,
                 rank_s, e2_s, act_s, aw_s, acc_s):
    e = pl.program_id(1)
    chunks = [slice(c * LANES, (c + 1) * LANES) for c in range(PEER_TN // LANES)]

    @pl.when(e == 0)
    def _():
        for hd in range(PEER_HEADS):
            for cs in chunks:
                s2 = s2_ref[hd, :, cs]
                rank = jnp.zeros_like(s2)
                for b in range(PEER_TOPK):
                    rank = rank + jnp.where(s2 < t2_ref[hd, b:b + 1, cs], 1.0, 0.0)
                rank_s[hd, :, cs] = rank.astype(BF16)
                e2_s[hd, :, cs] = jnp.exp(s2 - st_ref[2, hd, :, cs]).astype(BF16)
        acc_s[...] = jnp.zeros_like(acc_s)

    count, gain = [], []
    for hd in range(PEER_HEADS):
        s1 = s1_ref[hd]
        cnt = jnp.zeros_like(s1)
        for b in range(PEER_TOPK):
            cnt = cnt + jnp.where(s1 + t2_ref[hd, b:b + 1, :] >= st_ref[0, hd], 1.0, 0.0)
        count.append(cnt)
        gain.append(jnp.exp(s1 - st_ref[1, hd]) * st_ref[3, hd])

    zero = jnp.zeros((), BF16)
    act_s[...] = lax.dot_general(u_ref[...], h2_ref[...], (((1,), (1,)), ((), ())),
                                 preferred_element_type=F32)
    for ii in range(PEER_ROWS):
        rows = slice(ii * N_KEYS, (ii + 1) * N_KEYS)
        for cs in chunks:
            w = None
            for hd in range(PEER_HEADS):
                limit = jnp.broadcast_to(count[hd][ii:ii + 1, cs], (N_KEYS, LANES)).astype(BF16)
                scale = jnp.broadcast_to(gain[hd][ii:ii + 1, cs], (N_KEYS, LANES)).astype(BF16)
                term = jnp.where(rank_s[hd, :, cs] < limit, e2_s[hd, :, cs], zero) * scale
                w = term if w is None else w + term
            aw_s[rows, cs] = jax.nn.gelu(act_s[rows, cs].astype(BF16)) * w
    acc_s[...] += lax.dot_general(aw_s[...], v_ref[...], (((0,), (0,)), ((), ())),
                                  preferred_element_type=F32)

    @pl.when(e == pl.num_programs(1) - 1)
    def _():
        y_ref[...] = _rms(x1_ref[...] + m_ref[0][5:6] * acc_s[...], fn_ref[...])


def _peer(h2, s1t, s2t, t2, stats, u_tab, v_tab, x1, m6, mrow, final_norm):
    ntok = h2.shape[0]
    tn, te = PEER_TN, PEER_TE
    tok = pl.BlockSpec((tn, D_MODEL), lambda t, e: (t, 0))
    tab = pl.BlockSpec((te, D_MODEL), lambda t, e: (e, 0))
    keys = pltpu.VMEM((PEER_HEADS, N_KEYS, tn), BF16)
    return pl.pallas_call(
        _peer_kernel,
        grid=(ntok // tn, N_EXPERTS // te),
        in_specs=[tok,
                  pl.BlockSpec((PEER_HEADS, PEER_ROWS, tn), lambda t, e: (0, e, t)),
                  pl.BlockSpec((PEER_HEADS, N_KEYS, tn), lambda t, e: (0, 0, t)),
                  pl.BlockSpec((PEER_HEADS, PEER_TOPK, tn), lambda t, e: (0, 0, t)),
                  pl.BlockSpec((4, PEER_HEADS, 1, tn), lambda t, e: (0, 0, 0, t)),
                  tab, tab, tok,
                  pl.BlockSpec((1, 6, D_MODEL), lambda t, e: (mrow(t), 0, 0)),
                  pl.BlockSpec((1, D_MODEL), lambda t, e: (0, 0))],
        out_specs=tok,
        out_shape=jax.ShapeDtypeStruct((ntok, D_MODEL), F32),
        scratch_shapes=[keys, keys, pltpu.VMEM((te, tn), F32),
                        pltpu.VMEM((te, tn), BF16), pltpu.VMEM((tn, D_MODEL), F32)],
        compiler_params=_cparams(("parallel", "arbitrary")),
        name="peer_dense",
    )(h2, s1t, s2t, t2, stats, u_tab, v_tab, x1, m6, final_norm)


def _rot_cols(w):
    j = np.arange(QK_ROPE)
    first = (j % (QK_ROPE // 2)) < (QK_ROPE // 4)
    perm = np.where(first, j + QK_ROPE // 4, j - QK_ROPE // 4)
    sign = np.where(first, -1.0, 1.0).astype(np.float32)
    return w[..., perm] * sign


def _rope_slot(w_rope):
    pad = [(0, 0)] * (w_rope.ndim - 1)
    return jnp.pad(w_rope, pad + [(QK_NOPE, HEAD_PAD - QK_NOPE - QK_ROPE)])


def _rope_tables(n_tokens):
    n_rows = n_tokens // GRID_W
    rows = jnp.repeat(jnp.arange(n_rows, dtype=F32), GRID_W)
    cols = jnp.tile(jnp.arange(GRID_W, dtype=F32), n_rows)
    half = QK_ROPE // 2
    inv_freq = 1.0 / (ROPE_BASE ** (jnp.arange(0, half, 2, dtype=F32) / half))
    ang_r = rows[:, None] * inv_freq
    ang_c = cols[:, None] * inv_freq
    ang = jnp.concatenate([ang_r, ang_r, ang_c, ang_c], axis=-1)
    lead = (QK_NOPE, HEAD_PAD - QK_NOPE - QK_ROPE)
    cos = jnp.pad(jnp.cos(ang), [(0, 0), lead], constant_values=1.0)
    sin = jnp.pad(jnp.sin(ang), [(0, 0), lead])
    return cos, sin


def kernel(x_prompt, x_sample, c, cache_ckv, cache_krope, state_ssm, c_ctx, w_mod, b_mod, norm1, w_in,
           ssm_lam_re, ssm_lam_im, ssm_log_dt, ssm_b_re, ssm_b_im, ssm_c_re, ssm_c_im, ssm_d, w_glu,
           q_norm, w_uq, kv_norm, w_uk, w_uv, w_out, norm2, w_query, sub_keys, u_table, v_table,
           final_norm):
    bc, tc_len, _ = x_prompt.shape
    bl, tl_len, _ = x_sample.shape
    l = 0
    row = lambda a: a.reshape(1, -1)

    n_mod = 8
    cvec = jnp.concatenate([c_ctx[None], c, jnp.zeros((n_mod - 1 - bl, D_MODEL), F32)], 0)
    m6 = _modulation(cvec, w_mod[l], row(b_mod[l])).reshape(n_mod, 6, D_MODEL)

    wi = w_in[l]
    o_kr = SSM_WIDTH + Q_RANK + KV_RANK
    w_kr = wi[:, o_kr:]
    w_in_ext = jnp.concatenate([wi[:, :o_kr], _rope_slot(w_kr), _rope_slot(_rot_cols(w_kr))], 1).astype(BF16)
    wq3 = w_uq[l].reshape(Q_RANK, MLA_HEADS, QK_NOPE + QK_ROPE)
    wq_main = jnp.pad(wq3, [(0, 0), (0, 0), (0, HEAD_PAD - QK_NOPE - QK_ROPE)])
    wq_rot = _rope_slot(_rot_cols(wq3[..., QK_NOPE:]))
    w_uq_ext = jnp.concatenate([wq_main.reshape(Q_RANK, -1), wq_rot.reshape(Q_RANK, -1)], 1).astype(BF16)
    w_uk_ext = jnp.pad(w_uk[l].reshape(KV_RANK, MLA_HEADS, QK_NOPE),
                       [(0, 0), (0, 0), (0, HEAD_PAD - QK_NOPE)]).reshape(KV_RANK, -1).astype(BF16)
    w_uv_b = w_uv[l].astype(BF16)
    w_glu_b = w_glu[l].astype(BF16)
    w_out_s = w_out[l][:SSM_WIDTH].astype(BF16)
    w_out_a = w_out[l][SSM_WIDTH:].astype(BF16)
    w_query_b = w_query[l].astype(BF16)
    sub_keys_b = sub_keys[l].astype(BF16)
    u_tab = u_table[l].astype(BF16)
    v_tab = v_table[l].astype(BF16)
    cos, sin = _rope_tables(tl_len)

    bw, cw, a_blk = _s5_params(ssm_lam_re[l], ssm_lam_im[l], ssm_log_dt[l], ssm_b_re[l], ssm_b_im[l],
                               ssm_c_re[l], ssm_c_im[l])

    ctx_row = lambda i: 0
    lat_tiles = tl_len // TOK_TILE
    lat_row = lambda i: 1 + i // lat_tiles
    xc = x_prompt.reshape(bc * tc_len, D_MODEL)
    xl = x_sample.reshape(bl * tl_len, D_MODEL)

    u_c, q_c, ckv_c, kr_c = _pre(xc, m6, ctx_row, row(norm1[l]), w_in_ext, row(q_norm[l]), row(kv_norm[l]),
                                 w_uq_ext[:, :MLA_HEADS * HEAD_PAD], bc, tc_len, None)
    u_l, q_l, ckv_l, kr_l = _pre(xl, m6, lat_row, row(norm1[l]), w_in_ext, row(q_norm[l]), row(kv_norm[l]),
                                 w_uq_ext, bl, tl_len, (cos, sin))

    rl = 8
    h0_c = jnp.zeros((2, 2, S5_BLOCKS_PER_HALF, 2, bc, LANES), F32)
    yf_c, yb_c, st_c = _s5_scan(u_c.reshape(tc_len * bc, SSM_WIDTH), bw, cw, a_blk, h0_c, bc, tc_len,
                                S5_TILE_ROWS // bc)
    u_lp = jnp.pad(u_l.reshape(tl_len, bl, SSM_WIDTH), [(0, 0), (0, rl - bl), (0, 0)])
    h0_l = jnp.pad(_s5_state_in(state_ssm[:, l]), [(0, 0)] * 4 + [(0, rl - bl), (0, 0)])
    yf_l, yb_l, _ = _s5_scan(u_lp.reshape(tl_len * rl, SSM_WIDTH), bw, cw, a_blk, h0_l, rl, tl_len,
                             S5_TILE_ROWS // rl)

    at_c = _attention(q_c, ckv_c.reshape(bc, tc_len, KV_RANK), kr_c.reshape(bc, tc_len, HEAD_PAD),
                      w_uk_ext, w_uv_b, bc, tc_len)
    ckv_all = jnp.concatenate([cache_ckv[:, l], ckv_l.reshape(bl, tl_len, KV_RANK)], 1)
    kr_all = jnp.concatenate([_rope_slot(cache_krope[:, l]), kr_l.reshape(bl, tl_len, HEAD_PAD)], 1)
    at_l = _attention(q_l, ckv_all, kr_all, w_uk_ext, w_uv_b, bl, tl_len)

    post = functools.partial(_post, ssm_d=row(ssm_d[l]), w_glu=w_glu_b, w_out_s=w_out_s, w_out_a=w_out_a,
                             norm2=row(norm2[l]), w_query=w_query_b, sub_keys=sub_keys_b)
    tm_c = lambda a: a.reshape(tc_len, bc * SSM_WIDTH)
    tm_l = lambda a: a.reshape(tl_len, rl * SSM_WIDTH)
    x1_c, h2_c, s1_c, s2_c = post(xc, at_c, tm_c(u_c), tm_c(yf_c), tm_c(yb_c), m6, ctx_row,
                                  batch=bc, seq=tc_len)
    x1_l, h2_l, s1_l, s2_l = post(xl, at_l, tm_l(u_lp), tm_l(yf_l), tm_l(yb_l), m6, lat_row,
                                  batch=bl, seq=tl_len)

    x1 = jnp.concatenate([x1_c, x1_l], 0)
    h2 = jnp.concatenate([h2_c, h2_l], 0)
    s1t = jnp.concatenate([s1_c, s1_l], 2)
    s2t = jnp.concatenate([s2_c, s2_l], 2)
    ntok = s1t.shape[2]
    dense = lambda a: a.reshape(PEER_HEADS, N_KEYS, ntok // LANES, LANES)
    stats, t2 = _route(dense(s1t), dense(s2t))
    stats = stats.reshape(4, PEER_HEADS, 1, ntok)
    t2 = t2.reshape(PEER_HEADS, PEER_TOPK, ntok)
    ctx_peer_tiles = bc * tc_len // PEER_TN
    lat_peer_tiles = tl_len // PEER_TN
    peer_row = lambda t: jnp.where(t < ctx_peer_tiles, 0, 1 + (t - ctx_peer_tiles) // lat_peer_tiles)
    y = _peer(h2, s1t, s2t, t2, stats, u_tab, v_tab, x1, m6, peer_row, row(final_norm))

    y_prompt = y[:bc * tc_len].reshape(bc, tc_len, D_MODEL)
    y_sample = y[bc * tc_len:].reshape(bl, tl_len, D_MODEL)
    new_ckv = ckv_c.reshape(bc, 1, tc_len, KV_RANK)
    new_krope = kr_c[:, QK_NOPE:QK_NOPE + QK_ROPE].reshape(bc, 1, tc_len, QK_ROPE)
    new_ssm = _s5_state_out(st_c)[:, None]
    return (y_prompt, y_sample, new_ckv, new_krope, new_ssm)
```

```python
import functools
import math

import jax
import jax.numpy as jnp
import numpy as np
from jax import lax
from jax.experimental import pallas as pl
from jax.experimental.pallas import tpu as pltpu

F32 = jnp.float32
BF16 = jnp.bfloat16

D_MODEL = 1024
GRID_W = 64
EPS = 1e-6
SSM_WIDTH = 512
SSM_GROUP = 16
SSM_GROUPS = 32
SSM_STATE = 64
MLA_HEADS = 8
QK_NOPE = 64
QK_ROPE = 32
V_DIM = 64
Q_RANK = 384
KV_RANK = 256
ROPE_BASE = 10000.0
N_KEYS = 128
N_EXPERTS = N_KEYS * N_KEYS
PEER_HEADS = 8
PEER_TOPK = 16
KEY_DIM = 128

LANES = 128
HEAD_PAD = 128
TOK_TILE = 256
S5_PAIR = 2
S5_BLOCKS = SSM_GROUPS // S5_PAIR
S5_HALF = SSM_WIDTH // 2
S5_BLOCKS_PER_HALF = S5_BLOCKS // 2
S5_TILE_ROWS = 512
PEER_TN = 512
PEER_ROWS = 8
PEER_TE = PEER_ROWS * N_KEYS
PEER_SPAN = 32
VMEM_LIMIT = 48 * 1024 * 1024


def _cparams(sem):
    return pltpu.CompilerParams(dimension_semantics=sem, vmem_limit_bytes=VMEM_LIMIT)


def _rms(x, g):
    return x * lax.rsqrt(jnp.mean(x * x, axis=-1, keepdims=True) + EPS) * g


def _mod_kernel(c_ref, w_ref, b_ref, o_ref):
    o_ref[...] = jnp.dot(jax.nn.silu(c_ref[...]), w_ref[...], preferred_element_type=F32) + b_ref[...]


def _modulation(cvec, w_mod, b_mod):
    rows, d = cvec.shape
    n = w_mod.shape[1]
    tn = 1536
    return pl.pallas_call(
        _mod_kernel,
        grid=(n // tn,),
        in_specs=[pl.BlockSpec((rows, d), lambda j: (0, 0)),
                  pl.BlockSpec((d, tn), lambda j: (0, j)),
                  pl.BlockSpec((1, tn), lambda j: (0, j))],
        out_specs=pl.BlockSpec((rows, tn), lambda j: (0, j)),
        out_shape=jax.ShapeDtypeStruct((rows, n), F32),
        compiler_params=_cparams(("parallel",)),
        name="modulation",
    )(cvec, w_mod, b_mod)


def _pre_kernel(use_rope, x_ref, m_ref, n1_ref, win_ref, qn_ref, kvn_ref, wuq_ref, *rest):
    if use_rope:
        cos_ref, sin_ref, u_ref, q_ref, ckv_ref, kr_ref = rest
    else:
        u_ref, q_ref, ckv_ref, kr_ref = rest
    m = m_ref[0]
    h = _rms(x_ref[...], n1_ref[...]) * (1.0 + m[1:2]) + m[0:1]
    z = jnp.dot(h.astype(BF16), win_ref[...], preferred_element_type=F32)
    u_ref[...] = z[:, :SSM_WIDTH]
    o_q, o_kv, o_kr = SSM_WIDTH, SSM_WIDTH + Q_RANK, SSM_WIDTH + Q_RANK + KV_RANK
    cqn = _rms(z[:, o_q:o_kv], qn_ref[...])
    qq = jnp.dot(cqn.astype(BF16), wuq_ref[...], preferred_element_type=F32)
    ckv_ref[...] = _rms(z[:, o_kv:o_kr], kvn_ref[...])
    kr = z[:, o_kr:o_kr + HEAD_PAD]
    scale = (QK_NOPE + QK_ROPE) ** -0.5
    nq = MLA_HEADS * HEAD_PAD
    if use_rope:
        cos, sin = cos_ref[...], sin_ref[...]
        kr = kr * cos + z[:, o_kr + HEAD_PAD:o_kr + 2 * HEAD_PAD] * sin
        for hd in range(MLA_HEADS):
            sl = slice(hd * HEAD_PAD, (hd + 1) * HEAD_PAD)
            qh = qq[:, sl] * cos + qq[:, nq + hd * HEAD_PAD:nq + (hd + 1) * HEAD_PAD] * sin
            q_ref[:, sl] = (qh * scale).astype(BF16)
    else:
        q_ref[...] = (qq[:, :nq] * scale).astype(BF16)
    kr_ref[...] = kr


def _pre(x2d, m6, mrow, norm1, w_in_ext, q_norm, kv_norm, w_uq_ext, batch, seq, rope):
    ntok = batch * seq
    n_t = seq // TOK_TILE
    use_rope = rope is not None
    full = lambda a: pl.BlockSpec(a.shape, lambda i: (0,) * a.ndim)
    in_specs = [pl.BlockSpec((TOK_TILE, D_MODEL), lambda i: (i, 0)),
                pl.BlockSpec((1, 6, D_MODEL), lambda i: (mrow(i), 0, 0)),
                full(norm1), full(w_in_ext), full(q_norm), full(kv_norm), full(w_uq_ext)]
    args = [x2d, m6, norm1, w_in_ext, q_norm, kv_norm, w_uq_ext]
    if use_rope:
        in_specs += [pl.BlockSpec((TOK_TILE, HEAD_PAD), lambda i: (i % n_t, 0))] * 2
        args += list(rope)
    out_specs = [pl.BlockSpec((TOK_TILE, SSM_WIDTH), lambda i: (i % n_t, i // n_t)),
                 pl.BlockSpec((TOK_TILE, MLA_HEADS * HEAD_PAD), lambda i: (i, 0)),
                 pl.BlockSpec((TOK_TILE, KV_RANK), lambda i: (i, 0)),
                 pl.BlockSpec((TOK_TILE, HEAD_PAD), lambda i: (i, 0))]
    out_shape = [jax.ShapeDtypeStruct((seq, batch * SSM_WIDTH), F32),
                 jax.ShapeDtypeStruct((ntok, MLA_HEADS * HEAD_PAD), BF16),
                 jax.ShapeDtypeStruct((ntok, KV_RANK), F32),
                 jax.ShapeDtypeStruct((ntok, HEAD_PAD), F32)]
    return pl.pallas_call(
        functools.partial(_pre_kernel, use_rope),
        grid=(ntok // TOK_TILE,),
        in_specs=in_specs, out_specs=out_specs, out_shape=out_shape,
        compiler_params=_cparams(("parallel",)),
        name="pre_rope" if use_rope else "pre",
    )(*args)


def _s5_param_kernel(lr_ref, li_ref, ldt_ref, lrx_ref, lix_ref, ldtx_ref, bre_ref, bim_ref, cim_ref,
                     abr_ref, abi_ref, bfr_ref, bfi_ref, ncim_ref):
    def disc(lr, li, ldt):
        dt = jnp.exp(ldt)
        mag = jnp.exp(lr * dt)
        ab_re, ab_im = mag * jnp.cos(li * dt), mag * jnp.sin(li * dt)
        den = lr * lr + li * li
        br, bi = lr / den, -li / den
        ar = ab_re - 1.0
        return ab_re, ab_im, ar * br - ab_im * bi, ar * bi + ab_im * br

    ab_re, ab_im, _, _ = disc(lr_ref[...], li_ref[...], ldt_ref[...])
    abr_ref[...] = ab_re
    abi_ref[...] = ab_im
    _, _, f_re, f_im = disc(lrx_ref[...], lix_ref[...], ldtx_ref[...])
    b_re, b_im = bre_ref[...], bim_ref[...]
    bfr_ref[...] = f_re * b_re - f_im * b_im
    bfi_ref[...] = f_re * b_im + f_im * b_re
    ncim_ref[...] = -cim_ref[...]


def _s5_params(lam_re, lam_im, log_dt, b_re, b_im, c_re, c_im):
    dg = 2 * SSM_GROUPS
    n, p = SSM_STATE, SSM_GROUP
    lr = lam_re.reshape(dg, n)
    li = lam_im.reshape(dg, n)
    ldt = jnp.broadcast_to(log_dt.reshape(dg, 1), (dg, n))
    rep = lambda a: jnp.repeat(a, p, axis=1)
    args = [lr, li, ldt, rep(lr), rep(li), rep(ldt),
            b_re.reshape(dg, n * p), b_im.reshape(dg, n * p), c_im.reshape(dg, p * n)]
    small = jax.ShapeDtypeStruct((dg, n), F32)
    big = jax.ShapeDtypeStruct((dg, n * p), F32)
    ab_re, ab_im, bf_re, bf_im, ncim = pl.pallas_call(
        _s5_param_kernel, out_shape=[small, small, big, big, big], name="s5_params")(*args)

    nb, gp = S5_BLOCKS, S5_PAIR
    eye = jnp.eye(gp, dtype=F32)
    pos = np.arange(nb) % S5_BLOCKS_PER_HALF
    bf = jnp.stack([bf_re, bf_im], 0).reshape(2, 2, nb, gp, n, p)
    bc = jnp.einsum('adbgnp,gh->bdgpahn', bf, eye).reshape(nb, 2, gp * p, 2 * gp * n)
    bw = jnp.zeros((nb, 2, S5_BLOCKS_PER_HALF, gp * p, 2 * gp * n), F32)
    bw = bw.at[np.arange(nb), :, pos].set(bc).reshape(nb, 2, S5_HALF, 2 * gp * n)
    cc = jnp.stack([c_re.reshape(2, nb, gp, p, n), ncim.reshape(2, nb, gp, p, n)], 0)
    cc = jnp.einsum('adbgpn,gh->bdagnhp', cc, eye).reshape(nb, 2, 2 * gp * n, gp * p)
    cw = jnp.zeros((nb, 2, 2 * gp * n, S5_BLOCKS_PER_HALF, gp * p), F32)
    cw = cw.at[np.arange(nb), :, :, pos].set(cc).reshape(nb, 2, 2 * gp * n, S5_HALF)
    a = jnp.stack([ab_re, ab_im], 0).reshape(2, 2, nb, gp * n).transpose(2, 1, 0, 3)
    hb = S5_BLOCKS_PER_HALF
    bw = bw.reshape(2, hb, 2, S5_HALF, 2 * gp * n).transpose(0, 2, 3, 1, 4).reshape(2, 2, S5_HALF, -1)
    cw = cw.reshape(2, hb, 2, 2 * gp * n, S5_HALF).transpose(0, 2, 1, 3, 4).reshape(2, 2, -1, S5_HALF)
    a = a.reshape(2, hb, 2, 2, 1, gp * n).transpose(0, 2, 1, 3, 4, 5)
    return bw.astype(BF16), cw.astype(BF16), a


def _s5_state_in(h):
    bsz = h.shape[0]
    return h.reshape(bsz, 2, 2, 2, S5_BLOCKS_PER_HALF, LANES).transpose(3, 1, 4, 2, 0, 5)


def _s5_state_out(st):
    bsz = st.shape[4]
    return st.transpose(4, 1, 3, 0, 2, 5).reshape(bsz, 2, 2, SSM_GROUPS, SSM_STATE)


def _s5_kernel(rows_per_step, steps, uf_ref, ub_ref, bw_ref, cw_ref, a_ref, h0_ref,
               yf_ref, yb_ref, st_ref, buf, coef, carry):
    r = rows_per_step
    nb = S5_BLOCKS_PER_HALF
    width = 2 * LANES

    @pl.when(pl.program_id(1) == 0)
    def _():
        carry[...] = h0_ref[0]
        coef[...] = jnp.broadcast_to(a_ref[0], coef.shape)

    for d, u_ref in enumerate((uf_ref, ub_ref)):
        buf[d] = jnp.dot(u_ref[...].astype(BF16), bw_ref[0, d], preferred_element_type=F32)

    chains = [(d, b) for d in range(2) for b in range(nb)]
    for g in range(r // 8):
        rows8 = slice(g * 8, (g + 1) * 8)

        def body(k, hs, rows8=rows8):
            out = []
            for (d, b), (h_re, h_im) in zip(chains, hs):
                t = k if d == 0 else steps - 1 - k
                r0 = pl.multiple_of(t * r + g * 8, 8)
                a_re, a_im = coef[d, b, 0], coef[d, b, 1]
                re_l, im_l = slice(b * width, b * width + LANES), slice(b * width + LANES, (b + 1) * width)
                n_re = a_re * h_re - a_im * h_im + buf[d, pl.ds(r0, 8), re_l]
                n_im = a_re * h_im + a_im * h_re + buf[d, pl.ds(r0, 8), im_l]
                buf[d, pl.ds(r0, 8), re_l] = n_re
                buf[d, pl.ds(r0, 8), im_l] = n_im
                out.append((n_re, n_im))
            return tuple(out)

        init = tuple((carry[d, b, 0, rows8], carry[d, b, 1, rows8]) for d, b in chains)
        final = lax.fori_loop(0, steps, body, init)
        for (d, b), (h_re, h_im) in zip(chains, final):
            carry[d, b, 0, rows8] = h_re
            carry[d, b, 1, rows8] = h_im

    for d, y_ref in enumerate((yf_ref, yb_ref)):
        y_ref[...] = jnp.dot(buf[d].astype(BF16), cw_ref[0, d], preferred_element_type=F32)
    st_ref[0] = carry[...]


def _s5_scan(u_tm, bw, cw, a, h0, rows_per_step, seq, steps):
    r = rows_per_step
    n_t = seq // steps
    rows = steps * r
    nb = S5_BLOCKS_PER_HALF
    wide = nb * 2 * LANES
    half4 = lambda h, t: (h, 0, 0, 0)
    half6 = lambda h, t: (h, 0, 0, 0, 0, 0)
    state = (2, nb, 2, r, LANES)
    in_specs = [pl.BlockSpec((rows, S5_HALF), lambda h, t: (t, h)),
                pl.BlockSpec((rows, S5_HALF), lambda h, t: (n_t - 1 - t, h)),
                pl.BlockSpec((1, 2, S5_HALF, wide), half4),
                pl.BlockSpec((1, 2, wide, S5_HALF), half4),
                pl.BlockSpec((1, 2, nb, 2, 1, LANES), half6),
                pl.BlockSpec((1,) + state, half6)]
    out_specs = [pl.BlockSpec((rows, S5_HALF), lambda h, t: (t, h)),
                 pl.BlockSpec((rows, S5_HALF), lambda h, t: (n_t - 1 - t, h)),
                 pl.BlockSpec((1,) + state, half6)]
    out_shape = [jax.ShapeDtypeStruct(u_tm.shape, F32), jax.ShapeDtypeStruct(u_tm.shape, F32),
                 jax.ShapeDtypeStruct((2,) + state, F32)]
    return pl.pallas_call(
        functools.partial(_s5_kernel, r, steps),
        grid=(2, n_t),
        in_specs=in_specs, out_specs=out_specs, out_shape=out_shape,
        scratch_shapes=[pltpu.VMEM((2, rows, wide), F32), pltpu.VMEM((2, nb, 2, 8, LANES), F32),
                        pltpu.VMEM(state, F32)],
        compiler_params=_cparams(("arbitrary", "arbitrary")),
        name="s5_scan",
    )(u_tm, u_tm, bw, cw, a, h0)


def _attn_kernel(q_ref, ckv_ref, kr_ref, wuk_ref, wuv_ref, o_ref, k_s, v_s):
    @pl.when(pl.program_id(1) == 0)
    def _():
        kv = ckv_ref[0].astype(BF16)
        kn = jnp.dot(kv, wuk_ref[...], preferred_element_type=F32)
        kr = kr_ref[0]
        for hd in range(MLA_HEADS):
            k_s[hd] = (kn[:, hd * HEAD_PAD:(hd + 1) * HEAD_PAD] + kr).astype(BF16)
        v_s[...] = jnp.dot(kv, wuv_ref[...], preferred_element_type=F32).astype(BF16)

    for hd in range(MLA_HEADS):
        qh = q_ref[:, hd * HEAD_PAD:(hd + 1) * HEAD_PAD]
        s = lax.dot_general(qh, k_s[hd], (((1,), (1,)), ((), ())), preferred_element_type=F32)
        p = jnp.exp(s - jnp.max(s, axis=-1, keepdims=True))
        l = jnp.sum(p, axis=-1, keepdims=True)
        o = jnp.dot(p.astype(BF16), v_s[:, hd * V_DIM:(hd + 1) * V_DIM], preferred_element_type=F32)
        o_ref[:, hd * V_DIM:(hd + 1) * V_DIM] = (o / l).astype(BF16)


def _attention(q, ckv_all, kr_all, w_uk_ext, w_uv, batch, seq):
    s_len = ckv_all.shape[1]
    n_q = seq // TOK_TILE
    return pl.pallas_call(
        _attn_kernel,
        grid=(batch, n_q),
        in_specs=[pl.BlockSpec((TOK_TILE, MLA_HEADS * HEAD_PAD), lambda b, i: (b * n_q + i, 0)),
                  pl.BlockSpec((1, s_len, KV_RANK), lambda b, i: (b, 0, 0)),
                  pl.BlockSpec((1, s_len, HEAD_PAD), lambda b, i: (b, 0, 0)),
                  pl.BlockSpec(w_uk_ext.shape, lambda b, i: (0, 0)),
                  pl.BlockSpec(w_uv.shape, lambda b, i: (0, 0))],
        out_specs=pl.BlockSpec((TOK_TILE, MLA_HEADS * V_DIM), lambda b, i: (b * n_q + i, 0)),
        out_shape=jax.ShapeDtypeStruct((batch * seq, MLA_HEADS * V_DIM), BF16),
        scratch_shapes=[pltpu.VMEM((MLA_HEADS, s_len, HEAD_PAD), BF16),
                        pltpu.VMEM((s_len, MLA_HEADS * V_DIM), BF16)],
        compiler_params=_cparams(("parallel", "arbitrary")),
        name="attention",
    )(q, ckv_all, kr_all, w_uk_ext, w_uv)


def _post_kernel(n_ctx, xc_ref, atc_ref, uc_ref, yfc_ref, ybc_ref, xl_ref, atl_ref, ul_ref, yfl_ref, ybl_ref,
                 m_ref, d_ref, wglu_ref, wos_ref, woa_ref, n2_ref, wq_ref, sk_ref,
                 x1_ref, h2_ref, s1_ref, s2_ref):
    is_ctx = pl.program_id(0) < n_ctx
    pick = lambda c_ref, l_ref: jnp.where(is_ctx, c_ref[...], l_ref[...])
    m = m_ref[0]
    y = pick(yfc_ref, yfl_ref) + pick(ybc_ref, ybl_ref) + pick(uc_ref, ul_ref) * d_ref[...]
    yg = jax.nn.gelu(y)
    gate = jax.nn.sigmoid(jnp.dot(yg.astype(BF16), wglu_ref[...], preferred_element_type=F32))
    mix = (jnp.dot((yg * gate).astype(BF16), wos_ref[...], preferred_element_type=F32)
           + jnp.dot(pick(atc_ref, atl_ref), woa_ref[...], preferred_element_type=F32))
    x1 = pick(xc_ref, xl_ref) + m[2:3] * mix
    x1_ref[...] = x1
    h2 = (_rms(x1, n2_ref[...]) * (1.0 + m[4:5]) + m[3:4]).astype(BF16)
    h2_ref[...] = h2
    qp = jnp.dot(h2, wq_ref[...], preferred_element_type=F32).astype(BF16)
    for hd in range(PEER_HEADS):
        for half, s_ref in enumerate((s1_ref, s2_ref)):
            c0 = (hd * 2 + half) * KEY_DIM
            s_ref[hd] = lax.dot_general(sk_ref[hd, half], qp[:, c0:c0 + KEY_DIM],
                                        (((1,), (1,)), ((), ())), preferred_element_type=F32)


def _post(ctx, lat, m6, ssm_d, w_glu, w_out_s, w_out_a, norm2, w_query, sub_keys):
    specs, args = [], []
    n_ctx, n_lat = (p[5] * p[6] // TOK_TILE for p in (ctx, lat))
    for pass_id, (x2d, attn, u_tm, yf, yb, batch, seq) in enumerate((ctx, lat)):
        n_t = seq // TOK_TILE
        loc = (lambda i: jnp.minimum(i, n_ctx - 1)) if pass_id == 0 else (lambda i: jnp.maximum(i - n_ctx, 0))
        tok = lambda w, loc=loc: pl.BlockSpec((TOK_TILE, w), lambda i: (loc(i), 0))
        tm = pl.BlockSpec((TOK_TILE, SSM_WIDTH), lambda i, loc=loc, n_t=n_t: (loc(i) % n_t, loc(i) // n_t))
        specs += [tok(D_MODEL), tok(MLA_HEADS * V_DIM), tm, tm, tm]
        args += [x2d, attn, u_tm, yf, yb]
    lat_tiles_per_row = lat[6] // TOK_TILE
    mrow = lambda i: jnp.where(i < n_ctx, 0, 1 + (i - n_ctx) // lat_tiles_per_row)
    full = lambda a: pl.BlockSpec(a.shape, lambda i: (0,) * a.ndim)
    weights = [ssm_d, w_glu, w_out_s, w_out_a, norm2, w_query, sub_keys]
    total = (n_ctx + n_lat) * TOK_TILE
    tok_out = lambda w: pl.BlockSpec((TOK_TILE, w), lambda i: (i, 0))
    sc = pl.BlockSpec((PEER_HEADS, N_KEYS, TOK_TILE), lambda i: (0, 0, i))
    return pl.pallas_call(
        functools.partial(_post_kernel, n_ctx),
        grid=(n_ctx + n_lat,),
        in_specs=specs + [pl.BlockSpec((1, 6, D_MODEL), lambda i: (mrow(i), 0, 0))] + [full(w) for w in weights],
        out_specs=[tok_out(D_MODEL), tok_out(D_MODEL), sc, sc],
        out_shape=[jax.ShapeDtypeStruct((total, D_MODEL), F32), jax.ShapeDtypeStruct((total, D_MODEL), BF16),
                   jax.ShapeDtypeStruct((PEER_HEADS, N_KEYS, total), F32),
                   jax.ShapeDtypeStruct((PEER_HEADS, N_KEYS, total), F32)],
        compiler_params=_cparams(("arbitrary",)),
        name="post",
    )(*args, m6, *weights)


def _sort_pairs(lo, hi):
    def merge(lo, hi, r):
        step = r * 2
        if step < hi - lo:
            yield from merge(lo, hi, step)
            yield from merge(lo + r, hi, step)
            for i in range(lo + r, hi - r, step):
                yield (i, i + r)
        else:
            yield (lo, lo + r)

    if hi - lo >= 1:
        mid = lo + (hi - lo) // 2
        yield from _sort_pairs(lo, mid)
        yield from _sort_pairs(mid + 1, hi)
        yield from merge(lo, hi, 1)


def _sort_desc(vals):
    vals = list(vals)
    for i, j in _sort_pairs(0, len(vals) - 1):
        vals[i], vals[j] = jnp.maximum(vals[i], vals[j]), jnp.minimum(vals[i], vals[j])
    return vals


def _merge_top(a, b):
    n = len(a)
    c = [jnp.maximum(a[k], b[n - 1 - k]) for k in range(n)]
    stride = n // 2
    while stride:
        for i in range(n):
            if not i & stride:
                c[i], c[i + stride] = jnp.maximum(c[i], c[i + stride]), jnp.minimum(c[i], c[i + stride])
        stride //= 2
    return c


def _top_keys(ref):
    k = PEER_TOPK
    groups = [_sort_desc([ref[0, g * k + r] for r in range(k)]) for g in range(N_KEYS // k)]
    while len(groups) > 1:
        groups = [_merge_top(groups[i], groups[i + 1]) for i in range(0, len(groups), 2)]
    return groups[0]


def _route_kernel(s1_ref, s2_ref, o_ref, t2_ref):
    k = PEER_TOPK
    t1, t2 = _top_keys(s1_ref), _top_keys(s2_ref)
    cands = [t1[a] + t2[b] for a in range(k) for b in range(k) if (a + 1) * (b + 1) <= k]
    size = 1 << (len(cands) - 1).bit_length()
    cands += [jnp.full_like(t1[0], -jnp.inf)] * (size - len(cands))
    best = _sort_desc(cands)
    z = jnp.ones_like(best[0])
    for v in best[1:k]:
        z = z + jnp.exp(v - best[0])
    o_ref[0, 0] = best[k - 1]
    o_ref[1, 0] = t1[0]
    o_ref[2, 0] = t2[0]
    o_ref[3, 0] = 1.0 / z
    for b in range(k):
        t2_ref[0, b] = t2[b]


def _route(s1d, s2d):
    n_rows = s1d.shape[2]
    spec = pl.BlockSpec((1, N_KEYS, 8, LANES), lambda i, h: (h, 0, i, 0))
    return pl.pallas_call(
        _route_kernel,
        grid=(n_rows // 8, PEER_HEADS),
        in_specs=[spec, spec],
        out_specs=[pl.BlockSpec((4, 1, 8, LANES), lambda i, h: (0, h, i, 0)),
                   pl.BlockSpec((1, PEER_TOPK, 8, LANES), lambda i, h: (h, 0, i, 0))],
        out_shape=[jax.ShapeDtypeStruct((4, PEER_HEADS, n_rows, LANES), F32),
                   jax.ShapeDtypeStruct((PEER_HEADS, PEER_TOPK, n_rows, LANES), F32)],
        compiler_params=_cparams(("parallel", "parallel")),
        name="peer_route",
    )(s1d, s2d)


def _cast_kernel(u_ref, v_ref, ub_ref, vb_ref):
    ub_ref[...] = u_ref[...].astype(BF16)
    vb_ref[...] = v_ref[...].astype(BF16)


def _cast_tables(u_table, v_table):
    rows = 2048
    spec = pl.BlockSpec((rows, D_MODEL), lambda i: (i, 0))
    out = jax.ShapeDtypeStruct(u_table.shape, BF16)
    return pl.pallas_call(
        _cast_kernel, grid=(u_table.shape[0] // rows,), in_specs=[spec, spec], out_specs=[spec, spec],
        out_shape=[out, out], compiler_params=_cparams(("parallel",)), name="cast_tables")(u_table, v_table)


def _peer_kernel(h2_ref, s1_ref, s2_ref, t2_ref, st_ref, u_ref, v_ref, x1_ref, m_ref, fn_ref, y_ref,
                 rank_s, e2_s, count_s, gain_s, act_s, aw_s, acc_s):
    k = pl.program_id(1)
    chunks = [slice(c * LANES, (c + 1) * LANES) for c in range(PEER_TN // LANES)]

    @pl.when(k == 0)
    def _():
        for hd in range(PEER_HEADS):
            for cs in chunks:
                s2 = s2_ref[hd, :, cs]
                rank = jnp.zeros_like(s2)
                for b in range(PEER_TOPK):
                    rank = rank + jnp.where(s2 < t2_ref[hd, b:b + 1, cs], 1.0, 0.0)
                rank_s[hd, :, cs] = rank
                e2_s[hd, :, cs] = jnp.exp(s2 - st_ref[2, hd, :, cs])
        acc_s[...] = jnp.zeros_like(acc_s)

    def prepare():
        for hd in range(PEER_HEADS):
            s1 = s1_ref[hd]
            cnt = jnp.zeros_like(s1)
            for b in range(PEER_TOPK):
                cnt = cnt + jnp.where(s1 + t2_ref[hd, b:b + 1, :] >= st_ref[0, hd], 1.0, 0.0)
            gain = jnp.exp(s1 - st_ref[1, hd]) * st_ref[3, hd]
            for ii in range(PEER_ROWS):
                count_s[ii, hd] = cnt[ii:ii + 1]
                gain_s[ii, hd] = gain[ii:ii + 1]

    def weigh(j, _):
        keys = pl.ds(pl.multiple_of(j * PEER_SPAN, PEER_SPAN), PEER_SPAN)
        for cs in chunks:
            w = [None] * PEER_ROWS
            for hd in range(PEER_HEADS):
                rank, e2 = rank_s[hd, keys, cs], e2_s[hd, keys, cs]
                for ii in range(PEER_ROWS):
                    term = jnp.where(rank < count_s[ii, hd, :, cs], e2, 0.0) * gain_s[ii, hd, :, cs]
                    w[ii] = term if w[ii] is None else w[ii] + term
            for ii in range(PEER_ROWS):
                rows = pl.ds(pl.multiple_of(ii * N_KEYS + j * PEER_SPAN, PEER_SPAN), PEER_SPAN)
                aw_s[rows, cs] = (jax.nn.gelu(act_s[rows, cs]) * w[ii]).astype(BF16)
        return 0

    prepare()
    act_s[...] = lax.dot_general(u_ref[...], h2_ref[...], (((1,), (1,)), ((), ())),
                                 preferred_element_type=F32)
    lax.fori_loop(0, N_KEYS // PEER_SPAN, weigh, 0)
    acc_s[...] += lax.dot_general(aw_s[...], v_ref[...], (((0,), (0,)), ((), ())),
                                  preferred_element_type=F32)

    @pl.when(k == pl.num_programs(1) - 1)
    def _():
        y_ref[...] = _rms(x1_ref[...] + m_ref[0][5:6] * acc_s[...], fn_ref[...])


def _peer(h2, s1t, s2t, t2, stats, u_tab, v_tab, x1, m6, mrow, final_norm):
    ntok = h2.shape[0]
    tn, te = PEER_TN, PEER_TE
    tok = pl.BlockSpec((tn, D_MODEL), lambda t, k: (t, 0))
    tab = pl.BlockSpec((te, D_MODEL), lambda t, k: (k, 0))
    keys = pltpu.VMEM((PEER_HEADS, N_KEYS, tn), F32)
    rows = pltpu.VMEM((PEER_ROWS, PEER_HEADS, 1, tn), F32)
    return pl.pallas_call(
        _peer_kernel,
        grid=(ntok // tn, N_EXPERTS // te),
        in_specs=[tok,
                  pl.BlockSpec((PEER_HEADS, PEER_ROWS, tn), lambda t, k: (0, k, t)),
                  pl.BlockSpec((PEER_HEADS, N_KEYS, tn), lambda t, k: (0, 0, t)),
                  pl.BlockSpec((PEER_HEADS, PEER_TOPK, tn), lambda t, k: (0, 0, t)),
                  pl.BlockSpec((4, PEER_HEADS, 1, tn), lambda t, k: (0, 0, 0, t)),
                  tab, tab, tok,
                  pl.BlockSpec((1, 6, D_MODEL), lambda t, k: (mrow(t), 0, 0)),
                  pl.BlockSpec((1, D_MODEL), lambda t, k: (0, 0))],
        out_specs=tok,
        out_shape=jax.ShapeDtypeStruct((ntok, D_MODEL), F32),
        scratch_shapes=[keys, keys, rows, rows, pltpu.VMEM((te, tn), F32), pltpu.VMEM((te, tn), BF16),
                        pltpu.VMEM((tn, D_MODEL), F32)],
        compiler_params=_cparams(("parallel", "arbitrary")),
        name="peer_dense",
    )(h2, s1t, s2t, t2, stats, u_tab, v_tab, x1, m6, final_norm)


def _rot_cols(w):
    j = np.arange(QK_ROPE)
    first = (j % (QK_ROPE // 2)) < (QK_ROPE // 4)
    perm = np.where(first, j + QK_ROPE // 4, j - QK_ROPE // 4)
    sign = np.where(first, -1.0, 1.0).astype(np.float32)
    return w[..., perm] * sign


def _rope_slot(w_rope):
    pad = [(0, 0)] * (w_rope.ndim - 1)
    return jnp.pad(w_rope, pad + [(QK_NOPE, HEAD_PAD - QK_NOPE - QK_ROPE)])


def _rope_tables(n_tokens):
    n_rows = n_tokens // GRID_W
    rows = jnp.repeat(jnp.arange(n_rows, dtype=F32), GRID_W)
    cols = jnp.tile(jnp.arange(GRID_W, dtype=F32), n_rows)
    half = QK_ROPE // 2
    inv_freq = 1.0 / (ROPE_BASE ** (jnp.arange(0, half, 2, dtype=F32) / half))
    ang_r = rows[:, None] * inv_freq
    ang_c = cols[:, None] * inv_freq
    ang = jnp.concatenate([ang_r, ang_r, ang_c, ang_c], axis=-1)
    lead = (QK_NOPE, HEAD_PAD - QK_NOPE - QK_ROPE)
    cos = jnp.pad(jnp.cos(ang), [(0, 0), lead], constant_values=1.0)
    sin = jnp.pad(jnp.sin(ang), [(0, 0), lead])
    return cos, sin


def kernel(x_prompt, x_sample, c, cache_ckv, cache_krope, state_ssm, c_ctx, w_mod, b_mod, norm1, w_in,
           ssm_lam_re, ssm_lam_im, ssm_log_dt, ssm_b_re, ssm_b_im, ssm_c_re, ssm_c_im, ssm_d, w_glu,
           q_norm, w_uq, kv_norm, w_uk, w_uv, w_out, norm2, w_query, sub_keys, u_table, v_table,
           final_norm):
    bc, tc_len, _ = x_prompt.shape
    bl, tl_len, _ = x_sample.shape
    l = 0
    row = lambda a: a.reshape(1, -1)

    n_mod = 8
    cvec = jnp.concatenate([c_ctx[None], c, jnp.zeros((n_mod - 1 - bl, D_MODEL), F32)], 0)
    m6 = _modulation(cvec, w_mod[l], row(b_mod[l])).reshape(n_mod, 6, D_MODEL)

    wi = w_in[l]
    o_kr = SSM_WIDTH + Q_RANK + KV_RANK
    w_kr = wi[:, o_kr:]
    w_in_ext = jnp.concatenate([wi[:, :o_kr], _rope_slot(w_kr), _rope_slot(_rot_cols(w_kr))], 1).astype(BF16)
    wq3 = w_uq[l].reshape(Q_RANK, MLA_HEADS, QK_NOPE + QK_ROPE)
    wq_main = jnp.pad(wq3, [(0, 0), (0, 0), (0, HEAD_PAD - QK_NOPE - QK_ROPE)])
    wq_rot = _rope_slot(_rot_cols(wq3[..., QK_NOPE:]))
    w_uq_ext = jnp.concatenate([wq_main.reshape(Q_RANK, -1), wq_rot.reshape(Q_RANK, -1)], 1).astype(BF16)
    w_uk_ext = jnp.pad(w_uk[l].reshape(KV_RANK, MLA_HEADS, QK_NOPE),
                       [(0, 0), (0, 0), (0, HEAD_PAD - QK_NOPE)]).reshape(KV_RANK, -1).astype(BF16)
    w_uv_b = w_uv[l].astype(BF16)
    w_glu_b = w_glu[l].astype(BF16)
    w_out_s = w_out[l][:SSM_WIDTH].astype(BF16)
    w_out_a = w_out[l][SSM_WIDTH:].astype(BF16)
    w_query_b = w_query[l].astype(BF16)
    sub_keys_b = sub_keys[l].astype(BF16)
    u_tab, v_tab = _cast_tables(u_table[l], v_table[l])
    cos, sin = _rope_tables(tl_len)

    bw, cw, a_blk = _s5_params(ssm_lam_re[l], ssm_lam_im[l], ssm_log_dt[l], ssm_b_re[l], ssm_b_im[l],
                               ssm_c_re[l], ssm_c_im[l])

    ctx_row = lambda i: 0
    lat_tiles = tl_len // TOK_TILE
    lat_row = lambda i: 1 + i // lat_tiles
    xc = x_prompt.reshape(bc * tc_len, D_MODEL)
    xl = x_sample.reshape(bl * tl_len, D_MODEL)

    u_c, q_c, ckv_c, kr_c = _pre(xc, m6, ctx_row, row(norm1[l]), w_in_ext, row(q_norm[l]), row(kv_norm[l]),
                                 w_uq_ext[:, :MLA_HEADS * HEAD_PAD], bc, tc_len, None)
    u_l, q_l, ckv_l, kr_l = _pre(xl, m6, lat_row, row(norm1[l]), w_in_ext, row(q_norm[l]), row(kv_norm[l]),
                                 w_uq_ext, bl, tl_len, (cos, sin))

    rl = 8
    h0_c = jnp.zeros((2, 2, S5_BLOCKS_PER_HALF, 2, bc, LANES), F32)
    yf_c, yb_c, st_c = _s5_scan(u_c.reshape(tc_len * bc, SSM_WIDTH), bw, cw, a_blk, h0_c, bc, tc_len,
                                S5_TILE_ROWS // bc)
    u_lp = jnp.pad(u_l.reshape(tl_len, bl, SSM_WIDTH), [(0, 0), (0, rl - bl), (0, 0)])
    h0_l = jnp.pad(_s5_state_in(state_ssm[:, l]), [(0, 0)] * 4 + [(0, rl - bl), (0, 0)])
    yf_l, yb_l, _ = _s5_scan(u_lp.reshape(tl_len * rl, SSM_WIDTH), bw, cw, a_blk, h0_l, rl, tl_len,
                             S5_TILE_ROWS // rl)

    at_c = _attention(q_c, ckv_c.reshape(bc, tc_len, KV_RANK), kr_c.reshape(bc, tc_len, HEAD_PAD),
                      w_uk_ext, w_uv_b, bc, tc_len)
    ckv_all = jnp.concatenate([cache_ckv[:, l], ckv_l.reshape(bl, tl_len, KV_RANK)], 1)
    kr_all = jnp.concatenate([_rope_slot(cache_krope[:, l]), kr_l.reshape(bl, tl_len, HEAD_PAD)], 1)
    at_l = _attention(q_l, ckv_all, kr_all, w_uk_ext, w_uv_b, bl, tl_len)

    tm_c = lambda a: a.reshape(tc_len, bc * SSM_WIDTH)
    tm_l = lambda a: a.reshape(tl_len, rl * SSM_WIDTH)
    ntok = bc * tc_len + bl * tl_len
    x1, h2, s1t, s2t = _post((xc, at_c, tm_c(u_c), tm_c(yf_c), tm_c(yb_c), bc, tc_len),
                             (xl, at_l, tm_l(u_lp), tm_l(yf_l), tm_l(yb_l), bl, tl_len),
                             m6, row(ssm_d[l]), w_glu_b, w_out_s, w_out_a, row(norm2[l]), w_query_b, sub_keys_b)

    dense = lambda a: a.reshape(PEER_HEADS, N_KEYS, ntok // LANES, LANES)
    stats, t2 = _route(dense(s1t), dense(s2t))
    stats = stats.reshape(4, PEER_HEADS, 1, ntok)
    t2 = t2.reshape(PEER_HEADS, PEER_TOPK, ntok)
    ctx_peer_tiles = bc * tc_len // PEER_TN
    lat_peer_tiles = tl_len // PEER_TN
    peer_row = lambda t: jnp.where(t < ctx_peer_tiles, 0, 1 + (t - ctx_peer_tiles) // lat_peer_tiles)
    y = _peer(h2, s1t, s2t, t2, stats, u_tab, v_tab, x1, m6, peer_row, row(final_norm))

    y_prompt = y[:bc * tc_len].reshape(bc, tc_len, D_MODEL)
    y_sample = y[bc * tc_len:].reshape(bl, tl_len, D_MODEL)
    new_ckv = ckv_c.reshape(bc, 1, tc_len, KV_RANK)
    new_krope = kr_c[:, QK_NOPE:QK_NOPE + QK_ROPE].reshape(bc, 1, tc_len, QK_ROPE)
    new_ssm = _s5_state_out(st_c)[:, None]
    return (y_prompt, y_sample, new_ckv, new_krope, new_ssm)
```

```python
import functools
import math

import jax
import jax.numpy as jnp
import numpy as np
from jax import lax
from jax.experimental import pallas as pl
from jax.experimental.pallas import tpu as pltpu

F32 = jnp.float32
BF16 = jnp.bfloat16

D_MODEL = 1024
GRID_W = 64
EPS = 1e-6
SSM_WIDTH = 512
SSM_GROUP = 16
SSM_GROUPS = 32
SSM_STATE = 64
MLA_HEADS = 8
QK_NOPE = 64
QK_ROPE = 32
V_DIM = 64
Q_RANK = 384
KV_RANK = 256
ROPE_BASE = 10000.0
N_KEYS = 128
N_EXPERTS = N_KEYS * N_KEYS
PEER_HEADS = 8
PEER_TOPK = 16
KEY_DIM = 128

LANES = 128
HEAD_PAD = 128
TOK_TILE = 256
S5_PAIR = 2
S5_BLOCKS = SSM_GROUPS // S5_PAIR
S5_HALF = SSM_WIDTH // 2
S5_BLOCKS_PER_HALF = S5_BLOCKS // 2
S5_TILE_ROWS = 512
PEER_TN = 512
PEER_ROWS = 8
PEER_TE = PEER_ROWS * N_KEYS
PEER_SPAN = 32
VMEM_LIMIT = 48 * 1024 * 1024


def _cparams(sem):
    return pltpu.CompilerParams(dimension_semantics=sem, vmem_limit_bytes=VMEM_LIMIT)


def _rms(x, g):
    return x * lax.rsqrt(jnp.mean(x * x, axis=-1, keepdims=True) + EPS) * g


def _mod_kernel(c_ref, w_ref, b_ref, o_ref):
    o_ref[...] = jnp.dot(jax.nn.silu(c_ref[...]), w_ref[...], preferred_element_type=F32) + b_ref[...]


def _modulation(cvec, w_mod, b_mod):
    rows, d = cvec.shape
    n = w_mod.shape[1]
    tn = 1536
    return pl.pallas_call(
        _mod_kernel,
        grid=(n // tn,),
        in_specs=[pl.BlockSpec((rows, d), lambda j: (0, 0)),
                  pl.BlockSpec((d, tn), lambda j: (0, j)),
                  pl.BlockSpec((1, tn), lambda j: (0, j))],
        out_specs=pl.BlockSpec((rows, tn), lambda j: (0, j)),
        out_shape=jax.ShapeDtypeStruct((rows, n), F32),
        compiler_params=_cparams(("parallel",)),
        name="modulation",
    )(cvec, w_mod, b_mod)


def _pre_kernel(use_rope, x_ref, m_ref, n1_ref, win_ref, qn_ref, kvn_ref, wuq_ref, *rest):
    if use_rope:
        cos_ref, sin_ref, u_ref, q_ref, ckv_ref, kr_ref = rest
    else:
        u_ref, q_ref, ckv_ref, kr_ref = rest
    m = m_ref[0]
    h = _rms(x_ref[...], n1_ref[...]) * (1.0 + m[1:2]) + m[0:1]
    z = jnp.dot(h.astype(BF16), win_ref[...], preferred_element_type=F32)
    u_ref[...] = z[:, :SSM_WIDTH]
    o_q, o_kv, o_kr = SSM_WIDTH, SSM_WIDTH + Q_RANK, SSM_WIDTH + Q_RANK + KV_RANK
    cqn = _rms(z[:, o_q:o_kv], qn_ref[...])
    qq = jnp.dot(cqn.astype(BF16), wuq_ref[...], preferred_element_type=F32)
    ckv_ref[...] = _rms(z[:, o_kv:o_kr], kvn_ref[...])
    kr = z[:, o_kr:o_kr + HEAD_PAD]
    scale = (QK_NOPE + QK_ROPE) ** -0.5
    nq = MLA_HEADS * HEAD_PAD
    if use_rope:
        cos, sin = cos_ref[...], sin_ref[...]
        kr = kr * cos + z[:, o_kr + HEAD_PAD:o_kr + 2 * HEAD_PAD] * sin
        for hd in range(MLA_HEADS):
            sl = slice(hd * HEAD_PAD, (hd + 1) * HEAD_PAD)
            qh = qq[:, sl] * cos + qq[:, nq + hd * HEAD_PAD:nq + (hd + 1) * HEAD_PAD] * sin
            q_ref[:, sl] = (qh * scale).astype(BF16)
    else:
        q_ref[...] = (qq[:, :nq] * scale).astype(BF16)
    kr_ref[...] = kr


def _pre(x2d, m6, mrow, norm1, w_in_ext, q_norm, kv_norm, w_uq_ext, batch, seq, rope):
    ntok = batch * seq
    n_t = seq // TOK_TILE
    use_rope = rope is not None
    full = lambda a: pl.BlockSpec(a.shape, lambda i: (0,) * a.ndim)
    in_specs = [pl.BlockSpec((TOK_TILE, D_MODEL), lambda i: (i, 0)),
                pl.BlockSpec((1, 6, D_MODEL), lambda i: (mrow(i), 0, 0)),
                full(norm1), full(w_in_ext), full(q_norm), full(kv_norm), full(w_uq_ext)]
    args = [x2d, m6, norm1, w_in_ext, q_norm, kv_norm, w_uq_ext]
    if use_rope:
        in_specs += [pl.BlockSpec((TOK_TILE, HEAD_PAD), lambda i: (i % n_t, 0))] * 2
        args += list(rope)
    out_specs = [pl.BlockSpec((TOK_TILE, SSM_WIDTH), lambda i: (i % n_t, i // n_t)),
                 pl.BlockSpec((TOK_TILE, MLA_HEADS * HEAD_PAD), lambda i: (i, 0)),
                 pl.BlockSpec((TOK_TILE, KV_RANK), lambda i: (i, 0)),
                 pl.BlockSpec((TOK_TILE, HEAD_PAD), lambda i: (i, 0))]
    out_shape = [jax.ShapeDtypeStruct((seq, batch * SSM_WIDTH), F32),
                 jax.ShapeDtypeStruct((ntok, MLA_HEADS * HEAD_PAD), BF16),
                 jax.ShapeDtypeStruct((ntok, KV_RANK), F32),
                 jax.ShapeDtypeStruct((ntok, HEAD_PAD), F32)]
    return pl.pallas_call(
        functools.partial(_pre_kernel, use_rope),
        grid=(ntok // TOK_TILE,),
        in_specs=in_specs, out_specs=out_specs, out_shape=out_shape,
        compiler_params=_cparams(("parallel",)),
        name="pre_rope" if use_rope else "pre",
    )(*args)


def _s5_param_kernel(lr_ref, li_ref, ldt_ref, lrx_ref, lix_ref, ldtx_ref, bre_ref, bim_ref, cim_ref,
                     abr_ref, abi_ref, bfr_ref, bfi_ref, ncim_ref):
    def disc(lr, li, ldt):
        dt = jnp.exp(ldt)
        mag = jnp.exp(lr * dt)
        ab_re, ab_im = mag * jnp.cos(li * dt), mag * jnp.sin(li * dt)
        den = lr * lr + li * li
        br, bi = lr / den, -li / den
        ar = ab_re - 1.0
        return ab_re, ab_im, ar * br - ab_im * bi, ar * bi + ab_im * br

    ab_re, ab_im, _, _ = disc(lr_ref[...], li_ref[...], ldt_ref[...])
    abr_ref[...] = ab_re
    abi_ref[...] = ab_im
    _, _, f_re, f_im = disc(lrx_ref[...], lix_ref[...], ldtx_ref[...])
    b_re, b_im = bre_ref[...], bim_ref[...]
    bfr_ref[...] = f_re * b_re - f_im * b_im
    bfi_ref[...] = f_re * b_im + f_im * b_re
    ncim_ref[...] = -cim_ref[...]


def _s5_params(lam_re, lam_im, log_dt, b_re, b_im, c_re, c_im):
    dg = 2 * SSM_GROUPS
    n, p = SSM_STATE, SSM_GROUP
    lr = lam_re.reshape(dg, n)
    li = lam_im.reshape(dg, n)
    ldt = jnp.broadcast_to(log_dt.reshape(dg, 1), (dg, n))
    rep = lambda a: jnp.repeat(a, p, axis=1)
    args = [lr, li, ldt, rep(lr), rep(li), rep(ldt),
            b_re.reshape(dg, n * p), b_im.reshape(dg, n * p), c_im.reshape(dg, p * n)]
    small = jax.ShapeDtypeStruct((dg, n), F32)
    big = jax.ShapeDtypeStruct((dg, n * p), F32)
    ab_re, ab_im, bf_re, bf_im, ncim = pl.pallas_call(
        _s5_param_kernel, out_shape=[small, small, big, big, big], name="s5_params")(*args)

    nb, gp = S5_BLOCKS, S5_PAIR
    eye = jnp.eye(gp, dtype=F32)
    pos = np.arange(nb) % S5_BLOCKS_PER_HALF
    bf = jnp.stack([bf_re, bf_im], 0).reshape(2, 2, nb, gp, n, p)
    bc = jnp.einsum('adbgnp,gh->bdgpahn', bf, eye).reshape(nb, 2, gp * p, 2 * gp * n)
    bw = jnp.zeros((nb, 2, S5_BLOCKS_PER_HALF, gp * p, 2 * gp * n), F32)
    bw = bw.at[np.arange(nb), :, pos].set(bc).reshape(nb, 2, S5_HALF, 2 * gp * n)
    cc = jnp.stack([c_re.reshape(2, nb, gp, p, n), ncim.reshape(2, nb, gp, p, n)], 0)
    cc = jnp.einsum('adbgpn,gh->bdagnhp', cc, eye).reshape(nb, 2, 2 * gp * n, gp * p)
    cw = jnp.zeros((nb, 2, 2 * gp * n, S5_BLOCKS_PER_HALF, gp * p), F32)
    cw = cw.at[np.arange(nb), :, :, pos].set(cc).reshape(nb, 2, 2 * gp * n, S5_HALF)
    a = jnp.stack([ab_re, ab_im], 0).reshape(2, 2, nb, gp * n).transpose(2, 1, 0, 3)
    hb = S5_BLOCKS_PER_HALF
    bw = bw.reshape(2, hb, 2, S5_HALF, 2 * gp * n).transpose(0, 2, 3, 1, 4).reshape(2, 2, S5_HALF, -1)
    cw = cw.reshape(2, hb, 2, 2 * gp * n, S5_HALF).transpose(0, 2, 1, 3, 4).reshape(2, 2, -1, S5_HALF)
    a = a.reshape(2, hb, 2, 2, 1, gp * n).transpose(0, 2, 1, 3, 4, 5)
    return bw.astype(BF16), cw.astype(BF16), a


def _s5_state_in(h):
    bsz = h.shape[0]
    return h.reshape(bsz, 2, 2, 2, S5_BLOCKS_PER_HALF, LANES).transpose(3, 1, 4, 2, 0, 5)


def _s5_state_out(st):
    bsz = st.shape[4]
    return st.transpose(4, 1, 3, 0, 2, 5).reshape(bsz, 2, 2, SSM_GROUPS, SSM_STATE)


def _s5_kernel(rows_per_step, steps, uf_ref, ub_ref, bw_ref, cw_ref, a_ref, h0_ref,
               yf_ref, yb_ref, st_ref, buf, coef, carry):
    r = rows_per_step
    nb = S5_BLOCKS_PER_HALF
    width = 2 * LANES

    @pl.when(pl.program_id(1) == 0)
    def _():
        carry[...] = h0_ref[0]
        coef[...] = jnp.broadcast_to(a_ref[0], coef.shape)

    for d, u_ref in enumerate((uf_ref, ub_ref)):
        buf[d] = jnp.dot(u_ref[...].astype(BF16), bw_ref[0, d], preferred_element_type=F32)

    chains = [(d, b) for d in range(2) for b in range(nb)]
    for g in range(r // 8):
        rows8 = slice(g * 8, (g + 1) * 8)

        def body(k, hs, rows8=rows8):
            out = []
            for (d, b), (h_re, h_im) in zip(chains, hs):
                t = k if d == 0 else steps - 1 - k
                r0 = pl.multiple_of(t * r + g * 8, 8)
                a_re, a_im = coef[d, b, 0], coef[d, b, 1]
                re_l, im_l = slice(b * width, b * width + LANES), slice(b * width + LANES, (b + 1) * width)
                n_re = a_re * h_re - a_im * h_im + buf[d, pl.ds(r0, 8), re_l]
                n_im = a_re * h_im + a_im * h_re + buf[d, pl.ds(r0, 8), im_l]
                buf[d, pl.ds(r0, 8), re_l] = n_re
                buf[d, pl.ds(r0, 8), im_l] = n_im
                out.append((n_re, n_im))
            return tuple(out)

        init = tuple((carry[d, b, 0, rows8], carry[d, b, 1, rows8]) for d, b in chains)
        final = lax.fori_loop(0, steps, body, init)
        for (d, b), (h_re, h_im) in zip(chains, final):
            carry[d, b, 0, rows8] = h_re
            carry[d, b, 1, rows8] = h_im

    for d, y_ref in enumerate((yf_ref, yb_ref)):
        y_ref[...] = jnp.dot(buf[d].astype(BF16), cw_ref[0, d], preferred_element_type=F32)
    st_ref[0] = carry[...]


def _s5_scan(u_tm, bw, cw, a, h0, rows_per_step, seq, steps):
    r = rows_per_step
    n_t = seq // steps
    rows = steps * r
    nb = S5_BLOCKS_PER_HALF
    wide = nb * 2 * LANES
    half4 = lambda h, t: (h, 0, 0, 0)
    half6 = lambda h, t: (h, 0, 0, 0, 0, 0)
    state = (2, nb, 2, r, LANES)
    in_specs = [pl.BlockSpec((rows, S5_HALF), lambda h, t: (t, h)),
                pl.BlockSpec((rows, S5_HALF), lambda h, t: (n_t - 1 - t, h)),
                pl.BlockSpec((1, 2, S5_HALF, wide), half4),
                pl.BlockSpec((1, 2, wide, S5_HALF), half4),
                pl.BlockSpec((1, 2, nb, 2, 1, LANES), half6),
                pl.BlockSpec((1,) + state, half6)]
    out_specs = [pl.BlockSpec((rows, S5_HALF), lambda h, t: (t, h)),
                 pl.BlockSpec((rows, S5_HALF), lambda h, t: (n_t - 1 - t, h)),
                 pl.BlockSpec((1,) + state, half6)]
    out_shape = [jax.ShapeDtypeStruct(u_tm.shape, F32), jax.ShapeDtypeStruct(u_tm.shape, F32),
                 jax.ShapeDtypeStruct((2,) + state, F32)]
    return pl.pallas_call(
        functools.partial(_s5_kernel, r, steps),
        grid=(2, n_t),
        in_specs=in_specs, out_specs=out_specs, out_shape=out_shape,
        scratch_shapes=[pltpu.VMEM((2, rows, wide), F32), pltpu.VMEM((2, nb, 2, 8, LANES), F32),
                        pltpu.VMEM(state, F32)],
        compiler_params=_cparams(("arbitrary", "arbitrary")),
        name="s5_scan",
    )(u_tm, u_tm, bw, cw, a, h0)


def _attn_kernel(q_ref, ckv_ref, kr_ref, wuk_ref, wuv_ref, o_ref, k_s, v_s):
    @pl.when(pl.program_id(1) == 0)
    def _():
        kv = ckv_ref[0].astype(BF16)
        kn = jnp.dot(kv, wuk_ref[...], preferred_element_type=F32)
        kr = kr_ref[0]
        for hd in range(MLA_HEADS):
            k_s[hd] = (kn[:, hd * HEAD_PAD:(hd + 1) * HEAD_PAD] + kr).astype(BF16)
        v_s[...] = jnp.dot(kv, wuv_ref[...], preferred_element_type=F32).astype(BF16)

    for hd in range(MLA_HEADS):
        qh = q_ref[:, hd * HEAD_PAD:(hd + 1) * HEAD_PAD]
        s = lax.dot_general(qh, k_s[hd], (((1,), (1,)), ((), ())), preferred_element_type=F32)
        p = jnp.exp(s - jnp.max(s, axis=-1, keepdims=True))
        l = jnp.sum(p, axis=-1, keepdims=True)
        o = jnp.dot(p.astype(BF16), v_s[:, hd * V_DIM:(hd + 1) * V_DIM], preferred_element_type=F32)
        o_ref[:, hd * V_DIM:(hd + 1) * V_DIM] = (o / l).astype(BF16)


def _attention(q, ckv_all, kr_all, w_uk_ext, w_uv, batch, seq):
    s_len = ckv_all.shape[1]
    n_q = seq // TOK_TILE
    return pl.pallas_call(
        _attn_kernel,
        grid=(batch, n_q),
        in_specs=[pl.BlockSpec((TOK_TILE, MLA_HEADS * HEAD_PAD), lambda b, i: (b * n_q + i, 0)),
                  pl.BlockSpec((1, s_len, KV_RANK), lambda b, i: (b, 0, 0)),
                  pl.BlockSpec((1, s_len, HEAD_PAD), lambda b, i: (b, 0, 0)),
                  pl.BlockSpec(w_uk_ext.shape, lambda b, i: (0, 0)),
                  pl.BlockSpec(w_uv.shape, lambda b, i: (0, 0))],
        out_specs=pl.BlockSpec((TOK_TILE, MLA_HEADS * V_DIM), lambda b, i: (b * n_q + i, 0)),
        out_shape=jax.ShapeDtypeStruct((batch * seq, MLA_HEADS * V_DIM), BF16),
        scratch_shapes=[pltpu.VMEM((MLA_HEADS, s_len, HEAD_PAD), BF16),
                        pltpu.VMEM((s_len, MLA_HEADS * V_DIM), BF16)],
        compiler_params=_cparams(("parallel", "arbitrary")),
        name="attention",
    )(q, ckv_all, kr_all, w_uk_ext, w_uv)


def _post_kernel(n_ctx, xc_ref, atc_ref, uc_ref, yfc_ref, ybc_ref, xl_ref, atl_ref, ul_ref, yfl_ref, ybl_ref,
                 m_ref, d_ref, wglu_ref, wos_ref, woa_ref, n2_ref, wq_ref, sk_ref,
                 x1_ref, h2_ref, s1_ref, s2_ref):
    is_ctx = pl.program_id(0) < n_ctx
    pick = lambda c_ref, l_ref: jnp.where(is_ctx, c_ref[...], l_ref[...])
    m = m_ref[0]
    y = pick(yfc_ref, yfl_ref) + pick(ybc_ref, ybl_ref) + pick(uc_ref, ul_ref) * d_ref[...]
    yg = jax.nn.gelu(y)
    gate = jax.nn.sigmoid(jnp.dot(yg.astype(BF16), wglu_ref[...], preferred_element_type=F32))
    mix = (jnp.dot((yg * gate).astype(BF16), wos_ref[...], preferred_element_type=F32)
           + jnp.dot(pick(atc_ref, atl_ref), woa_ref[...], preferred_element_type=F32))
    x1 = pick(xc_ref, xl_ref) + m[2:3] * mix
    x1_ref[...] = x1
    h2 = (_rms(x1, n2_ref[...]) * (1.0 + m[4:5]) + m[3:4]).astype(BF16)
    h2_ref[...] = h2
    qp = jnp.dot(h2, wq_ref[...], preferred_element_type=F32).astype(BF16)
    for hd in range(PEER_HEADS):
        for half, s_ref in enumerate((s1_ref, s2_ref)):
            c0 = (hd * 2 + half) * KEY_DIM
            s_ref[hd] = lax.dot_general(sk_ref[hd, half], qp[:, c0:c0 + KEY_DIM],
                                        (((1,), (1,)), ((), ())), preferred_element_type=F32)


def _post(ctx, lat, m6, ssm_d, w_glu, w_out_s, w_out_a, norm2, w_query, sub_keys):
    specs, args = [], []
    n_ctx, n_lat = (p[5] * p[6] // TOK_TILE for p in (ctx, lat))
    for pass_id, (x2d, attn, u_tm, yf, yb, batch, seq) in enumerate((ctx, lat)):
        n_t = seq // TOK_TILE
        loc = (lambda i: jnp.minimum(i, n_ctx - 1)) if pass_id == 0 else (lambda i: jnp.maximum(i - n_ctx, 0))
        tok = lambda w, loc=loc: pl.BlockSpec((TOK_TILE, w), lambda i: (loc(i), 0))
        tm = pl.BlockSpec((TOK_TILE, SSM_WIDTH), lambda i, loc=loc, n_t=n_t: (loc(i) % n_t, loc(i) // n_t))
        specs += [tok(D_MODEL), tok(MLA_HEADS * V_DIM), tm, tm, tm]
        args += [x2d, attn, u_tm, yf, yb]
    lat_tiles_per_row = lat[6] // TOK_TILE
    mrow = lambda i: jnp.where(i < n_ctx, 0, 1 + (i - n_ctx) // lat_tiles_per_row)
    full = lambda a: pl.BlockSpec(a.shape, lambda i: (0,) * a.ndim)
    weights = [ssm_d, w_glu, w_out_s, w_out_a, norm2, w_query, sub_keys]
    total = (n_ctx + n_lat) * TOK_TILE
    tok_out = lambda w: pl.BlockSpec((TOK_TILE, w), lambda i: (i, 0))
    sc = pl.BlockSpec((PEER_HEADS, N_KEYS, TOK_TILE), lambda i: (0, 0, i))
    return pl.pallas_call(
        functools.partial(_post_kernel, n_ctx),
        grid=(n_ctx + n_lat,),
        in_specs=specs + [pl.BlockSpec((1, 6, D_MODEL), lambda i: (mrow(i), 0, 0))] + [full(w) for w in weights],
        out_specs=[tok_out(D_MODEL), tok_out(D_MODEL), sc, sc],
        out_shape=[jax.ShapeDtypeStruct((total, D_MODEL), F32), jax.ShapeDtypeStruct((total, D_MODEL), BF16),
                   jax.ShapeDtypeStruct((PEER_HEADS, N_KEYS, total), F32),
                   jax.ShapeDtypeStruct((PEER_HEADS, N_KEYS, total), F32)],
        compiler_params=_cparams(("arbitrary",)),
        name="post",
    )(*args, m6, *weights)


def _sort_pairs(lo, hi):
    def merge(lo, hi, r):
        step = r * 2
        if step < hi - lo:
            yield from merge(lo, hi, step)
            yield from merge(lo + r, hi, step)
            for i in range(lo + r, hi - r, step):
                yield (i, i + r)
        else:
            yield (lo, lo + r)

    if hi - lo >= 1:
        mid = lo + (hi - lo) // 2
        yield from _sort_pairs(lo, mid)
        yield from _sort_pairs(mid + 1, hi)
        yield from merge(lo, hi, 1)


def _sort_desc(vals):
    vals = list(vals)
    for i, j in _sort_pairs(0, len(vals) - 1):
        vals[i], vals[j] = jnp.maximum(vals[i], vals[j]), jnp.minimum(vals[i], vals[j])
    return vals


def _merge_top(a, b):
    n = len(a)
    c = [jnp.maximum(a[k], b[n - 1 - k]) for k in range(n)]
    stride = n // 2
    while stride:
        for i in range(n):
            if not i & stride:
                c[i], c[i + stride] = jnp.maximum(c[i], c[i + stride]), jnp.minimum(c[i], c[i + stride])
        stride //= 2
    return c


def _top_keys(ref):
    k = PEER_TOPK
    groups = [_sort_desc([ref[0, g * k + r] for r in range(k)]) for g in range(N_KEYS // k)]
    while len(groups) > 1:
        groups = [_merge_top(groups[i], groups[i + 1]) for i in range(0, len(groups), 2)]
    return groups[0]


def _route_kernel(s1_ref, s2_ref, o_ref, t2_ref):
    k = PEER_TOPK
    t1, t2 = _top_keys(s1_ref), _top_keys(s2_ref)
    cands = [t1[a] + t2[b] for a in range(k) for b in range(k) if (a + 1) * (b + 1) <= k]
    size = 1 << (len(cands) - 1).bit_length()
    cands += [jnp.full_like(t1[0], -jnp.inf)] * (size - len(cands))
    best = _sort_desc(cands)
    z = jnp.ones_like(best[0])
    for v in best[1:k]:
        z = z + jnp.exp(v - best[0])
    o_ref[0, 0] = best[k - 1]
    o_ref[1, 0] = t1[0]
    o_ref[2, 0] = t2[0]
    o_ref[3, 0] = 1.0 / z
    for b in range(k):
        t2_ref[0, b] = t2[b]


def _route(s1d, s2d):
    n_rows = s1d.shape[2]
    spec = pl.BlockSpec((1, N_KEYS, 8, LANES), lambda i, h: (h, 0, i, 0))
    return pl.pallas_call(
        _route_kernel,
        grid=(n_rows // 8, PEER_HEADS),
        in_specs=[spec, spec],
        out_specs=[pl.BlockSpec((4, 1, 8, LANES), lambda i, h: (0, h, i, 0)),
                   pl.BlockSpec((1, PEER_TOPK, 8, LANES), lambda i, h: (h, 0, i, 0))],
        out_shape=[jax.ShapeDtypeStruct((4, PEER_HEADS, n_rows, LANES), F32),
                   jax.ShapeDtypeStruct((PEER_HEADS, PEER_TOPK, n_rows, LANES), F32)],
        compiler_params=_cparams(("parallel", "parallel")),
        name="peer_route",
    )(s1d, s2d)


def _cast_kernel(u_ref, v_ref, ub_ref, vb_ref):
    ub_ref[...] = u_ref[...].astype(BF16)
    vb_ref[...] = v_ref[...].astype(BF16)


def _cast_tables(u_table, v_table):
    rows = 2048
    spec = pl.BlockSpec((rows, D_MODEL), lambda i: (i, 0))
    out = jax.ShapeDtypeStruct(u_table.shape, BF16)
    return pl.pallas_call(
        _cast_kernel, grid=(u_table.shape[0] // rows,), in_specs=[spec, spec], out_specs=[spec, spec],
        out_shape=[out, out], compiler_params=_cparams(("parallel",)), name="cast_tables")(u_table, v_table)


def _peer_kernel(h2_ref, s1_ref, s2_ref, t2_ref, st_ref, u_ref, v_ref, x1_ref, m_ref, fn_ref, y_ref,
                 rank_s, e2_s, act_s, aw_s, acc_s):
    k = pl.program_id(1)
    chunks = [slice(c * LANES, (c + 1) * LANES) for c in range(PEER_TN // LANES)]

    @pl.when(k == 0)
    def _():
        for hd in range(PEER_HEADS):
            for cs in chunks:
                s2 = s2_ref[hd, :, cs]
                rank = jnp.zeros_like(s2)
                for b in range(PEER_TOPK):
                    rank = rank + jnp.where(s2 < t2_ref[hd, b:b + 1, cs], 1.0, 0.0)
                rank_s[hd, :, cs] = rank
                e2_s[hd, :, cs] = jnp.exp(s2 - st_ref[2, hd, :, cs])
        acc_s[...] = jnp.zeros_like(acc_s)

    count, gain = [], []
    for hd in range(PEER_HEADS):
        s1 = s1_ref[hd]
        cnt = jnp.zeros_like(s1)
        for b in range(PEER_TOPK):
            cnt = cnt + jnp.where(s1 + t2_ref[hd, b:b + 1, :] >= st_ref[0, hd], 1.0, 0.0)
        count.append(cnt)
        gain.append(jnp.exp(s1 - st_ref[1, hd]) * st_ref[3, hd])

    act_s[...] = lax.dot_general(u_ref[...], h2_ref[...], (((1,), (1,)), ((), ())),
                                 preferred_element_type=F32)
    for ii in range(PEER_ROWS):
        rows = slice(ii * N_KEYS, (ii + 1) * N_KEYS)
        for cs in chunks:
            w = jnp.zeros((N_KEYS, LANES), F32)
            for hd in range(PEER_HEADS):
                hit = rank_s[hd, :, cs] < count[hd][ii:ii + 1, cs]
                w = w + jnp.where(hit, e2_s[hd, :, cs], 0.0) * gain[hd][ii:ii + 1, cs]
            aw_s[rows, cs] = (jax.nn.gelu(act_s[rows, cs]) * w).astype(BF16)
    acc_s[...] += lax.dot_general(aw_s[...], v_ref[...], (((0,), (0,)), ((), ())),
                                  preferred_element_type=F32)

    @pl.when(k == pl.num_programs(1) - 1)
    def _():
        y_ref[...] = _rms(x1_ref[...] + m_ref[0][5:6] * acc_s[...], fn_ref[...])


def _peer(h2, s1t, s2t, t2, stats, u_tab, v_tab, x1, m6, mrow, final_norm):
    ntok = h2.shape[0]
    tn, te = PEER_TN, PEER_TE
    tok = pl.BlockSpec((tn, D_MODEL), lambda t, k: (t, 0))
    tab = pl.BlockSpec((te, D_MODEL), lambda t, k: (k, 0))
    keys = pltpu.VMEM((PEER_HEADS, N_KEYS, tn), F32)
    return pl.pallas_call(
        _peer_kernel,
        grid=(ntok // tn, N_EXPERTS // te),
        in_specs=[tok,
                  pl.BlockSpec((PEER_HEADS, PEER_ROWS, tn), lambda t, k: (0, k, t)),
                  pl.BlockSpec((PEER_HEADS, N_KEYS, tn), lambda t, k: (0, 0, t)),
                  pl.BlockSpec((PEER_HEADS, PEER_TOPK, tn), lambda t, k: (0, 0, t)),
                  pl.BlockSpec((4, PEER_HEADS, 1, tn), lambda t, k: (0, 0, 0, t)),
                  tab, tab, tok,
                  pl.BlockSpec((1, 6, D_MODEL), lambda t, k: (mrow(t), 0, 0)),
                  pl.BlockSpec((1, D_MODEL), lambda t, k: (0, 0))],
        out_specs=tok,
        out_shape=jax.ShapeDtypeStruct((ntok, D_MODEL), F32),
        scratch_shapes=[keys, keys, pltpu.VMEM((te, tn), F32), pltpu.VMEM((te, tn), BF16),
                        pltpu.VMEM((tn, D_MODEL), F32)],
        compiler_params=_cparams(("parallel", "arbitrary")),
        name="peer_dense",
    )(h2, s1t, s2t, t2, stats, u_tab, v_tab, x1, m6, final_norm)


def _rot_cols(w):
    j = np.arange(QK_ROPE)
    first = (j % (QK_ROPE // 2)) < (QK_ROPE // 4)
    perm = np.where(first, j + QK_ROPE // 4, j - QK_ROPE // 4)
    sign = np.where(first, -1.0, 1.0).astype(np.float32)
    return w[..., perm] * sign


def _rope_slot(w_rope):
    pad = [(0, 0)] * (w_rope.ndim - 1)
    return jnp.pad(w_rope, pad + [(QK_NOPE, HEAD_PAD - QK_NOPE - QK_ROPE)])


def _rope_tables(n_tokens):
    n_rows = n_tokens // GRID_W
    rows = jnp.repeat(jnp.arange(n_rows, dtype=F32), GRID_W)
    cols = jnp.tile(jnp.arange(GRID_W, dtype=F32), n_rows)
    half = QK_ROPE // 2
    inv_freq = 1.0 / (ROPE_BASE ** (jnp.arange(0, half, 2, dtype=F32) / half))
    ang_r = rows[:, None] * inv_freq
    ang_c = cols[:, None] * inv_freq
    ang = jnp.concatenate([ang_r, ang_r, ang_c, ang_c], axis=-1)
    lead = (QK_NOPE, HEAD_PAD - QK_NOPE - QK_ROPE)
    cos = jnp.pad(jnp.cos(ang), [(0, 0), lead], constant_values=1.0)
    sin = jnp.pad(jnp.sin(ang), [(0, 0), lead])
    return cos, sin


def kernel(x_prompt, x_sample, c, cache_ckv, cache_krope, state_ssm, c_ctx, w_mod, b_mod, norm1, w_in,
           ssm_lam_re, ssm_lam_im, ssm_log_dt, ssm_b_re, ssm_b_im, ssm_c_re, ssm_c_im, ssm_d, w_glu,
           q_norm, w_uq, kv_norm, w_uk, w_uv, w_out, norm2, w_query, sub_keys, u_table, v_table,
           final_norm):
    bc, tc_len, _ = x_prompt.shape
    bl, tl_len, _ = x_sample.shape
    l = 0
    row = lambda a: a.reshape(1, -1)

    n_mod = 8
    cvec = jnp.concatenate([c_ctx[None], c, jnp.zeros((n_mod - 1 - bl, D_MODEL), F32)], 0)
    m6 = _modulation(cvec, w_mod[l], row(b_mod[l])).reshape(n_mod, 6, D_MODEL)

    wi = w_in[l]
    o_kr = SSM_WIDTH + Q_RANK + KV_RANK
    w_kr = wi[:, o_kr:]
    w_in_ext = jnp.concatenate([wi[:, :o_kr], _rope_slot(w_kr), _rope_slot(_rot_cols(w_kr))], 1).astype(BF16)
    wq3 = w_uq[l].reshape(Q_RANK, MLA_HEADS, QK_NOPE + QK_ROPE)
    wq_main = jnp.pad(wq3, [(0, 0), (0, 0), (0, HEAD_PAD - QK_NOPE - QK_ROPE)])
    wq_rot = _rope_slot(_rot_cols(wq3[..., QK_NOPE:]))
    w_uq_ext = jnp.concatenate([wq_main.reshape(Q_RANK, -1), wq_rot.reshape(Q_RANK, -1)], 1).astype(BF16)
    w_uk_ext = jnp.pad(w_uk[l].reshape(KV_RANK, MLA_HEADS, QK_NOPE),
                       [(0, 0), (0, 0), (0, HEAD_PAD - QK_NOPE)]).reshape(KV_RANK, -1).astype(BF16)
    w_uv_b = w_uv[l].astype(BF16)
    w_glu_b = w_glu[l].astype(BF16)
    w_out_s = w_out[l][:SSM_WIDTH].astype(BF16)
    w_out_a = w_out[l][SSM_WIDTH:].astype(BF16)
    w_query_b = w_query[l].astype(BF16)
    sub_keys_b = sub_keys[l].astype(BF16)
    u_tab, v_tab = _cast_tables(u_table[l], v_table[l])
    cos, sin = _rope_tables(tl_len)

    bw, cw, a_blk = _s5_params(ssm_lam_re[l], ssm_lam_im[l], ssm_log_dt[l], ssm_b_re[l], ssm_b_im[l],
                               ssm_c_re[l], ssm_c_im[l])

    ctx_row = lambda i: 0
    lat_tiles = tl_len // TOK_TILE
    lat_row = lambda i: 1 + i // lat_tiles
    xc = x_prompt.reshape(bc * tc_len, D_MODEL)
    xl = x_sample.reshape(bl * tl_len, D_MODEL)

    u_c, q_c, ckv_c, kr_c = _pre(xc, m6, ctx_row, row(norm1[l]), w_in_ext, row(q_norm[l]), row(kv_norm[l]),
                                 w_uq_ext[:, :MLA_HEADS * HEAD_PAD], bc, tc_len, None)
    u_l, q_l, ckv_l, kr_l = _pre(xl, m6, lat_row, row(norm1[l]), w_in_ext, row(q_norm[l]), row(kv_norm[l]),
                                 w_uq_ext, bl, tl_len, (cos, sin))

    rl = 8
    h0_c = jnp.zeros((2, 2, S5_BLOCKS_PER_HALF, 2, bc, LANES), F32)
    yf_c, yb_c, st_c = _s5_scan(u_c.reshape(tc_len * bc, SSM_WIDTH), bw, cw, a_blk, h0_c, bc, tc_len,
                                S5_TILE_ROWS // bc)
    u_lp = jnp.pad(u_l.reshape(tl_len, bl, SSM_WIDTH), [(0, 0), (0, rl - bl), (0, 0)])
    h0_l = jnp.pad(_s5_state_in(state_ssm[:, l]), [(0, 0)] * 4 + [(0, rl - bl), (0, 0)])
    yf_l, yb_l, _ = _s5_scan(u_lp.reshape(tl_len * rl, SSM_WIDTH), bw, cw, a_blk, h0_l, rl, tl_len,
                             S5_TILE_ROWS // rl)

    at_c = _attention(q_c, ckv_c.reshape(bc, tc_len, KV_RANK), kr_c.reshape(bc, tc_len, HEAD_PAD),
                      w_uk_ext, w_uv_b, bc, tc_len)
    ckv_all = jnp.concatenate([cache_ckv[:, l], ckv_l.reshape(bl, tl_len, KV_RANK)], 1)
    kr_all = jnp.concatenate([_rope_slot(cache_krope[:, l]), kr_l.reshape(bl, tl_len, HEAD_PAD)], 1)
    at_l = _attention(q_l, ckv_all, kr_all, w_uk_ext, w_uv_b, bl, tl_len)

    tm_c = lambda a: a.reshape(tc_len, bc * SSM_WIDTH)
    tm_l = lambda a: a.reshape(tl_len, rl * SSM_WIDTH)
    ntok = bc * tc_len + bl * tl_len
    x1, h2, s1t, s2t = _post((xc, at_c, tm_c(u_c), tm_c(yf_c), tm_c(yb_c), bc, tc_len),
                             (xl, at_l, tm_l(u_lp), tm_l(yf_l), tm_l(yb_l), bl, tl_len),
                             m6, row(ssm_d[l]), w_glu_b, w_out_s, w_out_a, row(norm2[l]), w_query_b, sub_keys_b)

    dense = lambda a: a.reshape(PEER_HEADS, N_KEYS, ntok // LANES, LANES)
    stats, t2 = _route(dense(s1t), dense(s2t))
    stats = stats.reshape(4, PEER_HEADS, 1, ntok)
    t2 = t2.reshape(PEER_HEADS, PEER_TOPK, ntok)
    ctx_peer_tiles = bc * tc_len // PEER_TN
    lat_peer_tiles = tl_len // PEER_TN
    peer_row = lambda t: jnp.where(t < ctx_peer_tiles, 0, 1 + (t - ctx_peer_tiles) // lat_peer_tiles)
    y = _peer(h2, s1t, s2t, t2, stats, u_tab, v_tab, x1, m6, peer_row, row(final_norm))

    y_prompt = y[:bc * tc_len].reshape(bc, tc_len, D_MODEL)
    y_sample = y[bc * tc_len:].reshape(bl, tl_len, D_MODEL)
    new_ckv = ckv_c.reshape(bc, 1, tc_len, KV_RANK)
    new_krope = kr_c[:, QK_NOPE:QK_NOPE + QK_ROPE].reshape(bc, 1, tc_len, QK_ROPE)
    new_ssm = _s5_state_out(st_c)[:, None]
    return (y_prompt, y_sample, new_ckv, new_krope, new_ssm)
```

```python
import functools
import math

import jax
import jax.numpy as jnp
import numpy as np
from jax import lax
from jax.experimental import pallas as pl
from jax.experimental.pallas import tpu as pltpu

F32 = jnp.float32
BF16 = jnp.bfloat16

D_MODEL = 1024
GRID_W = 64
EPS = 1e-6
SSM_WIDTH = 512
SSM_GROUP = 16
SSM_GROUPS = 32
SSM_STATE = 64
MLA_HEADS = 8
QK_NOPE = 64
QK_ROPE = 32
V_DIM = 64
Q_RANK = 384
KV_RANK = 256
ROPE_BASE = 10000.0
N_KEYS = 128
N_EXPERTS = N_KEYS * N_KEYS
PEER_HEADS = 8
PEER_TOPK = 16
KEY_DIM = 128

LANES = 128
HEAD_PAD = 128
TOK_TILE = 256
S5_PAIR = 2
S5_BLOCKS = SSM_GROUPS // S5_PAIR
S5_HALF = SSM_WIDTH // 2
S5_BLOCKS_PER_HALF = S5_BLOCKS // 2
S5_TILE_ROWS = 512
PEER_TN = 512
PEER_ROWS = 8
PEER_TE = PEER_ROWS * N_KEYS
GELU_C0 = math.sqrt(2.0 / math.pi)
GELU_C1 = 0.044715 * GELU_C0
VMEM_LIMIT = 54 * 1024 * 1024


def _cparams(sem):
    return pltpu.CompilerParams(dimension_semantics=sem, vmem_limit_bytes=VMEM_LIMIT)


def _rms(x, g):
    return x * lax.rsqrt(jnp.mean(x * x, axis=-1, keepdims=True) + EPS) * g


def _mod_kernel(c_ref, w_ref, b_ref, o_ref):
    o_ref[...] = jnp.dot(jax.nn.silu(c_ref[...]), w_ref[...], preferred_element_type=F32) + b_ref[...]


def _modulation(cvec, w_mod, b_mod):
    rows, d = cvec.shape
    n = w_mod.shape[1]
    tn = 1536
    return pl.pallas_call(
        _mod_kernel,
        grid=(n // tn,),
        in_specs=[pl.BlockSpec((rows, d), lambda j: (0, 0)),
                  pl.BlockSpec((d, tn), lambda j: (0, j)),
                  pl.BlockSpec((1, tn), lambda j: (0, j))],
        out_specs=pl.BlockSpec((rows, tn), lambda j: (0, j)),
        out_shape=jax.ShapeDtypeStruct((rows, n), F32),
        compiler_params=_cparams(("parallel",)),
        name="modulation",
    )(cvec, w_mod, b_mod)


def _pre_kernel(use_rope, x_ref, m_ref, n1_ref, win_ref, qn_ref, kvn_ref, wuq_ref, *rest):
    if use_rope:
        cos_ref, sin_ref, u_ref, q_ref, ckv_ref, kr_ref = rest
    else:
        u_ref, q_ref, ckv_ref, kr_ref = rest
    m = m_ref[0]
    h = _rms(x_ref[...], n1_ref[...]) * (1.0 + m[1:2]) + m[0:1]
    z = jnp.dot(h.astype(BF16), win_ref[...], preferred_element_type=F32)
    u_ref[...] = z[:, :SSM_WIDTH]
    o_q, o_kv, o_kr = SSM_WIDTH, SSM_WIDTH + Q_RANK, SSM_WIDTH + Q_RANK + KV_RANK
    cqn = _rms(z[:, o_q:o_kv], qn_ref[...])
    qq = jnp.dot(cqn.astype(BF16), wuq_ref[...], preferred_element_type=F32)
    ckv_ref[...] = _rms(z[:, o_kv:o_kr], kvn_ref[...])
    kr = z[:, o_kr:o_kr + HEAD_PAD]
    scale = (QK_NOPE + QK_ROPE) ** -0.5
    nq = MLA_HEADS * HEAD_PAD
    if use_rope:
        cos, sin = cos_ref[...], sin_ref[...]
        kr = kr * cos + z[:, o_kr + HEAD_PAD:o_kr + 2 * HEAD_PAD] * sin
        for hd in range(MLA_HEADS):
            sl = slice(hd * HEAD_PAD, (hd + 1) * HEAD_PAD)
            qh = qq[:, sl] * cos + qq[:, nq + hd * HEAD_PAD:nq + (hd + 1) * HEAD_PAD] * sin
            q_ref[:, sl] = (qh * scale).astype(BF16)
    else:
        q_ref[...] = (qq[:, :nq] * scale).astype(BF16)
    kr_ref[...] = kr


def _pre(x2d, m6, mrow, norm1, w_in_ext, q_norm, kv_norm, w_uq_ext, batch, seq, rope):
    ntok = batch * seq
    n_t = seq // TOK_TILE
    use_rope = rope is not None
    full = lambda a: pl.BlockSpec(a.shape, lambda i: (0,) * a.ndim)
    in_specs = [pl.BlockSpec((TOK_TILE, D_MODEL), lambda i: (i, 0)),
                pl.BlockSpec((1, 6, D_MODEL), lambda i: (mrow(i), 0, 0)),
                full(norm1), full(w_in_ext), full(q_norm), full(kv_norm), full(w_uq_ext)]
    args = [x2d, m6, norm1, w_in_ext, q_norm, kv_norm, w_uq_ext]
    if use_rope:
        in_specs += [pl.BlockSpec((TOK_TILE, HEAD_PAD), lambda i: (i % n_t, 0))] * 2
        args += list(rope)
    out_specs = [pl.BlockSpec((TOK_TILE, SSM_WIDTH), lambda i: (i % n_t, i // n_t)),
                 pl.BlockSpec((TOK_TILE, MLA_HEADS * HEAD_PAD), lambda i: (i, 0)),
                 pl.BlockSpec((TOK_TILE, KV_RANK), lambda i: (i, 0)),
                 pl.BlockSpec((TOK_TILE, HEAD_PAD), lambda i: (i, 0))]
    out_shape = [jax.ShapeDtypeStruct((seq, batch * SSM_WIDTH), F32),
                 jax.ShapeDtypeStruct((ntok, MLA_HEADS * HEAD_PAD), BF16),
                 jax.ShapeDtypeStruct((ntok, KV_RANK), F32),
                 jax.ShapeDtypeStruct((ntok, HEAD_PAD), F32)]
    return pl.pallas_call(
        functools.partial(_pre_kernel, use_rope),
        grid=(ntok // TOK_TILE,),
        in_specs=in_specs, out_specs=out_specs, out_shape=out_shape,
        compiler_params=_cparams(("parallel",)),
        name="pre_rope" if use_rope else "pre",
    )(*args)


def _s5_param_kernel(lr_ref, li_ref, ldt_ref, lrx_ref, lix_ref, ldtx_ref, bre_ref, bim_ref, cim_ref,
                     abr_ref, abi_ref, bfr_ref, bfi_ref, ncim_ref):
    def disc(lr, li, ldt):
        dt = jnp.exp(ldt)
        mag = jnp.exp(lr * dt)
        ab_re, ab_im = mag * jnp.cos(li * dt), mag * jnp.sin(li * dt)
        den = lr * lr + li * li
        br, bi = lr / den, -li / den
        ar = ab_re - 1.0
        return ab_re, ab_im, ar * br - ab_im * bi, ar * bi + ab_im * br

    ab_re, ab_im, _, _ = disc(lr_ref[...], li_ref[...], ldt_ref[...])
    abr_ref[...] = ab_re
    abi_ref[...] = ab_im
    _, _, f_re, f_im = disc(lrx_ref[...], lix_ref[...], ldtx_ref[...])
    b_re, b_im = bre_ref[...], bim_ref[...]
    bfr_ref[...] = f_re * b_re - f_im * b_im
    bfi_ref[...] = f_re * b_im + f_im * b_re
    ncim_ref[...] = -cim_ref[...]


def _s5_params(lam_re, lam_im, log_dt, b_re, b_im, c_re, c_im):
    dg = 2 * SSM_GROUPS
    n, p = SSM_STATE, SSM_GROUP
    lr = lam_re.reshape(dg, n)
    li = lam_im.reshape(dg, n)
    ldt = jnp.broadcast_to(log_dt.reshape(dg, 1), (dg, n))
    rep = lambda a: jnp.repeat(a, p, axis=1)
    args = [lr, li, ldt, rep(lr), rep(li), rep(ldt),
            b_re.reshape(dg, n * p), b_im.reshape(dg, n * p), c_im.reshape(dg, p * n)]
    small = jax.ShapeDtypeStruct((dg, n), F32)
    big = jax.ShapeDtypeStruct((dg, n * p), F32)
    ab_re, ab_im, bf_re, bf_im, ncim = pl.pallas_call(
        _s5_param_kernel, out_shape=[small, small, big, big, big], name="s5_params")(*args)

    nb, gp = S5_BLOCKS, S5_PAIR
    eye = jnp.eye(gp, dtype=F32)
    pos = np.arange(nb) % S5_BLOCKS_PER_HALF
    bf = jnp.stack([bf_re, bf_im], 0).reshape(2, 2, nb, gp, n, p)
    bc = jnp.einsum('adbgnp,gh->bdgpahn', bf, eye).reshape(nb, 2, gp * p, 2 * gp * n)
    bw = jnp.zeros((nb, 2, S5_BLOCKS_PER_HALF, gp * p, 2 * gp * n), F32)
    bw = bw.at[np.arange(nb), :, pos].set(bc).reshape(nb, 2, S5_HALF, 2 * gp * n)
    cc = jnp.stack([c_re.reshape(2, nb, gp, p, n), ncim.reshape(2, nb, gp, p, n)], 0)
    cc = jnp.einsum('adbgpn,gh->bdagnhp', cc, eye).reshape(nb, 2, 2 * gp * n, gp * p)
    cw = jnp.zeros((nb, 2, 2 * gp * n, S5_BLOCKS_PER_HALF, gp * p), F32)
    cw = cw.at[np.arange(nb), :, :, pos].set(cc).reshape(nb, 2, 2 * gp * n, S5_HALF)
    a = jnp.stack([ab_re, ab_im], 0).reshape(2, 2, nb, gp * n).transpose(2, 1, 0, 3)
    hb = S5_BLOCKS_PER_HALF
    bw = bw.reshape(2, hb, 2, S5_HALF, 2 * gp * n).transpose(0, 2, 3, 1, 4).reshape(2, 2, S5_HALF, -1)
    cw = cw.reshape(2, hb, 2, 2 * gp * n, S5_HALF).transpose(0, 2, 1, 3, 4).reshape(2, 2, -1, S5_HALF)
    a = a.reshape(2, hb, 2, 2, 1, gp * n).transpose(0, 2, 1, 3, 4, 5)
    return bw.astype(BF16), cw.astype(BF16), a


def _s5_state_in(h):
    bsz = h.shape[0]
    return h.reshape(bsz, 2, 2, 2, S5_BLOCKS_PER_HALF, LANES).transpose(3, 1, 4, 2, 0, 5)


def _s5_state_out(st):
    bsz = st.shape[4]
    return st.transpose(4, 1, 3, 0, 2, 5).reshape(bsz, 2, 2, SSM_GROUPS, SSM_STATE)


def _s5_kernel(rows_per_step, steps, uf_ref, ub_ref, bw_ref, cw_ref, a_ref, h0_ref,
               yf_ref, yb_ref, st_ref, buf, coef, carry):
    r = rows_per_step
    nb = S5_BLOCKS_PER_HALF
    width = 2 * LANES

    @pl.when(pl.program_id(1) == 0)
    def _():
        carry[...] = h0_ref[0]
        coef[...] = jnp.broadcast_to(a_ref[0], coef.shape)

    for d, u_ref in enumerate((uf_ref, ub_ref)):
        buf[d] = jnp.dot(u_ref[...].astype(BF16), bw_ref[0, d], preferred_element_type=F32)

    chains = [(d, b) for d in range(2) for b in range(nb)]
    for g in range(r // 8):
        rows8 = slice(g * 8, (g + 1) * 8)

        def body(k, hs, rows8=rows8):
            out = []
            for (d, b), (h_re, h_im) in zip(chains, hs):
                t = k if d == 0 else steps - 1 - k
                r0 = pl.multiple_of(t * r + g * 8, 8)
                a_re, a_im = coef[d, b, 0], coef[d, b, 1]
                re_l, im_l = slice(b * width, b * width + LANES), slice(b * width + LANES, (b + 1) * width)
                n_re = a_re * h_re - a_im * h_im + buf[d, pl.ds(r0, 8), re_l]
                n_im = a_re * h_im + a_im * h_re + buf[d, pl.ds(r0, 8), im_l]
                buf[d, pl.ds(r0, 8), re_l] = n_re
                buf[d, pl.ds(r0, 8), im_l] = n_im
                out.append((n_re, n_im))
            return tuple(out)

        init = tuple((carry[d, b, 0, rows8], carry[d, b, 1, rows8]) for d, b in chains)
        final = lax.fori_loop(0, steps, body, init)
        for (d, b), (h_re, h_im) in zip(chains, final):
            carry[d, b, 0, rows8] = h_re
            carry[d, b, 1, rows8] = h_im

    for d, y_ref in enumerate((yf_ref, yb_ref)):
        y_ref[...] = jnp.dot(buf[d].astype(BF16), cw_ref[0, d], preferred_element_type=F32)
    st_ref[0] = carry[...]


def _s5_scan(u_tm, bw, cw, a, h0, rows_per_step, seq, steps):
    r = rows_per_step
    n_t = seq // steps
    rows = steps * r
    nb = S5_BLOCKS_PER_HALF
    wide = nb * 2 * LANES
    half4 = lambda h, t: (h, 0, 0, 0)
    half6 = lambda h, t: (h, 0, 0, 0, 0, 0)
    state = (2, nb, 2, r, LANES)
    in_specs = [pl.BlockSpec((rows, S5_HALF), lambda h, t: (t, h)),
                pl.BlockSpec((rows, S5_HALF), lambda h, t: (n_t - 1 - t, h)),
                pl.BlockSpec((1, 2, S5_HALF, wide), half4),
                pl.BlockSpec((1, 2, wide, S5_HALF), half4),
                pl.BlockSpec((1, 2, nb, 2, 1, LANES), half6),
                pl.BlockSpec((1,) + state, half6)]
    out_specs = [pl.BlockSpec((rows, S5_HALF), lambda h, t: (t, h)),
                 pl.BlockSpec((rows, S5_HALF), lambda h, t: (n_t - 1 - t, h)),
                 pl.BlockSpec((1,) + state, half6)]
    out_shape = [jax.ShapeDtypeStruct(u_tm.shape, F32), jax.ShapeDtypeStruct(u_tm.shape, F32),
                 jax.ShapeDtypeStruct((2,) + state, F32)]
    return pl.pallas_call(
        functools.partial(_s5_kernel, r, steps),
        grid=(2, n_t),
        in_specs=in_specs, out_specs=out_specs, out_shape=out_shape,
        scratch_shapes=[pltpu.VMEM((2, rows, wide), F32), pltpu.VMEM((2, nb, 2, 8, LANES), F32),
                        pltpu.VMEM(state, F32)],
        compiler_params=_cparams(("arbitrary", "arbitrary")),
        name="s5_scan",
    )(u_tm, u_tm, bw, cw, a, h0)


def _attn_kernel(q_ref, ckv_ref, kr_ref, wuk_ref, wuv_ref, o_ref, k_s, v_s):
    @pl.when(pl.program_id(1) == 0)
    def _():
        kv = ckv_ref[0].astype(BF16)
        kn = jnp.dot(kv, wuk_ref[...], preferred_element_type=F32)
        kr = kr_ref[0]
        for hd in range(MLA_HEADS):
            k_s[hd] = (kn[:, hd * HEAD_PAD:(hd + 1) * HEAD_PAD] + kr).astype(BF16)
        v_s[...] = jnp.dot(kv, wuv_ref[...], preferred_element_type=F32).astype(BF16)

    for hd in range(MLA_HEADS):
        qh = q_ref[:, hd * HEAD_PAD:(hd + 1) * HEAD_PAD]
        s = lax.dot_general(qh, k_s[hd], (((1,), (1,)), ((), ())), preferred_element_type=F32)
        p = jnp.exp(s - jnp.max(s, axis=-1, keepdims=True))
        l = jnp.sum(p, axis=-1, keepdims=True)
        o = jnp.dot(p.astype(BF16), v_s[:, hd * V_DIM:(hd + 1) * V_DIM], preferred_element_type=F32)
        o_ref[:, hd * V_DIM:(hd + 1) * V_DIM] = (o / l).astype(BF16)


def _attention(q, ckv_all, kr_all, w_uk_ext, w_uv, batch, seq):
    s_len = ckv_all.shape[1]
    n_q = seq // TOK_TILE
    return pl.pallas_call(
        _attn_kernel,
        grid=(batch, n_q),
        in_specs=[pl.BlockSpec((TOK_TILE, MLA_HEADS * HEAD_PAD), lambda b, i: (b * n_q + i, 0)),
                  pl.BlockSpec((1, s_len, KV_RANK), lambda b, i: (b, 0, 0)),
                  pl.BlockSpec((1, s_len, HEAD_PAD), lambda b, i: (b, 0, 0)),
                  pl.BlockSpec(w_uk_ext.shape, lambda b, i: (0, 0)),
                  pl.BlockSpec(w_uv.shape, lambda b, i: (0, 0))],
        out_specs=pl.BlockSpec((TOK_TILE, MLA_HEADS * V_DIM), lambda b, i: (b * n_q + i, 0)),
        out_shape=jax.ShapeDtypeStruct((batch * seq, MLA_HEADS * V_DIM), BF16),
        scratch_shapes=[pltpu.VMEM((MLA_HEADS, s_len, HEAD_PAD), BF16),
                        pltpu.VMEM((s_len, MLA_HEADS * V_DIM), BF16)],
        compiler_params=_cparams(("parallel", "arbitrary")),
        name="attention",
    )(q, ckv_all, kr_all, w_uk_ext, w_uv)


def _post_kernel(n_ctx, xc_ref, atc_ref, uc_ref, yfc_ref, ybc_ref, xl_ref, atl_ref, ul_ref, yfl_ref, ybl_ref,
                 m_ref, d_ref, wglu_ref, wos_ref, woa_ref, n2_ref, wq_ref, sk_ref,
                 x1_ref, h2_ref, s1_ref, s2_ref):
    is_ctx = pl.program_id(0) < n_ctx
    pick = lambda c_ref, l_ref: jnp.where(is_ctx, c_ref[...], l_ref[...])
    m = m_ref[0]
    y = pick(yfc_ref, yfl_ref) + pick(ybc_ref, ybl_ref) + pick(uc_ref, ul_ref) * d_ref[...]
    yg = jax.nn.gelu(y)
    gate = jax.nn.sigmoid(jnp.dot(yg.astype(BF16), wglu_ref[...], preferred_element_type=F32))
    mix = (jnp.dot((yg * gate).astype(BF16), wos_ref[...], preferred_element_type=F32)
           + jnp.dot(pick(atc_ref, atl_ref), woa_ref[...], preferred_element_type=F32))
    x1 = pick(xc_ref, xl_ref) + m[2:3] * mix
    x1_ref[...] = x1
    h2 = (_rms(x1, n2_ref[...]) * (1.0 + m[4:5]) + m[3:4]).astype(BF16)
    h2_ref[...] = h2
    qp = jnp.dot(h2, wq_ref[...], preferred_element_type=F32).astype(BF16)
    for hd in range(PEER_HEADS):
        for half, s_ref in enumerate((s1_ref, s2_ref)):
            c0 = (hd * 2 + half) * KEY_DIM
            s_ref[hd] = lax.dot_general(sk_ref[hd, half], qp[:, c0:c0 + KEY_DIM],
                                        (((1,), (1,)), ((), ())), preferred_element_type=F32)


def _post(ctx, lat, m6, ssm_d, w_glu, w_out_s, w_out_a, norm2, w_query, sub_keys):
    specs, args = [], []
    n_ctx, n_lat = (p[5] * p[6] // TOK_TILE for p in (ctx, lat))
    for pass_id, (x2d, attn, u_tm, yf, yb, batch, seq) in enumerate((ctx, lat)):
        n_t = seq // TOK_TILE
        loc = (lambda i: jnp.minimum(i, n_ctx - 1)) if pass_id == 0 else (lambda i: jnp.maximum(i - n_ctx, 0))
        tok = lambda w, loc=loc: pl.BlockSpec((TOK_TILE, w), lambda i: (loc(i), 0))
        tm = pl.BlockSpec((TOK_TILE, SSM_WIDTH), lambda i, loc=loc, n_t=n_t: (loc(i) % n_t, loc(i) // n_t))
        specs += [tok(D_MODEL), tok(MLA_HEADS * V_DIM), tm, tm, tm]
        args += [x2d, attn, u_tm, yf, yb]
    lat_tiles_per_row = lat[6] // TOK_TILE
    mrow = lambda i: jnp.where(i < n_ctx, 0, 1 + (i - n_ctx) // lat_tiles_per_row)
    full = lambda a: pl.BlockSpec(a.shape, lambda i: (0,) * a.ndim)
    weights = [ssm_d, w_glu, w_out_s, w_out_a, norm2, w_query, sub_keys]
    total = (n_ctx + n_lat) * TOK_TILE
    tok_out = lambda w: pl.BlockSpec((TOK_TILE, w), lambda i: (i, 0))
    sc = pl.BlockSpec((PEER_HEADS, N_KEYS, TOK_TILE), lambda i: (0, 0, i))
    return pl.pallas_call(
        functools.partial(_post_kernel, n_ctx),
        grid=(n_ctx + n_lat,),
        in_specs=specs + [pl.BlockSpec((1, 6, D_MODEL), lambda i: (mrow(i), 0, 0))] + [full(w) for w in weights],
        out_specs=[tok_out(D_MODEL), tok_out(D_MODEL), sc, sc],
        out_shape=[jax.ShapeDtypeStruct((total, D_MODEL), F32), jax.ShapeDtypeStruct((total, D_MODEL), BF16),
                   jax.ShapeDtypeStruct((PEER_HEADS, N_KEYS, total), F32),
                   jax.ShapeDtypeStruct((PEER_HEADS, N_KEYS, total), F32)],
        compiler_params=_cparams(("arbitrary",)),
        name="post",
    )(*args, m6, *weights)


def _sort_pairs(lo, hi):
    def merge(lo, hi, r):
        step = r * 2
        if step < hi - lo:
            yield from merge(lo, hi, step)
            yield from merge(lo + r, hi, step)
            for i in range(lo + r, hi - r, step):
                yield (i, i + r)
        else:
            yield (lo, lo + r)

    if hi - lo >= 1:
        mid = lo + (hi - lo) // 2
        yield from _sort_pairs(lo, mid)
        yield from _sort_pairs(mid + 1, hi)
        yield from merge(lo, hi, 1)


def _sort_desc(vals):
    vals = list(vals)
    for i, j in _sort_pairs(0, len(vals) - 1):
        vals[i], vals[j] = jnp.maximum(vals[i], vals[j]), jnp.minimum(vals[i], vals[j])
    return vals


def _merge_top(a, b):
    n = len(a)
    c = [jnp.maximum(a[k], b[n - 1 - k]) for k in range(n)]
    stride = n // 2
    while stride:
        for i in range(n):
            if not i & stride:
                c[i], c[i + stride] = jnp.maximum(c[i], c[i + stride]), jnp.minimum(c[i], c[i + stride])
        stride //= 2
    return c


def _top_keys(ref):
    k = PEER_TOPK
    groups = [_sort_desc([ref[0, g * k + r] for r in range(k)]) for g in range(N_KEYS // k)]
    while len(groups) > 1:
        groups = [_merge_top(groups[i], groups[i + 1]) for i in range(0, len(groups), 2)]
    return groups[0]


def _route_kernel(s1_ref, s2_ref, o_ref, t2_ref):
    k = PEER_TOPK
    t1, t2 = _top_keys(s1_ref), _top_keys(s2_ref)
    cands = [t1[a] + t2[b] for a in range(k) for b in range(k) if (a + 1) * (b + 1) <= k]
    size = 1 << (len(cands) - 1).bit_length()
    cands += [jnp.full_like(t1[0], -jnp.inf)] * (size - len(cands))
    best = _sort_desc(cands)
    z = jnp.ones_like(best[0])
    for v in best[1:k]:
        z = z + jnp.exp(v - best[0])
    o_ref[0, 0] = best[k - 1]
    o_ref[1, 0] = t1[0]
    o_ref[2, 0] = t2[0]
    o_ref[3, 0] = 1.0 / z
    for b in range(k):
        t2_ref[0, b] = t2[b]


def _route(s1d, s2d):
    n_rows = s1d.shape[2]
    spec = pl.BlockSpec((1, N_KEYS, 8, LANES), lambda i, h: (h, 0, i, 0))
    return pl.pallas_call(
        _route_kernel,
        grid=(n_rows // 8, PEER_HEADS),
        in_specs=[spec, spec],
        out_specs=[pl.BlockSpec((4, 1, 8, LANES), lambda i, h: (0, h, i, 0)),
                   pl.BlockSpec((1, PEER_TOPK, 8, LANES), lambda i, h: (h, 0, i, 0))],
        out_shape=[jax.ShapeDtypeStruct((4, PEER_HEADS, n_rows, LANES), F32),
                   jax.ShapeDtypeStruct((PEER_HEADS, PEER_TOPK, n_rows, LANES), F32)],
        compiler_params=_cparams(("parallel", "parallel")),
        name="peer_route",
    )(s1d, s2d)


def _peer_kernel(h2_ref, s1_ref, s2_ref, t2_ref, st_ref, u_ref, v_ref, x1_ref, m_ref, fn_ref, y_ref,
                 e2_s, act_s, aw_s, acc_s):
    k = pl.program_id(1)
    chunks = [slice(c * LANES, (c + 1) * LANES) for c in range(PEER_TN // LANES)]

    @pl.when(k == 0)
    def _():
        for hd in range(PEER_HEADS):
            e2_s[hd] = jnp.exp(s2_ref[hd] - st_ref[2, hd])
        acc_s[...] = jnp.zeros_like(acc_s)

    floor, gain = [], []
    for hd in range(PEER_HEADS):
        s1 = s1_ref[hd]
        low = jnp.full_like(s1, jnp.inf)
        for b in range(PEER_TOPK):
            t2b = t2_ref[hd, b:b + 1, :]
            low = jnp.where(s1 + t2b >= st_ref[0, hd], t2b, low)
        floor.append(low)
        gain.append(jnp.exp(s1 - st_ref[1, hd]) * (0.5 * st_ref[3, hd]))

    act_s[...] = lax.dot_general(u_ref[...].astype(BF16), h2_ref[...], (((1,), (1,)), ((), ())),
                                 preferred_element_type=F32)
    for ii in range(PEER_ROWS):
        rows = slice(ii * N_KEYS, (ii + 1) * N_KEYS)
        for cs in chunks:
            w = None
            for hd in range(PEER_HEADS):
                hit = s2_ref[hd, :, cs] >= floor[hd][ii:ii + 1, cs]
                term = jnp.where(hit, e2_s[hd, :, cs], 0.0) * gain[hd][ii:ii + 1, cs]
                w = term if w is None else w + term
            x = act_s[rows, cs]
            t = jnp.tanh(x * (GELU_C0 + GELU_C1 * (x * x)))
            aw_s[rows, cs] = ((x * w) * (1.0 + t)).astype(BF16)
    acc_s[...] += lax.dot_general(aw_s[...], v_ref[...].astype(BF16), (((0,), (0,)), ((), ())),
                                  preferred_element_type=F32)

    @pl.when(k == pl.num_programs(1) - 1)
    def _():
        y_ref[...] = _rms(x1_ref[...] + m_ref[0][5:6] * acc_s[...], fn_ref[...])


def _peer(h2, s1t, s2t, t2, stats, u_tab, v_tab, x1, m6, mrow, final_norm):
    ntok = h2.shape[0]
    tn, te = PEER_TN, PEER_TE
    tok = pl.BlockSpec((tn, D_MODEL), lambda t, k: (t, 0))
    tab = pl.BlockSpec((te, D_MODEL), lambda t, k: (k, 0))
    keys = pltpu.VMEM((PEER_HEADS, N_KEYS, tn), F32)
    return pl.pallas_call(
        _peer_kernel,
        grid=(ntok // tn, N_EXPERTS // te),
        in_specs=[tok,
                  pl.BlockSpec((PEER_HEADS, PEER_ROWS, tn), lambda t, k: (0, k, t)),
                  pl.BlockSpec((PEER_HEADS, N_KEYS, tn), lambda t, k: (0, 0, t)),
                  pl.BlockSpec((PEER_HEADS, PEER_TOPK, tn), lambda t, k: (0, 0, t)),
                  pl.BlockSpec((4, PEER_HEADS, 1, tn), lambda t, k: (0, 0, 0, t)),
                  tab, tab, tok,
                  pl.BlockSpec((1, 6, D_MODEL), lambda t, k: (mrow(t), 0, 0)),
                  pl.BlockSpec((1, D_MODEL), lambda t, k: (0, 0))],
        out_specs=tok,
        out_shape=jax.ShapeDtypeStruct((ntok, D_MODEL), F32),
        scratch_shapes=[keys, pltpu.VMEM((te, tn), F32), pltpu.VMEM((te, tn), BF16),
                        pltpu.VMEM((tn, D_MODEL), F32)],
        compiler_params=_cparams(("parallel", "arbitrary")),
        name="peer_dense",
    )(h2, s1t, s2t, t2, stats, u_tab, v_tab, x1, m6, final_norm)


def _rot_cols(w):
    j = np.arange(QK_ROPE)
    first = (j % (QK_ROPE // 2)) < (QK_ROPE // 4)
    perm = np.where(first, j + QK_ROPE // 4, j - QK_ROPE // 4)
    sign = np.where(first, -1.0, 1.0).astype(np.float32)
    return w[..., perm] * sign


def _rope_slot(w_rope):
    pad = [(0, 0)] * (w_rope.ndim - 1)
    return jnp.pad(w_rope, pad + [(QK_NOPE, HEAD_PAD - QK_NOPE - QK_ROPE)])


def _rope_tables(n_tokens):
    n_rows = n_tokens // GRID_W
    rows = jnp.repeat(jnp.arange(n_rows, dtype=F32), GRID_W)
    cols = jnp.tile(jnp.arange(GRID_W, dtype=F32), n_rows)
    half = QK_ROPE // 2
    inv_freq = 1.0 / (ROPE_BASE ** (jnp.arange(0, half, 2, dtype=F32) / half))
    ang_r = rows[:, None] * inv_freq
    ang_c = cols[:, None] * inv_freq
    ang = jnp.concatenate([ang_r, ang_r, ang_c, ang_c], axis=-1)
    lead = (QK_NOPE, HEAD_PAD - QK_NOPE - QK_ROPE)
    cos = jnp.pad(jnp.cos(ang), [(0, 0), lead], constant_values=1.0)
    sin = jnp.pad(jnp.sin(ang), [(0, 0), lead])
    return cos, sin


def kernel(x_prompt, x_sample, c, cache_ckv, cache_krope, state_ssm, c_ctx, w_mod, b_mod, norm1, w_in,
           ssm_lam_re, ssm_lam_im, ssm_log_dt, ssm_b_re, ssm_b_im, ssm_c_re, ssm_c_im, ssm_d, w_glu,
           q_norm, w_uq, kv_norm, w_uk, w_uv, w_out, norm2, w_query, sub_keys, u_table, v_table,
           final_norm):
    bc, tc_len, _ = x_prompt.shape
    bl, tl_len, _ = x_sample.shape
    l = 0
    row = lambda a: a.reshape(1, -1)

    n_mod = 8
    cvec = jnp.concatenate([c_ctx[None], c, jnp.zeros((n_mod - 1 - bl, D_MODEL), F32)], 0)
    m6 = _modulation(cvec, w_mod[l], row(b_mod[l])).reshape(n_mod, 6, D_MODEL)

    wi = w_in[l]
    o_kr = SSM_WIDTH + Q_RANK + KV_RANK
    w_kr = wi[:, o_kr:]
    w_in_ext = jnp.concatenate([wi[:, :o_kr], _rope_slot(w_kr), _rope_slot(_rot_cols(w_kr))], 1).astype(BF16)
    wq3 = w_uq[l].reshape(Q_RANK, MLA_HEADS, QK_NOPE + QK_ROPE)
    wq_main = jnp.pad(wq3, [(0, 0), (0, 0), (0, HEAD_PAD - QK_NOPE - QK_ROPE)])
    wq_rot = _rope_slot(_rot_cols(wq3[..., QK_NOPE:]))
    w_uq_ext = jnp.concatenate([wq_main.reshape(Q_RANK, -1), wq_rot.reshape(Q_RANK, -1)], 1).astype(BF16)
    w_uk_ext = jnp.pad(w_uk[l].reshape(KV_RANK, MLA_HEADS, QK_NOPE),
                       [(0, 0), (0, 0), (0, HEAD_PAD - QK_NOPE)]).reshape(KV_RANK, -1).astype(BF16)
    w_uv_b = w_uv[l].astype(BF16)
    w_glu_b = w_glu[l].astype(BF16)
    w_out_s = w_out[l][:SSM_WIDTH].astype(BF16)
    w_out_a = w_out[l][SSM_WIDTH:].astype(BF16)
    w_query_b = w_query[l].astype(BF16)
    sub_keys_b = sub_keys[l].astype(BF16)
    u_tab, v_tab = u_table[l], v_table[l]
    cos, sin = _rope_tables(tl_len)

    bw, cw, a_blk = _s5_params(ssm_lam_re[l], ssm_lam_im[l], ssm_log_dt[l], ssm_b_re[l], ssm_b_im[l],
                               ssm_c_re[l], ssm_c_im[l])

    ctx_row = lambda i: 0
    lat_tiles = tl_len // TOK_TILE
    lat_row = lambda i: 1 + i // lat_tiles
    xc = x_prompt.reshape(bc * tc_len, D_MODEL)
    xl = x_sample.reshape(bl * tl_len, D_MODEL)

    u_c, q_c, ckv_c, kr_c = _pre(xc, m6, ctx_row, row(norm1[l]), w_in_ext, row(q_norm[l]), row(kv_norm[l]),
                                 w_uq_ext[:, :MLA_HEADS * HEAD_PAD], bc, tc_len, None)
    u_l, q_l, ckv_l, kr_l = _pre(xl, m6, lat_row, row(norm1[l]), w_in_ext, row(q_norm[l]), row(kv_norm[l]),
                                 w_uq_ext, bl, tl_len, (cos, sin))

    rl = 8
    h0_c = jnp.zeros((2, 2, S5_BLOCKS_PER_HALF, 2, bc, LANES), F32)
    yf_c, yb_c, st_c = _s5_scan(u_c.reshape(tc_len * bc, SSM_WIDTH), bw, cw, a_blk, h0_c, bc, tc_len,
                                S5_TILE_ROWS // bc)
    u_lp = jnp.pad(u_l.reshape(tl_len, bl, SSM_WIDTH), [(0, 0), (0, rl - bl), (0, 0)])
    h0_l = jnp.pad(_s5_state_in(state_ssm[:, l]), [(0, 0)] * 4 + [(0, rl - bl), (0, 0)])
    yf_l, yb_l, _ = _s5_scan(u_lp.reshape(tl_len * rl, SSM_WIDTH), bw, cw, a_blk, h0_l, rl, tl_len,
                             S5_TILE_ROWS // rl)

    at_c = _attention(q_c, ckv_c.reshape(bc, tc_len, KV_RANK), kr_c.reshape(bc, tc_len, HEAD_PAD),
                      w_uk_ext, w_uv_b, bc, tc_len)
    ckv_all = jnp.concatenate([cache_ckv[:, l], ckv_l.reshape(bl, tl_len, KV_RANK)], 1)
    kr_all = jnp.concatenate([_rope_slot(cache_krope[:, l]), kr_l.reshape(bl, tl_len, HEAD_PAD)], 1)
    at_l = _attention(q_l, ckv_all, kr_all, w_uk_ext, w_uv_b, bl, tl_len)

    tm_c = lambda a: a.reshape(tc_len, bc * SSM_WIDTH)
    tm_l = lambda a: a.reshape(tl_len, rl * SSM_WIDTH)
    ntok = bc * tc_len + bl * tl_len
    x1, h2, s1t, s2t = _post((xc, at_c, tm_c(u_c), tm_c(yf_c), tm_c(yb_c), bc, tc_len),
                             (xl, at_l, tm_l(u_lp), tm_l(yf_l), tm_l(yb_l), bl, tl_len),
                             m6, row(ssm_d[l]), w_glu_b, w_out_s, w_out_a, row(norm2[l]), w_query_b, sub_keys_b)

    dense = lambda a: a.reshape(PEER_HEADS, N_KEYS, ntok // LANES, LANES)
    stats, t2 = _route(dense(s1t), dense(s2t))
    stats = stats.reshape(4, PEER_HEADS, 1, ntok)
    t2 = t2.reshape(PEER_HEADS, PEER_TOPK, ntok)
    ctx_peer_tiles = bc * tc_len // PEER_TN
    lat_peer_tiles = tl_len // PEER_TN
    peer_row = lambda t: jnp.where(t < ctx_peer_tiles, 0, 1 + (t - ctx_peer_tiles) // lat_peer_tiles)
    y = _peer(h2, s1t, s2t, t2, stats, u_tab, v_tab, x1, m6, peer_row, row(final_norm))

    y_prompt = y[:bc * tc_len].reshape(bc, tc_len, D_MODEL)
    y_sample = y[bc * tc_len:].reshape(bl, tl_len, D_MODEL)
    new_ckv = ckv_c.reshape(bc, 1, tc_len, KV_RANK)
    new_krope = kr_c[:, QK_NOPE:QK_NOPE + QK_ROPE].reshape(bc, 1, tc_len, QK_ROPE)
    new_ssm = _s5_state_out(st_c)[:, None]
    return (y_prompt, y_sample, new_ckv, new_krope, new_ssm)
```

```python
import functools
import math

import jax
import jax.numpy as jnp
import numpy as np
from jax import lax
from jax.experimental import pallas as pl
from jax.experimental.pallas import tpu as pltpu

F32 = jnp.float32
BF16 = jnp.bfloat16

D_MODEL = 1024
GRID_W = 64
EPS = 1e-6
SSM_WIDTH = 512
SSM_GROUP = 16
SSM_GROUPS = 32
SSM_STATE = 64
MLA_HEADS = 8
QK_NOPE = 64
QK_ROPE = 32
V_DIM = 64
Q_RANK = 384
KV_RANK = 256
ROPE_BASE = 10000.0
N_KEYS = 128
N_EXPERTS = N_KEYS * N_KEYS
PEER_HEADS = 8
PEER_TOPK = 16
KEY_DIM = 128

LANES = 128
HEAD_PAD = 128
TOK_TILE = 256
S5_PAIR = 2
S5_BLOCKS = SSM_GROUPS // S5_PAIR
S5_HALF = SSM_WIDTH // 2
S5_BLOCKS_PER_HALF = S5_BLOCKS // 2
S5_TILE_ROWS = 512
PEER_TN = 512
PEER_ROWS = 8
PEER_TE = PEER_ROWS * N_KEYS
GELU_C0 = math.sqrt(2.0 / math.pi)
GELU_C1 = 0.044715 * GELU_C0
VMEM_LIMIT = 54 * 1024 * 1024


def _cparams(sem):
    return pltpu.CompilerParams(dimension_semantics=sem, vmem_limit_bytes=VMEM_LIMIT)


def _rms(x, g):
    return x * lax.rsqrt(jnp.mean(x * x, axis=-1, keepdims=True) + EPS) * g


def _mod_kernel(c_ref, w_ref, b_ref, o_ref):
    o_ref[...] = jnp.dot(jax.nn.silu(c_ref[...]), w_ref[...], preferred_element_type=F32) + b_ref[...]


def _modulation(cvec, w_mod, b_mod):
    rows, d = cvec.shape
    n = w_mod.shape[1]
    tn = 1536
    return pl.pallas_call(
        _mod_kernel,
        grid=(n // tn,),
        in_specs=[pl.BlockSpec((rows, d), lambda j: (0, 0)),
                  pl.BlockSpec((d, tn), lambda j: (0, j)),
                  pl.BlockSpec((1, tn), lambda j: (0, j))],
        out_specs=pl.BlockSpec((rows, tn), lambda j: (0, j)),
        out_shape=jax.ShapeDtypeStruct((rows, n), F32),
        compiler_params=_cparams(("parallel",)),
        name="modulation",
    )(cvec, w_mod, b_mod)


def _pre_kernel(use_rope, x_ref, m_ref, n1_ref, win_ref, qn_ref, kvn_ref, wuq_ref, *rest):
    if use_rope:
        cos_ref, sin_ref, u_ref, q_ref, ckv_ref, kr_ref = rest
    else:
        u_ref, q_ref, ckv_ref, kr_ref = rest
    m = m_ref[0]
    h = _rms(x_ref[...], n1_ref[...]) * (1.0 + m[1:2]) + m[0:1]
    z = jnp.dot(h.astype(BF16), win_ref[...], preferred_element_type=F32)
    u_ref[...] = z[:, :SSM_WIDTH]
    o_q, o_kv, o_kr = SSM_WIDTH, SSM_WIDTH + Q_RANK, SSM_WIDTH + Q_RANK + KV_RANK
    cqn = _rms(z[:, o_q:o_kv], qn_ref[...])
    qq = jnp.dot(cqn.astype(BF16), wuq_ref[...], preferred_element_type=F32)
    ckv_ref[...] = _rms(z[:, o_kv:o_kr], kvn_ref[...])
    kr = z[:, o_kr:o_kr + HEAD_PAD]
    scale = (QK_NOPE + QK_ROPE) ** -0.5
    nq = MLA_HEADS * HEAD_PAD
    if use_rope:
        cos, sin = cos_ref[...], sin_ref[...]
        kr = kr * cos + z[:, o_kr + HEAD_PAD:o_kr + 2 * HEAD_PAD] * sin
        for hd in range(MLA_HEADS):
            sl = slice(hd * HEAD_PAD, (hd + 1) * HEAD_PAD)
            qh = qq[:, sl] * cos + qq[:, nq + hd * HEAD_PAD:nq + (hd + 1) * HEAD_PAD] * sin
            q_ref[:, sl] = (qh * scale).astype(BF16)
    else:
        q_ref[...] = (qq[:, :nq] * scale).astype(BF16)
    kr_ref[...] = kr


def _pre(x2d, m6, mrow, norm1, w_in_ext, q_norm, kv_norm, w_uq_ext, batch, seq, rope):
    ntok = batch * seq
    n_t = seq // TOK_TILE
    use_rope = rope is not None
    full = lambda a: pl.BlockSpec(a.shape, lambda i: (0,) * a.ndim)
    in_specs = [pl.BlockSpec((TOK_TILE, D_MODEL), lambda i: (i, 0)),
                pl.BlockSpec((1, 6, D_MODEL), lambda i: (mrow(i), 0, 0)),
                full(norm1), full(w_in_ext), full(q_norm), full(kv_norm), full(w_uq_ext)]
    args = [x2d, m6, norm1, w_in_ext, q_norm, kv_norm, w_uq_ext]
    if use_rope:
        in_specs += [pl.BlockSpec((TOK_TILE, HEAD_PAD), lambda i: (i % n_t, 0))] * 2
        args += list(rope)
    out_specs = [pl.BlockSpec((TOK_TILE, SSM_WIDTH), lambda i: (i % n_t, i // n_t)),
                 pl.BlockSpec((TOK_TILE, MLA_HEADS * HEAD_PAD), lambda i: (i, 0)),
                 pl.BlockSpec((TOK_TILE, KV_RANK), lambda i: (i, 0)),
                 pl.BlockSpec((TOK_TILE, HEAD_PAD), lambda i: (i, 0))]
    out_shape = [jax.ShapeDtypeStruct((seq, batch * SSM_WIDTH), F32),
                 jax.ShapeDtypeStruct((ntok, MLA_HEADS * HEAD_PAD), BF16),
                 jax.ShapeDtypeStruct((ntok, KV_RANK), F32),
                 jax.ShapeDtypeStruct((ntok, HEAD_PAD), F32)]
    return pl.pallas_call(
        functools.partial(_pre_kernel, use_rope),
        grid=(ntok // TOK_TILE,),
        in_specs=in_specs, out_specs=out_specs, out_shape=out_shape,
        compiler_params=_cparams(("parallel",)),
        name="pre_rope" if use_rope else "pre",
    )(*args)


def _s5_param_kernel(lr_ref, li_ref, ldt_ref, lrx_ref, lix_ref, ldtx_ref, bre_ref, bim_ref, cim_ref,
                     abr_ref, abi_ref, bfr_ref, bfi_ref, ncim_ref):
    def disc(lr, li, ldt):
        dt = jnp.exp(ldt)
        mag = jnp.exp(lr * dt)
        ab_re, ab_im = mag * jnp.cos(li * dt), mag * jnp.sin(li * dt)
        den = lr * lr + li * li
        br, bi = lr / den, -li / den
        ar = ab_re - 1.0
        return ab_re, ab_im, ar * br - ab_im * bi, ar * bi + ab_im * br

    ab_re, ab_im, _, _ = disc(lr_ref[...], li_ref[...], ldt_ref[...])
    abr_ref[...] = ab_re
    abi_ref[...] = ab_im
    _, _, f_re, f_im = disc(lrx_ref[...], lix_ref[...], ldtx_ref[...])
    b_re, b_im = bre_ref[...], bim_ref[...]
    bfr_ref[...] = f_re * b_re - f_im * b_im
    bfi_ref[...] = f_re * b_im + f_im * b_re
    ncim_ref[...] = -cim_ref[...]


def _s5_params(lam_re, lam_im, log_dt, b_re, b_im, c_re, c_im):
    dg = 2 * SSM_GROUPS
    n, p = SSM_STATE, SSM_GROUP
    lr = lam_re.reshape(dg, n)
    li = lam_im.reshape(dg, n)
    ldt = jnp.broadcast_to(log_dt.reshape(dg, 1), (dg, n))
    rep = lambda a: jnp.repeat(a, p, axis=1)
    args = [lr, li, ldt, rep(lr), rep(li), rep(ldt),
            b_re.reshape(dg, n * p), b_im.reshape(dg, n * p), c_im.reshape(dg, p * n)]
    small = jax.ShapeDtypeStruct((dg, n), F32)
    big = jax.ShapeDtypeStruct((dg, n * p), F32)
    ab_re, ab_im, bf_re, bf_im, ncim = pl.pallas_call(
        _s5_param_kernel, out_shape=[small, small, big, big, big], name="s5_params")(*args)

    nb, gp = S5_BLOCKS, S5_PAIR
    eye = jnp.eye(gp, dtype=F32)
    pos = np.arange(nb) % S5_BLOCKS_PER_HALF
    bf = jnp.stack([bf_re, bf_im], 0).reshape(2, 2, nb, gp, n, p)
    bc = jnp.einsum('adbgnp,gh->bdgpahn', bf, eye).reshape(nb, 2, gp * p, 2 * gp * n)
    bw = jnp.zeros((nb, 2, S5_BLOCKS_PER_HALF, gp * p, 2 * gp * n), F32)
    bw = bw.at[np.arange(nb), :, pos].set(bc).reshape(nb, 2, S5_HALF, 2 * gp * n)
    cc = jnp.stack([c_re.reshape(2, nb, gp, p, n), ncim.reshape(2, nb, gp, p, n)], 0)
    cc = jnp.einsum('adbgpn,gh->bdagnhp', cc, eye).reshape(nb, 2, 2 * gp * n, gp * p)
    cw = jnp.zeros((nb, 2, 2 * gp * n, S5_BLOCKS_PER_HALF, gp * p), F32)
    cw = cw.at[np.arange(nb), :, :, pos].set(cc).reshape(nb, 2, 2 * gp * n, S5_HALF)
    a = jnp.stack([ab_re, ab_im], 0).reshape(2, 2, nb, gp * n).transpose(2, 1, 0, 3)
    hb = S5_BLOCKS_PER_HALF
    bw = bw.reshape(2, hb, 2, S5_HALF, 2 * gp * n).transpose(0, 2, 3, 1, 4).reshape(2, 2, S5_HALF, -1)
    cw = cw.reshape(2, hb, 2, 2 * gp * n, S5_HALF).transpose(0, 2, 1, 3, 4).reshape(2, 2, -1, S5_HALF)
    a = a.reshape(2, hb, 2, 2, 1, gp * n).transpose(0, 2, 1, 3, 4, 5)
    return bw.astype(BF16), cw.astype(BF16), a


def _s5_state_in(h):
    bsz = h.shape[0]
    return h.reshape(bsz, 2, 2, 2, S5_BLOCKS_PER_HALF, LANES).transpose(3, 1, 4, 2, 0, 5)


def _s5_state_out(st):
    bsz = st.shape[4]
    return st.transpose(4, 1, 3, 0, 2, 5).reshape(bsz, 2, 2, SSM_GROUPS, SSM_STATE)


def _s5_kernel(rows_per_step, steps, uf_ref, ub_ref, bw_ref, cw_ref, a_ref, h0_ref,
               yf_ref, yb_ref, st_ref, buf, coef, carry):
    r = rows_per_step
    nb = S5_BLOCKS_PER_HALF
    width = 2 * LANES

    @pl.when(pl.program_id(1) == 0)
    def _():
        carry[...] = h0_ref[0]
        coef[...] = jnp.broadcast_to(a_ref[0], coef.shape)

    for d, u_ref in enumerate((uf_ref, ub_ref)):
        buf[d] = jnp.dot(u_ref[...].astype(BF16), bw_ref[0, d], preferred_element_type=F32)

    chains = [(d, b) for d in range(2) for b in range(nb)]
    for g in range(r // 8):
        rows8 = slice(g * 8, (g + 1) * 8)

        def body(k, hs, rows8=rows8):
            out = []
            for (d, b), (h_re, h_im) in zip(chains, hs):
                t = k if d == 0 else steps - 1 - k
                r0 = pl.multiple_of(t * r + g * 8, 8)
                a_re, a_im = coef[d, b, 0], coef[d, b, 1]
                re_l, im_l = slice(b * width, b * width + LANES), slice(b * width + LANES, (b + 1) * width)
                n_re = a_re * h_re - a_im * h_im + buf[d, pl.ds(r0, 8), re_l]
                n_im = a_re * h_im + a_im * h_re + buf[d, pl.ds(r0, 8), im_l]
                buf[d, pl.ds(r0, 8), re_l] = n_re
                buf[d, pl.ds(r0, 8), im_l] = n_im
                out.append((n_re, n_im))
            return tuple(out)

        init = tuple((carry[d, b, 0, rows8], carry[d, b, 1, rows8]) for d, b in chains)
        final = lax.fori_loop(0, steps, body, init)
        for (d, b), (h_re, h_im) in zip(chains, final):
            carry[d, b, 0, rows8] = h_re
            carry[d, b, 1, rows8] = h_im

    for d, y_ref in enumerate((yf_ref, yb_ref)):
        y_ref[...] = jnp.dot(buf[d].astype(BF16), cw_ref[0, d], preferred_element_type=F32)
    st_ref[0] = carry[...]


def _s5_scan(u_tm, bw, cw, a, h0, rows_per_step, seq, steps):
    r = rows_per_step
    n_t = seq // steps
    rows = steps * r
    nb = S5_BLOCKS_PER_HALF
    wide = nb * 2 * LANES
    half4 = lambda h, t: (h, 0, 0, 0)
    half6 = lambda h, t: (h, 0, 0, 0, 0, 0)
    state = (2, nb, 2, r, LANES)
    in_specs = [pl.BlockSpec((rows, S5_HALF), lambda h, t: (t, h)),
                pl.BlockSpec((rows, S5_HALF), lambda h, t: (n_t - 1 - t, h)),
                pl.BlockSpec((1, 2, S5_HALF, wide), half4),
                pl.BlockSpec((1, 2, wide, S5_HALF), half4),
                pl.BlockSpec((1, 2, nb, 2, 1, LANES), half6),
                pl.BlockSpec((1,) + state, half6)]
    out_specs = [pl.BlockSpec((rows, S5_HALF), lambda h, t: (t, h)),
                 pl.BlockSpec((rows, S5_HALF), lambda h, t: (n_t - 1 - t, h)),
                 pl.BlockSpec((1,) + state, half6)]
    out_shape = [jax.ShapeDtypeStruct(u_tm.shape, F32), jax.ShapeDtypeStruct(u_tm.shape, F32),
                 jax.ShapeDtypeStruct((2,) + state, F32)]
    return pl.pallas_call(
        functools.partial(_s5_kernel, r, steps),
        grid=(2, n_t),
        in_specs=in_specs, out_specs=out_specs, out_shape=out_shape,
        scratch_shapes=[pltpu.VMEM((2, rows, wide), F32), pltpu.VMEM((2, nb, 2, 8, LANES), F32),
                        pltpu.VMEM(state, F32)],
        compiler_params=_cparams(("arbitrary", "arbitrary")),
        name="s5_scan",
    )(u_tm, u_tm, bw, cw, a, h0)


def _attn_kernel(q_ref, ckv_ref, kr_ref, wuk_ref, wuv_ref, o_ref, k_s, v_s):
    @pl.when(pl.program_id(1) == 0)
    def _():
        kv = ckv_ref[0].astype(BF16)
        kn = jnp.dot(kv, wuk_ref[...], preferred_element_type=F32)
        kr = kr_ref[0]
        for hd in range(MLA_HEADS):
            k_s[hd] = (kn[:, hd * HEAD_PAD:(hd + 1) * HEAD_PAD] + kr).astype(BF16)
        v_s[...] = jnp.dot(kv, wuv_ref[...], preferred_element_type=F32).astype(BF16)

    for hd in range(MLA_HEADS):
        qh = q_ref[:, hd * HEAD_PAD:(hd + 1) * HEAD_PAD]
        s = lax.dot_general(qh, k_s[hd], (((1,), (1,)), ((), ())), preferred_element_type=F32)
        p = jnp.exp(s - jnp.max(s, axis=-1, keepdims=True))
        l = jnp.sum(p, axis=-1, keepdims=True)
        o = jnp.dot(p.astype(BF16), v_s[:, hd * V_DIM:(hd + 1) * V_DIM], preferred_element_type=F32)
        o_ref[:, hd * V_DIM:(hd + 1) * V_DIM] = (o / l).astype(BF16)


def _attention(q, ckv_all, kr_all, w_uk_ext, w_uv, batch, seq):
    s_len = ckv_all.shape[1]
    n_q = seq // TOK_TILE
    return pl.pallas_call(
        _attn_kernel,
        grid=(batch, n_q),
        in_specs=[pl.BlockSpec((TOK_TILE, MLA_HEADS * HEAD_PAD), lambda b, i: (b * n_q + i, 0)),
                  pl.BlockSpec((1, s_len, KV_RANK), lambda b, i: (b, 0, 0)),
                  pl.BlockSpec((1, s_len, HEAD_PAD), lambda b, i: (b, 0, 0)),
                  pl.BlockSpec(w_uk_ext.shape, lambda b, i: (0, 0)),
                  pl.BlockSpec(w_uv.shape, lambda b, i: (0, 0))],
        out_specs=pl.BlockSpec((TOK_TILE, MLA_HEADS * V_DIM), lambda b, i: (b * n_q + i, 0)),
        out_shape=jax.ShapeDtypeStruct((batch * seq, MLA_HEADS * V_DIM), BF16),
        scratch_shapes=[pltpu.VMEM((MLA_HEADS, s_len, HEAD_PAD), BF16),
                        pltpu.VMEM((s_len, MLA_HEADS * V_DIM), BF16)],
        compiler_params=_cparams(("parallel", "arbitrary")),
        name="attention",
    )(q, ckv_all, kr_all, w_uk_ext, w_uv)


def _post_kernel(n_ctx, xc_ref, atc_ref, uc_ref, yfc_ref, ybc_ref, xl_ref, atl_ref, ul_ref, yfl_ref, ybl_ref,
                 m_ref, d_ref, wglu_ref, wos_ref, woa_ref, n2_ref, wq_ref, sk_ref,
                 x1_ref, h2_ref, s1_ref, s2_ref):
    is_ctx = pl.program_id(0) < n_ctx
    pick = lambda c_ref, l_ref: jnp.where(is_ctx, c_ref[...], l_ref[...])
    m = m_ref[0]
    y = pick(yfc_ref, yfl_ref) + pick(ybc_ref, ybl_ref) + pick(uc_ref, ul_ref) * d_ref[...]
    yg = jax.nn.gelu(y)
    gate = jax.nn.sigmoid(jnp.dot(yg.astype(BF16), wglu_ref[...], preferred_element_type=F32))
    mix = (jnp.dot((yg * gate).astype(BF16), wos_ref[...], preferred_element_type=F32)
           + jnp.dot(pick(atc_ref, atl_ref), woa_ref[...], preferred_element_type=F32))
    x1 = pick(xc_ref, xl_ref) + m[2:3] * mix
    x1_ref[...] = x1
    h2 = (_rms(x1, n2_ref[...]) * (1.0 + m[4:5]) + m[3:4]).astype(BF16)
    h2_ref[...] = h2
    qp = jnp.dot(h2, wq_ref[...], preferred_element_type=F32).astype(BF16)
    for hd in range(PEER_HEADS):
        for half, s_ref in enumerate((s1_ref, s2_ref)):
            c0 = (hd * 2 + half) * KEY_DIM
            s_ref[hd] = lax.dot_general(sk_ref[hd, half], qp[:, c0:c0 + KEY_DIM],
                                        (((1,), (1,)), ((), ())), preferred_element_type=F32)


def _post(ctx, lat, m6, ssm_d, w_glu, w_out_s, w_out_a, norm2, w_query, sub_keys):
    specs, args = [], []
    n_ctx, n_lat = (p[5] * p[6] // TOK_TILE for p in (ctx, lat))
    for pass_id, (x2d, attn, u_tm, yf, yb, batch, seq) in enumerate((ctx, lat)):
        n_t = seq // TOK_TILE
        loc = (lambda i: jnp.minimum(i, n_ctx - 1)) if pass_id == 0 else (lambda i: jnp.maximum(i - n_ctx, 0))
        tok = lambda w, loc=loc: pl.BlockSpec((TOK_TILE, w), lambda i: (loc(i), 0))
        tm = pl.BlockSpec((TOK_TILE, SSM_WIDTH), lambda i, loc=loc, n_t=n_t: (loc(i) % n_t, loc(i) // n_t))
        specs += [tok(D_MODEL), tok(MLA_HEADS * V_DIM), tm, tm, tm]
        args += [x2d, attn, u_tm, yf, yb]
    lat_tiles_per_row = lat[6] // TOK_TILE
    mrow = lambda i: jnp.where(i < n_ctx, 0, 1 + (i - n_ctx) // lat_tiles_per_row)
    full = lambda a: pl.BlockSpec(a.shape, lambda i: (0,) * a.ndim)
    weights = [ssm_d, w_glu, w_out_s, w_out_a, norm2, w_query, sub_keys]
    total = (n_ctx + n_lat) * TOK_TILE
    tok_out = lambda w: pl.BlockSpec((TOK_TILE, w), lambda i: (i, 0))
    sc = pl.BlockSpec((PEER_HEADS, N_KEYS, TOK_TILE), lambda i: (0, 0, i))
    return pl.pallas_call(
        functools.partial(_post_kernel, n_ctx),
        grid=(n_ctx + n_lat,),
        in_specs=specs + [pl.BlockSpec((1, 6, D_MODEL), lambda i: (mrow(i), 0, 0))] + [full(w) for w in weights],
        out_specs=[tok_out(D_MODEL), tok_out(D_MODEL), sc, sc],
        out_shape=[jax.ShapeDtypeStruct((total, D_MODEL), F32), jax.ShapeDtypeStruct((total, D_MODEL), BF16),
                   jax.ShapeDtypeStruct((PEER_HEADS, N_KEYS, total), F32),
                   jax.ShapeDtypeStruct((PEER_HEADS, N_KEYS, total), F32)],
        compiler_params=_cparams(("arbitrary",)),
        name="post",
    )(*args, m6, *weights)


def _sort_pairs(lo, hi):
    def merge(lo, hi, r):
        step = r * 2
        if step < hi - lo:
            yield from merge(lo, hi, step)
            yield from merge(lo + r, hi, step)
            for i in range(lo + r, hi - r, step):
                yield (i, i + r)
        else:
            yield (lo, lo + r)

    if hi - lo >= 1:
        mid = lo + (hi - lo) // 2
        yield from _sort_pairs(lo, mid)
        yield from _sort_pairs(mid + 1, hi)
        yield from merge(lo, hi, 1)


def _sort_desc(vals):
    vals = list(vals)
    for i, j in _sort_pairs(0, len(vals) - 1):
        vals[i], vals[j] = jnp.maximum(vals[i], vals[j]), jnp.minimum(vals[i], vals[j])
    return vals


def _merge_top(a, b):
    n = len(a)
    c = [jnp.maximum(a[k], b[n - 1 - k]) for k in range(n)]
    stride = n // 2
    while stride:
        for i in range(n):
            if not i & stride:
                c[i], c[i + stride] = jnp.maximum(c[i], c[i + stride]), jnp.minimum(c[i], c[i + stride])
        stride //= 2
    return c


def _top_keys(ref):
    k = PEER_TOPK
    groups = [_sort_desc([ref[0, g * k + r] for r in range(k)]) for g in range(N_KEYS // k)]
    while len(groups) > 1:
        groups = [_merge_top(groups[i], groups[i + 1]) for i in range(0, len(groups), 2)]
    return groups[0]


def _route_kernel(s1_ref, s2_ref, o_ref, t2_ref):
    k = PEER_TOPK
    t1, t2 = _top_keys(s1_ref), _top_keys(s2_ref)
    cands = [t1[a] + t2[b] for a in range(k) for b in range(k) if (a + 1) * (b + 1) <= k]
    size = 1 << (len(cands) - 1).bit_length()
    cands += [jnp.full_like(t1[0], -jnp.inf)] * (size - len(cands))
    best = _sort_desc(cands)
    z = jnp.ones_like(best[0])
    for v in best[1:k]:
        z = z + jnp.exp(v - best[0])
    o_ref[0, 0] = best[k - 1]
    o_ref[1, 0] = t1[0]
    o_ref[2, 0] = t2[0]
    o_ref[3, 0] = 1.0 / z
    for b in range(k):
        t2_ref[0, b] = t2[b]


def _route(s1d, s2d):
    n_rows = s1d.shape[2]
    spec = pl.BlockSpec((1, N_KEYS, 8, LANES), lambda i, h: (h, 0, i, 0))
    return pl.pallas_call(
        _route_kernel,
        grid=(n_rows // 8, PEER_HEADS),
        in_specs=[spec, spec],
        out_specs=[pl.BlockSpec((4, 1, 8, LANES), lambda i, h: (0, h, i, 0)),
                   pl.BlockSpec((1, PEER_TOPK, 8, LANES), lambda i, h: (h, 0, i, 0))],
        out_shape=[jax.ShapeDtypeStruct((4, PEER_HEADS, n_rows, LANES), F32),
                   jax.ShapeDtypeStruct((PEER_HEADS, PEER_TOPK, n_rows, LANES), F32)],
        compiler_params=_cparams(("parallel", "parallel")),
        name="peer_route",
    )(s1d, s2d)


def _peer_kernel(h2_ref, s1_ref, s2_ref, t2_ref, st_ref, u_ref, v_ref, x1_ref, m_ref, fn_ref, y_ref,
                 e2_s, act_s, aw_s, acc_s):
    k = pl.program_id(1)
    chunks = [slice(c * LANES, (c + 1) * LANES) for c in range(PEER_TN // LANES)]

    @pl.when(k == 0)
    def _():
        for hd in range(PEER_HEADS):
            e2_s[hd] = jnp.exp(s2_ref[hd] - st_ref[2, hd])
        acc_s[...] = jnp.zeros_like(acc_s)

    floor, gain = [], []
    for hd in range(PEER_HEADS):
        s1 = s1_ref[hd]
        low = jnp.full_like(s1, jnp.inf)
        for b in range(PEER_TOPK):
            t2b = t2_ref[hd, b:b + 1, :]
            low = jnp.where(s1 + t2b >= st_ref[0, hd], t2b, low)
        floor.append(low)
        gain.append(jnp.exp(s1 - st_ref[1, hd]) * (0.5 * st_ref[3, hd]))

    act_s[...] = lax.dot_general(u_ref[...].astype(BF16), h2_ref[...], (((1,), (1,)), ((), ())),
                                 preferred_element_type=F32)
    for ii in range(PEER_ROWS):
        rows = slice(ii * N_KEYS, (ii + 1) * N_KEYS)
        for cs in chunks:
            w = None
            for hd in range(PEER_HEADS):
                hit = s2_ref[hd, :, cs] >= floor[hd][ii:ii + 1, cs]
                term = jnp.where(hit, e2_s[hd, :, cs], 0.0) * gain[hd][ii:ii + 1, cs]
                w = term if w is None else w + term
            x = act_s[rows, cs]
            t = jnp.tanh(x * (GELU_C0 + GELU_C1 * (x * x)))
            aw_s[rows, cs] = ((x * w) * (1.0 + t)).astype(BF16)
    acc_s[...] += lax.dot_general(aw_s[...], v_ref[...].astype(BF16), (((0,), (0,)), ((), ())),
                                  preferred_element_type=F32)

    @pl.when(k == pl.num_programs(1) - 1)
    def _():
        y_ref[...] = _rms(x1_ref[...] + m_ref[0][5:6] * acc_s[...], fn_ref[...])


def _peer(h2, s1t, s2t, t2, stats, u_tab, v_tab, x1, m6, mrow, final_norm):
    ntok = h2.shape[0]
    tn, te = PEER_TN, PEER_TE
    tok = pl.BlockSpec((tn, D_MODEL), lambda t, k: (t, 0))
    tab = pl.BlockSpec((te, D_MODEL), lambda t, k: (k, 0))
    keys = pltpu.VMEM((PEER_HEADS, N_KEYS, tn), F32)
    return pl.pallas_call(
        _peer_kernel,
        grid=(ntok // tn, N_EXPERTS // te),
        in_specs=[tok,
                  pl.BlockSpec((PEER_HEADS, PEER_ROWS, tn), lambda t, k: (0, k, t)),
                  pl.BlockSpec((PEER_HEADS, N_KEYS, tn), lambda t, k: (0, 0, t)),
                  pl.BlockSpec((PEER_HEADS, PEER_TOPK, tn), lambda t, k: (0, 0, t)),
                  pl.BlockSpec((4, PEER_HEADS, 1, tn), lambda t, k: (0, 0, 0, t)),
                  tab, tab, tok,
                  pl.BlockSpec((1, 6, D_MODEL), lambda t, k: (mrow(t), 0, 0)),
                  pl.BlockSpec((1, D_MODEL), lambda t, k: (0, 0))],
        out_specs=tok,
        out_shape=jax.ShapeDtypeStruct((ntok, D_MODEL), F32),
        scratch_shapes=[keys, pltpu.VMEM((te, tn), F32), pltpu.VMEM((te, tn), BF16),
                        pltpu.VMEM((tn, D_MODEL), F32)],
        compiler_params=_cparams(("parallel", "arbitrary")),
        name="peer_dense",
    )(h2, s1t, s2t, t2, stats, u_tab, v_tab, x1, m6, final_norm)


def _rot_cols(w):
    j = np.arange(QK_ROPE)
    first = (j % (QK_ROPE // 2)) < (QK_ROPE // 4)
    perm = np.where(first, j + QK_ROPE // 4, j - QK_ROPE // 4)
    sign = np.where(first, -1.0, 1.0).astype(np.float32)
    return w[..., perm] * sign


def _rope_slot(w_rope):
    pad = [(0, 0)] * (w_rope.ndim - 1)
    return jnp.pad(w_rope, pad + [(QK_NOPE, HEAD_PAD - QK_NOPE - QK_ROPE)])


def _rope_tables(n_tokens):
    n_rows = n_tokens // GRID_W
    rows = jnp.repeat(jnp.arange(n_rows, dtype=F32), GRID_W)
    cols = jnp.tile(jnp.arange(GRID_W, dtype=F32), n_rows)
    half = QK_ROPE // 2
    inv_freq = 1.0 / (ROPE_BASE ** (jnp.arange(0, half, 2, dtype=F32) / half))
    ang_r = rows[:, None] * inv_freq
    ang_c = cols[:, None] * inv_freq
    ang = jnp.concatenate([ang_r, ang_r, ang_c, ang_c], axis=-1)
    lead = (QK_NOPE, HEAD_PAD - QK_NOPE - QK_ROPE)
    cos = jnp.pad(jnp.cos(ang), [(0, 0), lead], constant_values=1.0)
    sin = jnp.pad(jnp.sin(ang), [(0, 0), lead])
    return cos, sin


def kernel(x_prompt, x_sample, c, cache_ckv, cache_krope, state_ssm, c_ctx, w_mod, b_mod, norm1, w_in,
           ssm_lam_re, ssm_lam_im, ssm_log_dt, ssm_b_re, ssm_b_im, ssm_c_re, ssm_c_im, ssm_d, w_glu,
           q_norm, w_uq, kv_norm, w_uk, w_uv, w_out, norm2, w_query, sub_keys, u_table, v_table,
           final_norm):
    bc, tc_len, _ = x_prompt.shape
    bl, tl_len, _ = x_sample.shape
    l = 0
    row = lambda a: a.reshape(1, -1)
    assert w_mod.shape[0] == 1
    big = lambda a: a.reshape(a.shape[1:])

    n_mod = 8
    cvec = jnp.concatenate([c_ctx[None], c, jnp.zeros((n_mod - 1 - bl, D_MODEL), F32)], 0)
    m6 = _modulation(cvec, big(w_mod), row(b_mod[l])).reshape(n_mod, 6, D_MODEL)

    wi = w_in[l]
    o_kr = SSM_WIDTH + Q_RANK + KV_RANK
    w_kr = wi[:, o_kr:]
    w_in_ext = jnp.concatenate([wi[:, :o_kr], _rope_slot(w_kr), _rope_slot(_rot_cols(w_kr))], 1).astype(BF16)
    wq3 = w_uq[l].reshape(Q_RANK, MLA_HEADS, QK_NOPE + QK_ROPE)
    wq_main = jnp.pad(wq3, [(0, 0), (0, 0), (0, HEAD_PAD - QK_NOPE - QK_ROPE)])
    wq_rot = _rope_slot(_rot_cols(wq3[..., QK_NOPE:]))
    w_uq_ext = jnp.concatenate([wq_main.reshape(Q_RANK, -1), wq_rot.reshape(Q_RANK, -1)], 1).astype(BF16)
    w_uk_ext = jnp.pad(w_uk[l].reshape(KV_RANK, MLA_HEADS, QK_NOPE),
                       [(0, 0), (0, 0), (0, HEAD_PAD - QK_NOPE)]).reshape(KV_RANK, -1).astype(BF16)
    w_uv_b = w_uv[l].astype(BF16)
    w_glu_b = w_glu[l].astype(BF16)
    w_out_s = w_out[l][:SSM_WIDTH].astype(BF16)
    w_out_a = w_out[l][SSM_WIDTH:].astype(BF16)
    w_query_b = w_query[l].astype(BF16)
    sub_keys_b = sub_keys[l].astype(BF16)
    u_tab, v_tab = big(u_table), big(v_table)
    cos, sin = _rope_tables(tl_len)

    bw, cw, a_blk = _s5_params(ssm_lam_re[l], ssm_lam_im[l], ssm_log_dt[l], ssm_b_re[l], ssm_b_im[l],
                               ssm_c_re[l], ssm_c_im[l])

    ctx_row = lambda i: 0
    lat_tiles = tl_len // TOK_TILE
    lat_row = lambda i: 1 + i // lat_tiles
    xc = x_prompt.reshape(bc * tc_len, D_MODEL)
    xl = x_sample.reshape(bl * tl_len, D_MODEL)

    u_c, q_c, ckv_c, kr_c = _pre(xc, m6, ctx_row, row(norm1[l]), w_in_ext, row(q_norm[l]), row(kv_norm[l]),
                                 w_uq_ext[:, :MLA_HEADS * HEAD_PAD], bc, tc_len, None)
    u_l, q_l, ckv_l, kr_l = _pre(xl, m6, lat_row, row(norm1[l]), w_in_ext, row(q_norm[l]), row(kv_norm[l]),
                                 w_uq_ext, bl, tl_len, (cos, sin))

    rl = 8
    h0_c = jnp.zeros((2, 2, S5_BLOCKS_PER_HALF, 2, bc, LANES), F32)
    yf_c, yb_c, st_c = _s5_scan(u_c.reshape(tc_len * bc, SSM_WIDTH), bw, cw, a_blk, h0_c, bc, tc_len,
                                S5_TILE_ROWS // bc)
    u_lp = jnp.pad(u_l.reshape(tl_len, bl, SSM_WIDTH), [(0, 0), (0, rl - bl), (0, 0)])
    h0_l = jnp.pad(_s5_state_in(state_ssm[:, l]), [(0, 0)] * 4 + [(0, rl - bl), (0, 0)])
    yf_l, yb_l, _ = _s5_scan(u_lp.reshape(tl_len * rl, SSM_WIDTH), bw, cw, a_blk, h0_l, rl, tl_len,
                             S5_TILE_ROWS // rl)

    at_c = _attention(q_c, ckv_c.reshape(bc, tc_len, KV_RANK), kr_c.reshape(bc, tc_len, HEAD_PAD),
                      w_uk_ext, w_uv_b, bc, tc_len)
    ckv_all = jnp.concatenate([cache_ckv[:, l], ckv_l.reshape(bl, tl_len, KV_RANK)], 1)
    kr_all = jnp.concatenate([_rope_slot(cache_krope[:, l]), kr_l.reshape(bl, tl_len, HEAD_PAD)], 1)
    at_l = _attention(q_l, ckv_all, kr_all, w_uk_ext, w_uv_b, bl, tl_len)

    tm_c = lambda a: a.reshape(tc_len, bc * SSM_WIDTH)
    tm_l = lambda a: a.reshape(tl_len, rl * SSM_WIDTH)
    ntok = bc * tc_len + bl * tl_len
    x1, h2, s1t, s2t = _post((xc, at_c, tm_c(u_c), tm_c(yf_c), tm_c(yb_c), bc, tc_len),
                             (xl, at_l, tm_l(u_lp), tm_l(yf_l), tm_l(yb_l), bl, tl_len),
                             m6, row(ssm_d[l]), w_glu_b, w_out_s, w_out_a, row(norm2[l]), w_query_b, sub_keys_b)

    dense = lambda a: a.reshape(PEER_HEADS, N_KEYS, ntok // LANES, LANES)
    stats, t2 = _route(dense(s1t), dense(s2t))
    stats = stats.reshape(4, PEER_HEADS, 1, ntok)
    t2 = t2.reshape(PEER_HEADS, PEER_TOPK, ntok)
    ctx_peer_tiles = bc * tc_len // PEER_TN
    lat_peer_tiles = tl_len // PEER_TN
    peer_row = lambda t: jnp.where(t < ctx_peer_tiles, 0, 1 + (t - ctx_peer_tiles) // lat_peer_tiles)
    y = _peer(h2, s1t, s2t, t2, stats, u_tab, v_tab, x1, m6, peer_row, row(final_norm))

    y_prompt = y[:bc * tc_len].reshape(bc, tc_len, D_MODEL)
    y_sample = y[bc * tc_len:].reshape(bl, tl_len, D_MODEL)
    new_ckv = ckv_c.reshape(bc, 1, tc_len, KV_RANK)
    new_krope = kr_c[:, QK_NOPE:QK_NOPE + QK_ROPE].reshape(bc, 1, tc_len, QK_ROPE)
    new_ssm = _s5_state_out(st_c)[:, None]
    return (y_prompt, y_sample, new_ckv, new_krope, new_ssm)
```

```python
import functools
import math

import jax
import jax.numpy as jnp
import numpy as np
from jax import lax
from jax.experimental import pallas as pl
from jax.experimental.pallas import tpu as pltpu

F32 = jnp.float32
BF16 = jnp.bfloat16

D_MODEL = 1024
GRID_W = 64
EPS = 1e-6
SSM_WIDTH = 512
SSM_GROUP = 16
SSM_GROUPS = 32
SSM_STATE = 64
MLA_HEADS = 8
QK_NOPE = 64
QK_ROPE = 32
V_DIM = 64
Q_RANK = 384
KV_RANK = 256
ROPE_BASE = 10000.0
N_KEYS = 128
N_EXPERTS = N_KEYS * N_KEYS
PEER_HEADS = 8
PEER_TOPK = 16
KEY_DIM = 128

LANES = 128
HEAD_PAD = 128
TOK_TILE = 256
S5_PAIR = 2
S5_BLOCKS = SSM_GROUPS // S5_PAIR
S5_HALF = SSM_WIDTH // 2
S5_BLOCKS_PER_HALF = S5_BLOCKS // 2
S5_TILE_ROWS = 512
PEER_TN = 512
PEER_ROWS = 8
PEER_TE = PEER_ROWS * N_KEYS
GELU_C0 = math.sqrt(2.0 / math.pi)
GELU_C1 = 0.044715 * GELU_C0
VMEM_LIMIT = 54 * 1024 * 1024


def _cparams(sem):
    return pltpu.CompilerParams(dimension_semantics=sem, vmem_limit_bytes=VMEM_LIMIT)


def _rms(x, g):
    return x * lax.rsqrt(jnp.mean(x * x, axis=-1, keepdims=True) + EPS) * g


def _mod_kernel(c_ref, w_ref, b_ref, o_ref):
    o_ref[...] = jnp.dot(jax.nn.silu(c_ref[...]), w_ref[...], preferred_element_type=F32) + b_ref[...]


def _modulation(cvec, w_mod, b_mod):
    rows, d = cvec.shape
    n = w_mod.shape[1]
    tn = 1536
    return pl.pallas_call(
        _mod_kernel,
        grid=(n // tn,),
        in_specs=[pl.BlockSpec((rows, d), lambda j: (0, 0)),
                  pl.BlockSpec((d, tn), lambda j: (0, j)),
                  pl.BlockSpec((1, tn), lambda j: (0, j))],
        out_specs=pl.BlockSpec((rows, tn), lambda j: (0, j)),
        out_shape=jax.ShapeDtypeStruct((rows, n), F32),
        compiler_params=_cparams(("parallel",)),
        name="modulation",
    )(cvec, w_mod, b_mod)


def _pre_kernel(use_rope, pad_rows, x_ref, m_ref, n1_ref, win_ref, qn_ref, kvn_ref, wuq_ref, *rest):
    if use_rope:
        cos_ref, sin_ref, u_ref, q_ref, ckv_ref, kr_ref = rest
    else:
        u_ref, q_ref, ckv_ref, kr_ref = rest
    m = m_ref[0]
    h = _rms(x_ref[...], n1_ref[...]) * (1.0 + m[1:2]) + m[0:1]
    z = jnp.dot(h.astype(BF16), win_ref[...], preferred_element_type=F32)
    u_ref[:, :SSM_WIDTH] = z[:, :SSM_WIDTH]
    if pad_rows:
        u_ref[:, SSM_WIDTH:] = jnp.zeros((TOK_TILE, SSM_WIDTH), F32)
    o_q, o_kv, o_kr = SSM_WIDTH, SSM_WIDTH + Q_RANK, SSM_WIDTH + Q_RANK + KV_RANK
    cqn = _rms(z[:, o_q:o_kv], qn_ref[...])
    qq = jnp.dot(cqn.astype(BF16), wuq_ref[...], preferred_element_type=F32)
    ckv_ref[...] = _rms(z[:, o_kv:o_kr], kvn_ref[...])
    kr = z[:, o_kr:o_kr + HEAD_PAD]
    scale = (QK_NOPE + QK_ROPE) ** -0.5
    nq = MLA_HEADS * HEAD_PAD
    if use_rope:
        cos, sin = cos_ref[...], sin_ref[...]
        kr = kr * cos + z[:, o_kr + HEAD_PAD:o_kr + 2 * HEAD_PAD] * sin
        for hd in range(MLA_HEADS):
            sl = slice(hd * HEAD_PAD, (hd + 1) * HEAD_PAD)
            qh = qq[:, sl] * cos + qq[:, nq + hd * HEAD_PAD:nq + (hd + 1) * HEAD_PAD] * sin
            q_ref[:, sl] = (qh * scale).astype(BF16)
    else:
        q_ref[...] = (qq[:, :nq] * scale).astype(BF16)
    kr_ref[...] = kr


def _pre(x2d, m6, mrow, norm1, w_in_ext, q_norm, kv_norm, w_uq_ext, batch, seq, rope, pad_rows=False):
    ntok = batch * seq
    u_cols = SSM_WIDTH * (2 if pad_rows else 1)
    n_t = seq // TOK_TILE
    use_rope = rope is not None
    full = lambda a: pl.BlockSpec(a.shape, lambda i: (0,) * a.ndim)
    in_specs = [pl.BlockSpec((TOK_TILE, D_MODEL), lambda i: (i, 0)),
                pl.BlockSpec((1, 6, D_MODEL), lambda i: (mrow(i), 0, 0)),
                full(norm1), full(w_in_ext), full(q_norm), full(kv_norm), full(w_uq_ext)]
    args = [x2d, m6, norm1, w_in_ext, q_norm, kv_norm, w_uq_ext]
    if use_rope:
        in_specs += [pl.BlockSpec((TOK_TILE, HEAD_PAD), lambda i: (i % n_t, 0))] * 2
        args += list(rope)
    out_specs = [pl.BlockSpec((TOK_TILE, u_cols), lambda i: (i % n_t, i // n_t)),
                 pl.BlockSpec((TOK_TILE, MLA_HEADS * HEAD_PAD), lambda i: (i, 0)),
                 pl.BlockSpec((TOK_TILE, KV_RANK), lambda i: (i, 0)),
                 pl.BlockSpec((TOK_TILE, HEAD_PAD), lambda i: (i, 0))]
    out_shape = [jax.ShapeDtypeStruct((seq, batch * u_cols), F32),
                 jax.ShapeDtypeStruct((ntok, MLA_HEADS * HEAD_PAD), BF16),
                 jax.ShapeDtypeStruct((ntok, KV_RANK), F32),
                 jax.ShapeDtypeStruct((ntok, HEAD_PAD), F32)]
    return pl.pallas_call(
        functools.partial(_pre_kernel, use_rope, pad_rows),
        grid=(ntok // TOK_TILE,),
        in_specs=in_specs, out_specs=out_specs, out_shape=out_shape,
        compiler_params=_cparams(("parallel",)),
        name="pre_rope" if use_rope else "pre",
    )(*args)


def _s5_param_kernel(lr_ref, li_ref, ldt_ref, lrx_ref, lix_ref, ldtx_ref, bre_ref, bim_ref, cim_ref,
                     abr_ref, abi_ref, bfr_ref, bfi_ref, ncim_ref):
    def disc(lr, li, ldt):
        dt = jnp.exp(ldt)
        mag = jnp.exp(lr * dt)
        ab_re, ab_im = mag * jnp.cos(li * dt), mag * jnp.sin(li * dt)
        den = lr * lr + li * li
        br, bi = lr / den, -li / den
        ar = ab_re - 1.0
        return ab_re, ab_im, ar * br - ab_im * bi, ar * bi + ab_im * br

    ab_re, ab_im, _, _ = disc(lr_ref[...], li_ref[...], ldt_ref[...])
    abr_ref[...] = ab_re
    abi_ref[...] = ab_im
    _, _, f_re, f_im = disc(lrx_ref[...], lix_ref[...], ldtx_ref[...])
    b_re, b_im = bre_ref[...], bim_ref[...]
    bfr_ref[...] = f_re * b_re - f_im * b_im
    bfi_ref[...] = f_re * b_im + f_im * b_re
    ncim_ref[...] = -cim_ref[...]


def _s5_params(lam_re, lam_im, log_dt, b_re, b_im, c_re, c_im):
    dg = 2 * SSM_GROUPS
    n, p = SSM_STATE, SSM_GROUP
    lr = lam_re.reshape(dg, n)
    li = lam_im.reshape(dg, n)
    ldt = jnp.broadcast_to(log_dt.reshape(dg, 1), (dg, n))
    rep = lambda a: jnp.repeat(a, p, axis=1)
    args = [lr, li, ldt, rep(lr), rep(li), rep(ldt),
            b_re.reshape(dg, n * p), b_im.reshape(dg, n * p), c_im.reshape(dg, p * n)]
    small = jax.ShapeDtypeStruct((dg, n), F32)
    big = jax.ShapeDtypeStruct((dg, n * p), F32)
    ab_re, ab_im, bf_re, bf_im, ncim = pl.pallas_call(
        _s5_param_kernel, out_shape=[small, small, big, big, big], name="s5_params")(*args)

    nb, gp = S5_BLOCKS, S5_PAIR
    eye = jnp.eye(gp, dtype=F32)
    pos = np.arange(nb) % S5_BLOCKS_PER_HALF
    bf = jnp.stack([bf_re, bf_im], 0).reshape(2, 2, nb, gp, n, p)
    bc = jnp.einsum('adbgnp,gh->bdgpahn', bf, eye).reshape(nb, 2, gp * p, 2 * gp * n)
    bw = jnp.zeros((nb, 2, S5_BLOCKS_PER_HALF, gp * p, 2 * gp * n), F32)
    bw = bw.at[np.arange(nb), :, pos].set(bc).reshape(nb, 2, S5_HALF, 2 * gp * n)
    cc = jnp.stack([c_re.reshape(2, nb, gp, p, n), ncim.reshape(2, nb, gp, p, n)], 0)
    cc = jnp.einsum('adbgpn,gh->bdagnhp', cc, eye).reshape(nb, 2, 2 * gp * n, gp * p)
    cw = jnp.zeros((nb, 2, 2 * gp * n, S5_BLOCKS_PER_HALF, gp * p), F32)
    cw = cw.at[np.arange(nb), :, :, pos].set(cc).reshape(nb, 2, 2 * gp * n, S5_HALF)
    a = jnp.stack([ab_re, ab_im], 0).reshape(2, 2, nb, gp * n).transpose(2, 1, 0, 3)
    hb = S5_BLOCKS_PER_HALF
    bw = bw.reshape(2, hb, 2, S5_HALF, 2 * gp * n).transpose(0, 2, 3, 1, 4).reshape(2, 2, S5_HALF, -1)
    cw = cw.reshape(2, hb, 2, 2 * gp * n, S5_HALF).transpose(0, 2, 1, 3, 4).reshape(2, 2, -1, S5_HALF)
    a = a.reshape(2, hb, 2, 2, 1, gp * n).transpose(0, 2, 1, 3, 4, 5)
    return bw.astype(BF16), cw.astype(BF16), a


def _s5_state_in(h):
    bsz = h.shape[0]
    return h.reshape(bsz, 2, 2, 2, S5_BLOCKS_PER_HALF, LANES).transpose(3, 1, 4, 2, 0, 5)


def _s5_state_out(st):
    bsz = st.shape[4]
    return st.transpose(4, 1, 3, 0, 2, 5).reshape(bsz, 2, 2, SSM_GROUPS, SSM_STATE)


def _s5_kernel(rows_per_step, steps, uf_ref, ub_ref, bw_ref, cw_ref, a_ref, h0_ref,
               yf_ref, yb_ref, st_ref, buf, coef, carry):
    r = rows_per_step
    nb = S5_BLOCKS_PER_HALF
    width = 2 * LANES

    @pl.when(pl.program_id(1) == 0)
    def _():
        carry[...] = h0_ref[0]
        coef[...] = jnp.broadcast_to(a_ref[0], coef.shape)

    for d, u_ref in enumerate((uf_ref, ub_ref)):
        buf[d] = jnp.dot(u_ref[...].astype(BF16), bw_ref[0, d], preferred_element_type=F32)

    chains = [(d, b) for d in range(2) for b in range(nb)]
    for g in range(r // 8):
        rows8 = slice(g * 8, (g + 1) * 8)

        def body(k, hs, rows8=rows8):
            out = []
            for (d, b), (h_re, h_im) in zip(chains, hs):
                t = k if d == 0 else steps - 1 - k
                r0 = pl.multiple_of(t * r + g * 8, 8)
                a_re, a_im = coef[d, b, 0], coef[d, b, 1]
                re_l, im_l = slice(b * width, b * width + LANES), slice(b * width + LANES, (b + 1) * width)
                n_re = a_re * h_re - a_im * h_im + buf[d, pl.ds(r0, 8), re_l]
                n_im = a_re * h_im + a_im * h_re + buf[d, pl.ds(r0, 8), im_l]
                buf[d, pl.ds(r0, 8), re_l] = n_re
                buf[d, pl.ds(r0, 8), im_l] = n_im
                out.append((n_re, n_im))
            return tuple(out)

        init = tuple((carry[d, b, 0, rows8], carry[d, b, 1, rows8]) for d, b in chains)
        final = lax.fori_loop(0, steps, body, init)
        for (d, b), (h_re, h_im) in zip(chains, final):
            carry[d, b, 0, rows8] = h_re
            carry[d, b, 1, rows8] = h_im

    for d, y_ref in enumerate((yf_ref, yb_ref)):
        y_ref[...] = jnp.dot(buf[d].astype(BF16), cw_ref[0, d], preferred_element_type=F32)
    st_ref[0] = carry[...]


def _s5_scan(u_tm, bw, cw, a, h0, rows_per_step, seq, steps):
    r = rows_per_step
    n_t = seq // steps
    rows = steps * r
    nb = S5_BLOCKS_PER_HALF
    wide = nb * 2 * LANES
    half4 = lambda h, t: (h, 0, 0, 0)
    half6 = lambda h, t: (h, 0, 0, 0, 0, 0)
    state = (2, nb, 2, r, LANES)
    in_specs = [pl.BlockSpec((rows, S5_HALF), lambda h, t: (t, h)),
                pl.BlockSpec((rows, S5_HALF), lambda h, t: (n_t - 1 - t, h)),
                pl.BlockSpec((1, 2, S5_HALF, wide), half4),
                pl.BlockSpec((1, 2, wide, S5_HALF), half4),
                pl.BlockSpec((1, 2, nb, 2, 1, LANES), half6),
                pl.BlockSpec((1,) + state, half6)]
    out_specs = [pl.BlockSpec((rows, S5_HALF), lambda h, t: (t, h)),
                 pl.BlockSpec((rows, S5_HALF), lambda h, t: (n_t - 1 - t, h)),
                 pl.BlockSpec((1,) + state, half6)]
    out_shape = [jax.ShapeDtypeStruct(u_tm.shape, F32), jax.ShapeDtypeStruct(u_tm.shape, F32),
                 jax.ShapeDtypeStruct((2,) + state, F32)]
    return pl.pallas_call(
        functools.partial(_s5_kernel, r, steps),
        grid=(2, n_t),
        in_specs=in_specs, out_specs=out_specs, out_shape=out_shape,
        scratch_shapes=[pltpu.VMEM((2, rows, wide), F32), pltpu.VMEM((2, nb, 2, 8, LANES), F32),
                        pltpu.VMEM(state, F32)],
        compiler_params=_cparams(("arbitrary", "arbitrary")),
        name="s5_scan",
    )(u_tm, u_tm, bw, cw, a, h0)


def _attn_kernel(q_ref, ckv_ref, kr_ref, wuk_ref, wuv_ref, o_ref, k_s, v_s):
    @pl.when(pl.program_id(1) == 0)
    def _():
        kv = ckv_ref[0].astype(BF16)
        kn = jnp.dot(kv, wuk_ref[...], preferred_element_type=F32)
        kr = kr_ref[0]
        for hd in range(MLA_HEADS):
            k_s[hd] = (kn[:, hd * HEAD_PAD:(hd + 1) * HEAD_PAD] + kr).astype(BF16)
        v_s[...] = jnp.dot(kv, wuv_ref[...], preferred_element_type=F32).astype(BF16)

    for hd in range(MLA_HEADS):
        qh = q_ref[:, hd * HEAD_PAD:(hd + 1) * HEAD_PAD]
        s = lax.dot_general(qh, k_s[hd], (((1,), (1,)), ((), ())), preferred_element_type=F32)
        p = jnp.exp(s - jnp.max(s, axis=-1, keepdims=True))
        l = jnp.sum(p, axis=-1, keepdims=True)
        o = jnp.dot(p.astype(BF16), v_s[:, hd * V_DIM:(hd + 1) * V_DIM], preferred_element_type=F32)
        o_ref[:, hd * V_DIM:(hd + 1) * V_DIM] = (o / l).astype(BF16)


def _attention(q, ckv_all, kr_all, w_uk_ext, w_uv, batch, seq):
    s_len = ckv_all.shape[1]
    n_q = seq // TOK_TILE
    return pl.pallas_call(
        _attn_kernel,
        grid=(batch, n_q),
        in_specs=[pl.BlockSpec((TOK_TILE, MLA_HEADS * HEAD_PAD), lambda b, i: (b * n_q + i, 0)),
                  pl.BlockSpec((1, s_len, KV_RANK), lambda b, i: (b, 0, 0)),
                  pl.BlockSpec((1, s_len, HEAD_PAD), lambda b, i: (b, 0, 0)),
                  pl.BlockSpec(w_uk_ext.shape, lambda b, i: (0, 0)),
                  pl.BlockSpec(w_uv.shape, lambda b, i: (0, 0))],
        out_specs=pl.BlockSpec((TOK_TILE, MLA_HEADS * V_DIM), lambda b, i: (b * n_q + i, 0)),
        out_shape=jax.ShapeDtypeStruct((batch * seq, MLA_HEADS * V_DIM), BF16),
        scratch_shapes=[pltpu.VMEM((MLA_HEADS, s_len, HEAD_PAD), BF16),
                        pltpu.VMEM((s_len, MLA_HEADS * V_DIM), BF16)],
        compiler_params=_cparams(("parallel", "arbitrary")),
        name="attention",
    )(q, ckv_all, kr_all, w_uk_ext, w_uv)


def _post_kernel(n_ctx, xc_ref, atc_ref, uc_ref, yfc_ref, ybc_ref, xl_ref, atl_ref, ul_ref, yfl_ref, ybl_ref,
                 m_ref, d_ref, wglu_ref, wos_ref, woa_ref, n2_ref, wq_ref, sk_ref,
                 x1_ref, h2_ref, s1_ref, s2_ref):
    is_ctx = pl.program_id(0) < n_ctx
    pick = lambda c_ref, l_ref: jnp.where(is_ctx, c_ref[...], l_ref[...])
    m = m_ref[0]
    y = pick(yfc_ref, yfl_ref) + pick(ybc_ref, ybl_ref) + pick(uc_ref, ul_ref) * d_ref[...]
    yg = jax.nn.gelu(y)
    gate = jax.nn.sigmoid(jnp.dot(yg.astype(BF16), wglu_ref[...], preferred_element_type=F32))
    mix = (jnp.dot((yg * gate).astype(BF16), wos_ref[...], preferred_element_type=F32)
           + jnp.dot(pick(atc_ref, atl_ref), woa_ref[...], preferred_element_type=F32))
    x1 = pick(xc_ref, xl_ref) + m[2:3] * mix
    x1_ref[...] = x1
    h2 = (_rms(x1, n2_ref[...]) * (1.0 + m[4:5]) + m[3:4]).astype(BF16)
    h2_ref[...] = h2
    qp = jnp.dot(h2, wq_ref[...], preferred_element_type=F32).astype(BF16)
    for hd in range(PEER_HEADS):
        for half, s_ref in enumerate((s1_ref, s2_ref)):
            c0 = (hd * 2 + half) * KEY_DIM
            s_ref[hd] = lax.dot_general(sk_ref[hd, half], qp[:, c0:c0 + KEY_DIM],
                                        (((1,), (1,)), ((), ())), preferred_element_type=F32)


def _post(ctx, lat, m6, ssm_d, w_glu, w_out_s, w_out_a, norm2, w_query, sub_keys):
    specs, args = [], []
    n_ctx, n_lat = (p[5] * p[6] // TOK_TILE for p in (ctx, lat))
    for pass_id, (x2d, attn, u_tm, yf, yb, batch, seq, stride) in enumerate((ctx, lat)):
        n_t = seq // TOK_TILE
        loc = (lambda i: jnp.minimum(i, n_ctx - 1)) if pass_id == 0 else (lambda i: jnp.maximum(i - n_ctx, 0))
        tok = lambda w, loc=loc: pl.BlockSpec((TOK_TILE, w), lambda i: (loc(i), 0))
        tm = pl.BlockSpec((TOK_TILE, SSM_WIDTH),
                          lambda i, loc=loc, n_t=n_t, stride=stride: (loc(i) % n_t, (loc(i) // n_t) * stride))
        specs += [tok(D_MODEL), tok(MLA_HEADS * V_DIM), tm, tm, tm]
        args += [x2d, attn, u_tm, yf, yb]
    lat_tiles_per_row = lat[6] // TOK_TILE
    mrow = lambda i: jnp.where(i < n_ctx, 0, 1 + (i - n_ctx) // lat_tiles_per_row)
    full = lambda a: pl.BlockSpec(a.shape, lambda i: (0,) * a.ndim)
    weights = [ssm_d, w_glu, w_out_s, w_out_a, norm2, w_query, sub_keys]
    total = (n_ctx + n_lat) * TOK_TILE
    tok_out = lambda w: pl.BlockSpec((TOK_TILE, w), lambda i: (i, 0))
    sc = pl.BlockSpec((PEER_HEADS, N_KEYS, TOK_TILE), lambda i: (0, 0, i))
    return pl.pallas_call(
        functools.partial(_post_kernel, n_ctx),
        grid=(n_ctx + n_lat,),
        in_specs=specs + [pl.BlockSpec((1, 6, D_MODEL), lambda i: (mrow(i), 0, 0))] + [full(w) for w in weights],
        out_specs=[tok_out(D_MODEL), tok_out(D_MODEL), sc, sc],
        out_shape=[jax.ShapeDtypeStruct((total, D_MODEL), F32), jax.ShapeDtypeStruct((total, D_MODEL), BF16),
                   jax.ShapeDtypeStruct((PEER_HEADS, N_KEYS, total), F32),
                   jax.ShapeDtypeStruct((PEER_HEADS, N_KEYS, total), F32)],
        compiler_params=_cparams(("arbitrary",)),
        name="post",
    )(*args, m6, *weights)


def _sort_pairs(lo, hi):
    def merge(lo, hi, r):
        step = r * 2
        if step < hi - lo:
            yield from merge(lo, hi, step)
            yield from merge(lo + r, hi, step)
            for i in range(lo + r, hi - r, step):
                yield (i, i + r)
        else:
            yield (lo, lo + r)

    if hi - lo >= 1:
        mid = lo + (hi - lo) // 2
        yield from _sort_pairs(lo, mid)
        yield from _sort_pairs(mid + 1, hi)
        yield from merge(lo, hi, 1)


def _sort_desc(vals):
    vals = list(vals)
    for i, j in _sort_pairs(0, len(vals) - 1):
        vals[i], vals[j] = jnp.maximum(vals[i], vals[j]), jnp.minimum(vals[i], vals[j])
    return vals


def _merge_top(a, b):
    n = len(a)
    c = [jnp.maximum(a[k], b[n - 1 - k]) for k in range(n)]
    stride = n // 2
    while stride:
        for i in range(n):
            if not i & stride:
                c[i], c[i + stride] = jnp.maximum(c[i], c[i + stride]), jnp.minimum(c[i], c[i + stride])
        stride //= 2
    return c


def _top_keys(ref):
    k = PEER_TOPK
    groups = [_sort_desc([ref[0, g * k + r] for r in range(k)]) for g in range(N_KEYS // k)]
    while len(groups) > 1:
        groups = [_merge_top(groups[i], groups[i + 1]) for i in range(0, len(groups), 2)]
    return groups[0]


def _route_kernel(s1_ref, s2_ref, o_ref, t2_ref):
    k = PEER_TOPK
    t1, t2 = _top_keys(s1_ref), _top_keys(s2_ref)
    cands = [t1[a] + t2[b] for a in range(k) for b in range(k) if (a + 1) * (b + 1) <= k]
    size = 1 << (len(cands) - 1).bit_length()
    cands += [jnp.full_like(t1[0], -jnp.inf)] * (size - len(cands))
    best = _sort_desc(cands)
    z = jnp.ones_like(best[0])
    for v in best[1:k]:
        z = z + jnp.exp(v - best[0])
    stats = (best[k - 1], t1[0], t2[0], 1.0 / z)
    for c in range(8):
        lanes = slice(c * LANES, (c + 1) * LANES)
        for n, v in enumerate(stats):
            o_ref[n, 0, :, lanes] = v[c:c + 1]
        for b in range(k):
            t2_ref[0, b:b + 1, lanes] = t2[b][c:c + 1]


def _route(s1d, s2d):
    n_rows = s1d.shape[2]
    ntok = n_rows * LANES
    blk = 8 * LANES
    spec = pl.BlockSpec((1, N_KEYS, 8, LANES), lambda i, h: (h, 0, i, 0))
    return pl.pallas_call(
        _route_kernel,
        grid=(n_rows // 8, PEER_HEADS),
        in_specs=[spec, spec],
        out_specs=[pl.BlockSpec((4, 1, 1, blk), lambda i, h: (0, h, 0, i)),
                   pl.BlockSpec((1, PEER_TOPK, blk), lambda i, h: (h, 0, i))],
        out_shape=[jax.ShapeDtypeStruct((4, PEER_HEADS, 1, ntok), F32),
                   jax.ShapeDtypeStruct((PEER_HEADS, PEER_TOPK, ntok), F32)],
        compiler_params=_cparams(("parallel", "parallel")),
        name="peer_route",
    )(s1d, s2d)


def _peer_kernel(h2_ref, s1_ref, s2_ref, t2_ref, st_ref, u_ref, v_ref, x1_ref, m_ref, fn_ref, y_ref,
                 e2_s, act_s, aw_s, acc_s):
    k = pl.program_id(1)
    chunks = [slice(c * LANES, (c + 1) * LANES) for c in range(PEER_TN // LANES)]

    @pl.when(k == 0)
    def _():
        for hd in range(PEER_HEADS):
            e2_s[hd] = jnp.exp(s2_ref[hd] - st_ref[2, hd])
        acc_s[...] = jnp.zeros_like(acc_s)

    floor, gain = [], []
    for hd in range(PEER_HEADS):
        s1 = s1_ref[hd]
        low = jnp.full_like(s1, jnp.inf)
        for b in range(PEER_TOPK):
            t2b = t2_ref[hd, b:b + 1, :]
            low = jnp.where(s1 + t2b >= st_ref[0, hd], t2b, low)
        floor.append(low)
        gain.append(jnp.exp(s1 - st_ref[1, hd]) * (0.5 * st_ref[3, hd]))

    act_s[...] = lax.dot_general(u_ref[...].astype(BF16), h2_ref[...], (((1,), (1,)), ((), ())),
                                 preferred_element_type=F32)
    for ii in range(PEER_ROWS):
        rows = slice(ii * N_KEYS, (ii + 1) * N_KEYS)
        for cs in chunks:
            w = None
            for hd in range(PEER_HEADS):
                hit = s2_ref[hd, :, cs] >= floor[hd][ii:ii + 1, cs]
                term = jnp.where(hit, e2_s[hd, :, cs], 0.0) * gain[hd][ii:ii + 1, cs]
                w = term if w is None else w + term
            x = act_s[rows, cs]
            t = jnp.tanh(x * (GELU_C0 + GELU_C1 * (x * x)))
            aw_s[rows, cs] = ((x * w) * (1.0 + t)).astype(BF16)
    acc_s[...] += lax.dot_general(aw_s[...], v_ref[...].astype(BF16), (((0,), (0,)), ((), ())),
                                  preferred_element_type=F32)

    @pl.when(k == pl.num_programs(1) - 1)
    def _():
        y_ref[...] = _rms(x1_ref[...] + m_ref[0][5:6] * acc_s[...], fn_ref[...])


def _peer(h2, s1t, s2t, t2, stats, u_tab, v_tab, x1, m6, mrow, final_norm):
    ntok = h2.shape[0]
    tn, te = PEER_TN, PEER_TE
    tok = pl.BlockSpec((tn, D_MODEL), lambda t, k: (t, 0))
    tab = pl.BlockSpec((te, D_MODEL), lambda t, k: (k, 0))
    keys = pltpu.VMEM((PEER_HEADS, N_KEYS, tn), F32)
    return pl.pallas_call(
        _peer_kernel,
        grid=(ntok // tn, N_EXPERTS // te),
        in_specs=[tok,
                  pl.BlockSpec((PEER_HEADS, PEER_ROWS, tn), lambda t, k: (0, k, t)),
                  pl.BlockSpec((PEER_HEADS, N_KEYS, tn), lambda t, k: (0, 0, t)),
                  pl.BlockSpec((PEER_HEADS, PEER_TOPK, tn), lambda t, k: (0, 0, t)),
                  pl.BlockSpec((4, PEER_HEADS, 1, tn), lambda t, k: (0, 0, 0, t)),
                  tab, tab, tok,
                  pl.BlockSpec((1, 6, D_MODEL), lambda t, k: (mrow(t), 0, 0)),
                  pl.BlockSpec((1, D_MODEL), lambda t, k: (0, 0))],
        out_specs=tok,
        out_shape=jax.ShapeDtypeStruct((ntok, D_MODEL), F32),
        scratch_shapes=[keys, pltpu.VMEM((te, tn), F32), pltpu.VMEM((te, tn), BF16),
                        pltpu.VMEM((tn, D_MODEL), F32)],
        compiler_params=_cparams(("parallel", "arbitrary")),
        name="peer_dense",
    )(h2, s1t, s2t, t2, stats, u_tab, v_tab, x1, m6, final_norm)


def _rot_cols(w):
    j = np.arange(QK_ROPE)
    first = (j % (QK_ROPE // 2)) < (QK_ROPE // 4)
    perm = np.where(first, j + QK_ROPE // 4, j - QK_ROPE // 4)
    sign = np.where(first, -1.0, 1.0).astype(np.float32)
    return w[..., perm] * sign


def _rope_slot(w_rope):
    pad = [(0, 0)] * (w_rope.ndim - 1)
    return jnp.pad(w_rope, pad + [(QK_NOPE, HEAD_PAD - QK_NOPE - QK_ROPE)])


def _rope_tables(n_tokens):
    n_rows = n_tokens // GRID_W
    rows = jnp.repeat(jnp.arange(n_rows, dtype=F32), GRID_W)
    cols = jnp.tile(jnp.arange(GRID_W, dtype=F32), n_rows)
    half = QK_ROPE // 2
    inv_freq = 1.0 / (ROPE_BASE ** (jnp.arange(0, half, 2, dtype=F32) / half))
    ang_r = rows[:, None] * inv_freq
    ang_c = cols[:, None] * inv_freq
    ang = jnp.concatenate([ang_r, ang_r, ang_c, ang_c], axis=-1)
    lead = (QK_NOPE, HEAD_PAD - QK_NOPE - QK_ROPE)
    cos = jnp.pad(jnp.cos(ang), [(0, 0), lead], constant_values=1.0)
    sin = jnp.pad(jnp.sin(ang), [(0, 0), lead])
    return cos, sin


def kernel(x_prompt, x_sample, c, cache_ckv, cache_krope, state_ssm, c_ctx, w_mod, b_mod, norm1, w_in,
           ssm_lam_re, ssm_lam_im, ssm_log_dt, ssm_b_re, ssm_b_im, ssm_c_re, ssm_c_im, ssm_d, w_glu,
           q_norm, w_uq, kv_norm, w_uk, w_uv, w_out, norm2, w_query, sub_keys, u_table, v_table,
           final_norm):
    bc, tc_len, _ = x_prompt.shape
    bl, tl_len, _ = x_sample.shape
    l = 0
    row = lambda a: a.reshape(1, -1)
    assert w_mod.shape[0] == 1
    big = lambda a: a.reshape(a.shape[1:])

    n_mod = 8
    cvec = jnp.concatenate([c_ctx[None], c, jnp.zeros((n_mod - 1 - bl, D_MODEL), F32)], 0)
    m6 = _modulation(cvec, big(w_mod), row(b_mod[l])).reshape(n_mod, 6, D_MODEL)

    wi = w_in[l]
    o_kr = SSM_WIDTH + Q_RANK + KV_RANK
    w_kr = wi[:, o_kr:]
    w_in_ext = jnp.concatenate([wi[:, :o_kr], _rope_slot(w_kr), _rope_slot(_rot_cols(w_kr))], 1).astype(BF16)
    wq3 = w_uq[l].reshape(Q_RANK, MLA_HEADS, QK_NOPE + QK_ROPE)
    wq_main = jnp.pad(wq3, [(0, 0), (0, 0), (0, HEAD_PAD - QK_NOPE - QK_ROPE)])
    wq_rot = _rope_slot(_rot_cols(wq3[..., QK_NOPE:]))
    w_uq_ext = jnp.concatenate([wq_main.reshape(Q_RANK, -1), wq_rot.reshape(Q_RANK, -1)], 1).astype(BF16)
    w_uk_ext = jnp.pad(w_uk[l].reshape(KV_RANK, MLA_HEADS, QK_NOPE),
                       [(0, 0), (0, 0), (0, HEAD_PAD - QK_NOPE)]).reshape(KV_RANK, -1).astype(BF16)
    w_uv_b = w_uv[l].astype(BF16)
    w_glu_b = w_glu[l].astype(BF16)
    w_out_s = w_out[l][:SSM_WIDTH].astype(BF16)
    w_out_a = w_out[l][SSM_WIDTH:].astype(BF16)
    w_query_b = w_query[l].astype(BF16)
    sub_keys_b = sub_keys[l].astype(BF16)
    u_tab, v_tab = big(u_table), big(v_table)
    cos, sin = _rope_tables(tl_len)

    bw, cw, a_blk = _s5_params(ssm_lam_re[l], ssm_lam_im[l], ssm_log_dt[l], ssm_b_re[l], ssm_b_im[l],
                               ssm_c_re[l], ssm_c_im[l])

    ctx_row = lambda i: 0
    lat_tiles = tl_len // TOK_TILE
    lat_row = lambda i: 1 + i // lat_tiles
    xc = x_prompt.reshape(bc * tc_len, D_MODEL)
    xl = x_sample.reshape(bl * tl_len, D_MODEL)

    u_c, q_c, ckv_c, kr_c = _pre(xc, m6, ctx_row, row(norm1[l]), w_in_ext, row(q_norm[l]), row(kv_norm[l]),
                                 w_uq_ext[:, :MLA_HEADS * HEAD_PAD], bc, tc_len, None)
    u_l, q_l, ckv_l, kr_l = _pre(xl, m6, lat_row, row(norm1[l]), w_in_ext, row(q_norm[l]), row(kv_norm[l]),
                                 w_uq_ext, bl, tl_len, (cos, sin), pad_rows=True)

    rl = 2 * bl
    h0_c = jnp.zeros((2, 2, S5_BLOCKS_PER_HALF, 2, bc, LANES), F32)
    yf_c, yb_c, st_c = _s5_scan(u_c.reshape(tc_len * bc, SSM_WIDTH), bw, cw, a_blk, h0_c, bc, tc_len,
                                S5_TILE_ROWS // bc)
    h0_l = _s5_state_in(state_ssm[:, l])
    h0_l = jnp.stack([h0_l, jnp.zeros_like(h0_l)], axis=-2).reshape(h0_l.shape[:4] + (rl, LANES))
    yf_l, yb_l, _ = _s5_scan(u_l.reshape(tl_len * rl, SSM_WIDTH), bw, cw, a_blk, h0_l, rl, tl_len,
                             S5_TILE_ROWS // rl)

    at_c = _attention(q_c, ckv_c.reshape(bc, tc_len, KV_RANK), kr_c.reshape(bc, tc_len, HEAD_PAD),
                      w_uk_ext, w_uv_b, bc, tc_len)
    ckv_all = jnp.concatenate([cache_ckv[:, l], ckv_l.reshape(bl, tl_len, KV_RANK)], 1)
    kr_all = jnp.concatenate([_rope_slot(cache_krope[:, l]), kr_l.reshape(bl, tl_len, HEAD_PAD)], 1)
    at_l = _attention(q_l, ckv_all, kr_all, w_uk_ext, w_uv_b, bl, tl_len)

    tm_c = lambda a: a.reshape(tc_len, bc * SSM_WIDTH)
    tm_l = lambda a: a.reshape(tl_len, rl * SSM_WIDTH)
    ntok = bc * tc_len + bl * tl_len
    x1, h2, s1t, s2t = _post((xc, at_c, tm_c(u_c), tm_c(yf_c), tm_c(yb_c), bc, tc_len, 1),
                             (xl, at_l, u_l, tm_l(yf_l), tm_l(yb_l), bl, tl_len, 2),
                             m6, row(ssm_d[l]), w_glu_b, w_out_s, w_out_a, row(norm2[l]), w_query_b, sub_keys_b)

    dense = lambda a: a.reshape(PEER_HEADS, N_KEYS, ntok // LANES, LANES)
    stats, t2 = _route(dense(s1t), dense(s2t))
    ctx_peer_tiles = bc * tc_len // PEER_TN
    lat_peer_tiles = tl_len // PEER_TN
    peer_row = lambda t: jnp.where(t < ctx_peer_tiles, 0, 1 + (t - ctx_peer_tiles) // lat_peer_tiles)
    y = _peer(h2, s1t, s2t, t2, stats, u_tab, v_tab, x1, m6, peer_row, row(final_norm))

    y_prompt = y[:bc * tc_len].reshape(bc, tc_len, D_MODEL)
    y_sample = y[bc * tc_len:].reshape(bl, tl_len, D_MODEL)
    new_ckv = ckv_c.reshape(bc, 1, tc_len, KV_RANK)
    new_krope = kr_c[:, QK_NOPE:QK_NOPE + QK_ROPE].reshape(bc, 1, tc_len, QK_ROPE)
    new_ssm = _s5_state_out(st_c)[:, None]
    return (y_prompt, y_sample, new_ckv, new_krope, new_ssm)
```

```python
import functools
import math

import jax
import jax.numpy as jnp
import numpy as np
from jax import lax
from jax.experimental import pallas as pl
from jax.experimental.pallas import tpu as pltpu

F32 = jnp.float32
BF16 = jnp.bfloat16

D_MODEL = 1024
GRID_W = 64
EPS = 1e-6
SSM_WIDTH = 512
SSM_GROUP = 16
SSM_GROUPS = 32
SSM_STATE = 64
MLA_HEADS = 8
QK_NOPE = 64
QK_ROPE = 32
V_DIM = 64
Q_RANK = 384
KV_RANK = 256
ROPE_BASE = 10000.0
N_KEYS = 128
N_EXPERTS = N_KEYS * N_KEYS
PEER_HEADS = 8
PEER_TOPK = 16
KEY_DIM = 128

LANES = 128
HEAD_PAD = 128
TOK_TILE = 256
S5_PAIR = 2
S5_BLOCKS = SSM_GROUPS // S5_PAIR
S5_HALF = SSM_WIDTH // 2
S5_BLOCKS_PER_HALF = S5_BLOCKS // 2
S5_TILE_ROWS = 512
PEER_TN = 512
PEER_ROWS = 8
PEER_TE = PEER_ROWS * N_KEYS
GELU_C0 = math.sqrt(2.0 / math.pi)
GELU_C1 = 0.044715 * GELU_C0
VMEM_LIMIT = 54 * 1024 * 1024


def _cparams(sem):
    return pltpu.CompilerParams(dimension_semantics=sem, vmem_limit_bytes=VMEM_LIMIT)


def _rms(x, g):
    return x * lax.rsqrt(jnp.mean(x * x, axis=-1, keepdims=True) + EPS) * g


def _mod_kernel(c_ref, w_ref, b_ref, o_ref):
    o_ref[...] = jnp.dot(jax.nn.silu(c_ref[...]), w_ref[...], preferred_element_type=F32) + b_ref[...]


def _modulation(cvec, w_mod, b_mod):
    rows, d = cvec.shape
    n = w_mod.shape[1]
    tn = 1536
    return pl.pallas_call(
        _mod_kernel,
        grid=(n // tn,),
        in_specs=[pl.BlockSpec((rows, d), lambda j: (0, 0)),
                  pl.BlockSpec((d, tn), lambda j: (0, j)),
                  pl.BlockSpec((1, tn), lambda j: (0, j))],
        out_specs=pl.BlockSpec((rows, tn), lambda j: (0, j)),
        out_shape=jax.ShapeDtypeStruct((rows, n), F32),
        compiler_params=_cparams(("parallel",)),
        name="modulation",
    )(cvec, w_mod, b_mod)


def _pre_kernel(use_rope, pad_rows, x_ref, m_ref, n1_ref, win_ref, qn_ref, kvn_ref, wuq_ref, *rest):
    if use_rope:
        cos_ref, sin_ref, u_ref, q_ref, ckv_ref, kr_ref = rest
    else:
        u_ref, q_ref, ckv_ref, kr_ref = rest
    m = m_ref[0]
    h = _rms(x_ref[...], n1_ref[...]) * (1.0 + m[1:2]) + m[0:1]
    z = jnp.dot(h.astype(BF16), win_ref[...], preferred_element_type=F32)
    u_ref[:, :SSM_WIDTH] = z[:, :SSM_WIDTH]
    if pad_rows:
        u_ref[:, SSM_WIDTH:] = jnp.zeros((TOK_TILE, SSM_WIDTH), F32)
    o_q, o_kv, o_kr = SSM_WIDTH, SSM_WIDTH + Q_RANK, SSM_WIDTH + Q_RANK + KV_RANK
    cqn = _rms(z[:, o_q:o_kv], qn_ref[...])
    qq = jnp.dot(cqn.astype(BF16), wuq_ref[...], preferred_element_type=F32)
    ckv_ref[...] = _rms(z[:, o_kv:o_kr], kvn_ref[...])
    kr = z[:, o_kr:o_kr + HEAD_PAD]
    scale = (QK_NOPE + QK_ROPE) ** -0.5
    nq = MLA_HEADS * HEAD_PAD
    if use_rope:
        cos, sin = cos_ref[...], sin_ref[...]
        kr = kr * cos + z[:, o_kr + HEAD_PAD:o_kr + 2 * HEAD_PAD] * sin
        for hd in range(MLA_HEADS):
            sl = slice(hd * HEAD_PAD, (hd + 1) * HEAD_PAD)
            qh = qq[:, sl] * cos + qq[:, nq + hd * HEAD_PAD:nq + (hd + 1) * HEAD_PAD] * sin
            q_ref[:, sl] = (qh * scale).astype(BF16)
    else:
        q_ref[...] = (qq[:, :nq] * scale).astype(BF16)
    kr_ref[...] = kr


def _pre(x2d, m6, mrow, norm1, w_in_ext, q_norm, kv_norm, w_uq_ext, batch, seq, rope, pad_rows=False):
    ntok = batch * seq
    u_cols = SSM_WIDTH * (2 if pad_rows else 1)
    n_t = seq // TOK_TILE
    use_rope = rope is not None
    full = lambda a: pl.BlockSpec(a.shape, lambda i: (0,) * a.ndim)
    in_specs = [pl.BlockSpec((TOK_TILE, D_MODEL), lambda i: (i, 0)),
                pl.BlockSpec((1, 6, D_MODEL), lambda i: (mrow(i), 0, 0)),
                full(norm1), full(w_in_ext), full(q_norm), full(kv_norm), full(w_uq_ext)]
    args = [x2d, m6, norm1, w_in_ext, q_norm, kv_norm, w_uq_ext]
    if use_rope:
        in_specs += [pl.BlockSpec((TOK_TILE, HEAD_PAD), lambda i: (i % n_t, 0))] * 2
        args += list(rope)
    out_specs = [pl.BlockSpec((TOK_TILE, u_cols), lambda i: (i % n_t, i // n_t)),
                 pl.BlockSpec((TOK_TILE, MLA_HEADS * HEAD_PAD), lambda i: (i, 0)),
                 pl.BlockSpec((TOK_TILE, KV_RANK), lambda i: (i, 0)),
                 pl.BlockSpec((TOK_TILE, HEAD_PAD), lambda i: (i, 0))]
    out_shape = [jax.ShapeDtypeStruct((seq, batch * u_cols), F32),
                 jax.ShapeDtypeStruct((ntok, MLA_HEADS * HEAD_PAD), BF16),
                 jax.ShapeDtypeStruct((ntok, KV_RANK), F32),
                 jax.ShapeDtypeStruct((ntok, HEAD_PAD), F32)]
    return pl.pallas_call(
        functools.partial(_pre_kernel, use_rope, pad_rows),
        grid=(ntok // TOK_TILE,),
        in_specs=in_specs, out_specs=out_specs, out_shape=out_shape,
        compiler_params=_cparams(("parallel",)),
        name="pre_rope" if use_rope else "pre",
    )(*args)


def _s5_param_kernel(lr_ref, li_ref, ldt_ref, lrx_ref, lix_ref, ldtx_ref, bre_ref, bim_ref, cim_ref,
                     abr_ref, abi_ref, bfr_ref, bfi_ref, ncim_ref):
    def disc(lr, li, ldt):
        dt = jnp.exp(ldt)
        mag = jnp.exp(lr * dt)
        ab_re, ab_im = mag * jnp.cos(li * dt), mag * jnp.sin(li * dt)
        den = lr * lr + li * li
        br, bi = lr / den, -li / den
        ar = ab_re - 1.0
        return ab_re, ab_im, ar * br - ab_im * bi, ar * bi + ab_im * br

    ab_re, ab_im, _, _ = disc(lr_ref[...], li_ref[...], ldt_ref[...])
    abr_ref[...] = ab_re
    abi_ref[...] = ab_im
    _, _, f_re, f_im = disc(lrx_ref[...], lix_ref[...], ldtx_ref[...])
    b_re, b_im = bre_ref[...], bim_ref[...]
    bfr_ref[...] = f_re * b_re - f_im * b_im
    bfi_ref[...] = f_re * b_im + f_im * b_re
    ncim_ref[...] = -cim_ref[...]


def _s5_params(lam_re, lam_im, log_dt, b_re, b_im, c_re, c_im):
    dg = 2 * SSM_GROUPS
    n, p = SSM_STATE, SSM_GROUP
    lr = lam_re.reshape(dg, n)
    li = lam_im.reshape(dg, n)
    ldt = jnp.broadcast_to(log_dt.reshape(dg, 1), (dg, n))
    rep = lambda a: jnp.repeat(a, p, axis=1)
    args = [lr, li, ldt, rep(lr), rep(li), rep(ldt),
            b_re.reshape(dg, n * p), b_im.reshape(dg, n * p), c_im.reshape(dg, p * n)]
    small = jax.ShapeDtypeStruct((dg, n), F32)
    big = jax.ShapeDtypeStruct((dg, n * p), F32)
    ab_re, ab_im, bf_re, bf_im, ncim = pl.pallas_call(
        _s5_param_kernel, out_shape=[small, small, big, big, big], name="s5_params")(*args)

    nb, gp = S5_BLOCKS, S5_PAIR
    eye = jnp.eye(gp, dtype=F32)
    pos = np.arange(nb) % S5_BLOCKS_PER_HALF
    bf = jnp.stack([bf_re, bf_im], 0).reshape(2, 2, nb, gp, n, p)
    bc = jnp.einsum('adbgnp,gh->bdgpahn', bf, eye).reshape(nb, 2, gp * p, 2 * gp * n)
    bw = jnp.zeros((nb, 2, S5_BLOCKS_PER_HALF, gp * p, 2 * gp * n), F32)
    bw = bw.at[np.arange(nb), :, pos].set(bc).reshape(nb, 2, S5_HALF, 2 * gp * n)
    cc = jnp.stack([c_re.reshape(2, nb, gp, p, n), ncim.reshape(2, nb, gp, p, n)], 0)
    cc = jnp.einsum('adbgpn,gh->bdagnhp', cc, eye).reshape(nb, 2, 2 * gp * n, gp * p)
    cw = jnp.zeros((nb, 2, 2 * gp * n, S5_BLOCKS_PER_HALF, gp * p), F32)
    cw = cw.at[np.arange(nb), :, :, pos].set(cc).reshape(nb, 2, 2 * gp * n, S5_HALF)
    a = jnp.stack([ab_re, ab_im], 0).reshape(2, 2, nb, gp * n).transpose(2, 1, 0, 3)
    hb = S5_BLOCKS_PER_HALF
    bw = bw.reshape(2, hb, 2, S5_HALF, 2 * gp * n).transpose(0, 2, 3, 1, 4).reshape(2, 2, S5_HALF, -1)
    cw = cw.reshape(2, hb, 2, 2 * gp * n, S5_HALF).transpose(0, 2, 1, 3, 4).reshape(2, 2, -1, S5_HALF)
    a = a.reshape(2, hb, 2, 2, 1, gp * n).transpose(0, 2, 1, 3, 4, 5)
    return bw.astype(BF16), cw.astype(BF16), a


def _s5_state_in(h):
    bsz = h.shape[0]
    return h.reshape(bsz, 2, 2, 2, S5_BLOCKS_PER_HALF, LANES).transpose(3, 1, 4, 2, 0, 5)


def _s5_state_out(st):
    bsz = st.shape[4]
    return st.transpose(4, 1, 3, 0, 2, 5).reshape(bsz, 2, 2, SSM_GROUPS, SSM_STATE)


def _s5_kernel(rows_per_step, steps, uf_ref, ub_ref, bw_ref, cw_ref, a_ref, h0_ref,
               yf_ref, yb_ref, st_ref, buf, coef, carry):
    r = rows_per_step
    nb = S5_BLOCKS_PER_HALF
    width = 2 * LANES

    @pl.when(pl.program_id(1) == 0)
    def _():
        carry[...] = h0_ref[0]
        coef[...] = jnp.broadcast_to(a_ref[0], coef.shape)

    for d, u_ref in enumerate((uf_ref, ub_ref)):
        buf[d] = jnp.dot(u_ref[...].astype(BF16), bw_ref[0, d], preferred_element_type=F32)

    chains = [(d, b) for d in range(2) for b in range(nb)]
    for g in range(r // 8):
        rows8 = slice(g * 8, (g + 1) * 8)

        def body(k, hs, rows8=rows8):
            out = []
            for (d, b), (h_re, h_im) in zip(chains, hs):
                t = k if d == 0 else steps - 1 - k
                r0 = pl.multiple_of(t * r + g * 8, 8)
                a_re, a_im = coef[d, b, 0], coef[d, b, 1]
                re_l, im_l = slice(b * width, b * width + LANES), slice(b * width + LANES, (b + 1) * width)
                n_re = a_re * h_re - a_im * h_im + buf[d, pl.ds(r0, 8), re_l]
                n_im = a_re * h_im + a_im * h_re + buf[d, pl.ds(r0, 8), im_l]
                buf[d, pl.ds(r0, 8), re_l] = n_re
                buf[d, pl.ds(r0, 8), im_l] = n_im
                out.append((n_re, n_im))
            return tuple(out)

        init = tuple((carry[d, b, 0, rows8], carry[d, b, 1, rows8]) for d, b in chains)
        final = lax.fori_loop(0, steps, body, init)
        for (d, b), (h_re, h_im) in zip(chains, final):
            carry[d, b, 0, rows8] = h_re
            carry[d, b, 1, rows8] = h_im

    for d, y_ref in enumerate((yf_ref, yb_ref)):
        y_ref[...] = jnp.dot(buf[d].astype(BF16), cw_ref[0, d], preferred_element_type=F32)
    st_ref[0] = carry[...]


def _s5_scan(u_tm, bw, cw, a, h0, rows_per_step, seq, steps):
    r = rows_per_step
    n_t = seq // steps
    rows = steps * r
    nb = S5_BLOCKS_PER_HALF
    wide = nb * 2 * LANES
    half4 = lambda h, t: (h, 0, 0, 0)
    half6 = lambda h, t: (h, 0, 0, 0, 0, 0)
    state = (2, nb, 2, r, LANES)
    in_specs = [pl.BlockSpec((rows, S5_HALF), lambda h, t: (t, h)),
                pl.BlockSpec((rows, S5_HALF), lambda h, t: (n_t - 1 - t, h)),
                pl.BlockSpec((1, 2, S5_HALF, wide), half4),
                pl.BlockSpec((1, 2, wide, S5_HALF), half4),
                pl.BlockSpec((1, 2, nb, 2, 1, LANES), half6),
                pl.BlockSpec((1,) + state, half6)]
    out_specs = [pl.BlockSpec((rows, S5_HALF), lambda h, t: (t, h)),
                 pl.BlockSpec((rows, S5_HALF), lambda h, t: (n_t - 1 - t, h)),
                 pl.BlockSpec((1,) + state, half6)]
    out_shape = [jax.ShapeDtypeStruct(u_tm.shape, F32), jax.ShapeDtypeStruct(u_tm.shape, F32),
                 jax.ShapeDtypeStruct((2,) + state, F32)]
    return pl.pallas_call(
        functools.partial(_s5_kernel, r, steps),
        grid=(2, n_t),
        in_specs=in_specs, out_specs=out_specs, out_shape=out_shape,
        scratch_shapes=[pltpu.VMEM((2, rows, wide), F32), pltpu.VMEM((2, nb, 2, 8, LANES), F32),
                        pltpu.VMEM(state, F32)],
        compiler_params=_cparams(("arbitrary", "arbitrary")),
        name="s5_scan",
    )(u_tm, u_tm, bw, cw, a, h0)


def _attn_kernel(q_ref, ckv_ref, kr_ref, wuk_ref, wuv_ref, o_ref, k_s, v_s):
    @pl.when(pl.program_id(1) == 0)
    def _():
        kv = ckv_ref[0].astype(BF16)
        kn = jnp.dot(kv, wuk_ref[...], preferred_element_type=F32)
        kr = kr_ref[0]
        for hd in range(MLA_HEADS):
            k_s[hd] = (kn[:, hd * HEAD_PAD:(hd + 1) * HEAD_PAD] + kr).astype(BF16)
        v_s[...] = jnp.dot(kv, wuv_ref[...], preferred_element_type=F32).astype(BF16)

    for hd in range(MLA_HEADS):
        qh = q_ref[:, hd * HEAD_PAD:(hd + 1) * HEAD_PAD]
        s = lax.dot_general(qh, k_s[hd], (((1,), (1,)), ((), ())), preferred_element_type=F32)
        p = jnp.exp(s - jnp.max(s, axis=-1, keepdims=True))
        l = jnp.sum(p, axis=-1, keepdims=True)
        o = jnp.dot(p.astype(BF16), v_s[:, hd * V_DIM:(hd + 1) * V_DIM], preferred_element_type=F32)
        o_ref[:, hd * V_DIM:(hd + 1) * V_DIM] = (o / l).astype(BF16)


def _attention(q, ckv_all, kr_all, w_uk_ext, w_uv, batch, seq):
    s_len = ckv_all.shape[1]
    n_q = seq // TOK_TILE
    return pl.pallas_call(
        _attn_kernel,
        grid=(batch, n_q),
        in_specs=[pl.BlockSpec((TOK_TILE, MLA_HEADS * HEAD_PAD), lambda b, i: (b * n_q + i, 0)),
                  pl.BlockSpec((1, s_len, KV_RANK), lambda b, i: (b, 0, 0)),
                  pl.BlockSpec((1, s_len, HEAD_PAD), lambda b, i: (b, 0, 0)),
                  pl.BlockSpec(w_uk_ext.shape, lambda b, i: (0, 0)),
                  pl.BlockSpec(w_uv.shape, lambda b, i: (0, 0))],
        out_specs=pl.BlockSpec((TOK_TILE, MLA_HEADS * V_DIM), lambda b, i: (b * n_q + i, 0)),
        out_shape=jax.ShapeDtypeStruct((batch * seq, MLA_HEADS * V_DIM), BF16),
        scratch_shapes=[pltpu.VMEM((MLA_HEADS, s_len, HEAD_PAD), BF16),
                        pltpu.VMEM((s_len, MLA_HEADS * V_DIM), BF16)],
        compiler_params=_cparams(("parallel", "arbitrary")),
        name="attention",
    )(q, ckv_all, kr_all, w_uk_ext, w_uv)


def _post_kernel(n_ctx, xc_ref, atc_ref, uc_ref, yfc_ref, ybc_ref, xl_ref, atl_ref, ul_ref, yfl_ref, ybl_ref,
                 m_ref, d_ref, wglu_ref, wos_ref, woa_ref, n2_ref, wq_ref, sk_ref,
                 x1_ref, h2_ref, s1_ref, s2_ref):
    is_ctx = pl.program_id(0) < n_ctx
    pick = lambda c_ref, l_ref: jnp.where(is_ctx, c_ref[...], l_ref[...])
    m = m_ref[0]
    y = pick(yfc_ref, yfl_ref) + pick(ybc_ref, ybl_ref) + pick(uc_ref, ul_ref) * d_ref[...]
    yg = jax.nn.gelu(y)
    gate = jax.nn.sigmoid(jnp.dot(yg.astype(BF16), wglu_ref[...], preferred_element_type=F32))
    mix = (jnp.dot((yg * gate).astype(BF16), wos_ref[...], preferred_element_type=F32)
           + jnp.dot(pick(atc_ref, atl_ref), woa_ref[...], preferred_element_type=F32))
    x1 = pick(xc_ref, xl_ref) + m[2:3] * mix
    x1_ref[...] = x1
    h2 = (_rms(x1, n2_ref[...]) * (1.0 + m[4:5]) + m[3:4]).astype(BF16)
    h2_ref[...] = h2
    qp = jnp.dot(h2, wq_ref[...], preferred_element_type=F32).astype(BF16)
    for hd in range(PEER_HEADS):
        for half, s_ref in enumerate((s1_ref, s2_ref)):
            c0 = (hd * 2 + half) * KEY_DIM
            s_ref[hd] = lax.dot_general(sk_ref[hd, half], qp[:, c0:c0 + KEY_DIM],
                                        (((1,), (1,)), ((), ())), preferred_element_type=F32)


def _post(ctx, lat, m6, ssm_d, w_glu, w_out_s, w_out_a, norm2, w_query, sub_keys):
    specs, args = [], []
    n_ctx, n_lat = (p[5] * p[6] // TOK_TILE for p in (ctx, lat))
    for pass_id, (x2d, attn, u_tm, yf, yb, batch, seq, stride) in enumerate((ctx, lat)):
        n_t = seq // TOK_TILE
        loc = (lambda i: jnp.minimum(i, n_ctx - 1)) if pass_id == 0 else (lambda i: jnp.maximum(i - n_ctx, 0))
        tok = lambda w, loc=loc: pl.BlockSpec((TOK_TILE, w), lambda i: (loc(i), 0))
        tm = pl.BlockSpec((TOK_TILE, SSM_WIDTH),
                          lambda i, loc=loc, n_t=n_t, stride=stride: (loc(i) % n_t, (loc(i) // n_t) * stride))
        specs += [tok(D_MODEL), tok(MLA_HEADS * V_DIM), tm, tm, tm]
        args += [x2d, attn, u_tm, yf, yb]
    lat_tiles_per_row = lat[6] // TOK_TILE
    mrow = lambda i: jnp.where(i < n_ctx, 0, 1 + (i - n_ctx) // lat_tiles_per_row)
    full = lambda a: pl.BlockSpec(a.shape, lambda i: (0,) * a.ndim)
    weights = [ssm_d, w_glu, w_out_s, w_out_a, norm2, w_query, sub_keys]
    total = (n_ctx + n_lat) * TOK_TILE
    tok_out = lambda w: pl.BlockSpec((TOK_TILE, w), lambda i: (i, 0))
    sc = pl.BlockSpec((PEER_HEADS, N_KEYS, TOK_TILE), lambda i: (0, 0, i))
    return pl.pallas_call(
        functools.partial(_post_kernel, n_ctx),
        grid=(n_ctx + n_lat,),
        in_specs=specs + [pl.BlockSpec((1, 6, D_MODEL), lambda i: (mrow(i), 0, 0))] + [full(w) for w in weights],
        out_specs=[tok_out(D_MODEL), tok_out(D_MODEL), sc, sc],
        out_shape=[jax.ShapeDtypeStruct((total, D_MODEL), F32), jax.ShapeDtypeStruct((total, D_MODEL), BF16),
                   jax.ShapeDtypeStruct((PEER_HEADS, N_KEYS, total), F32),
                   jax.ShapeDtypeStruct((PEER_HEADS, N_KEYS, total), F32)],
        compiler_params=_cparams(("arbitrary",)),
        name="post",
    )(*args, m6, *weights)


def _sort_pairs(lo, hi):
    def merge(lo, hi, r):
        step = r * 2
        if step < hi - lo:
            yield from merge(lo, hi, step)
            yield from merge(lo + r, hi, step)
            for i in range(lo + r, hi - r, step):
                yield (i, i + r)
        else:
            yield (lo, lo + r)

    if hi - lo >= 1:
        mid = lo + (hi - lo) // 2
        yield from _sort_pairs(lo, mid)
        yield from _sort_pairs(mid + 1, hi)
        yield from merge(lo, hi, 1)


def _sort_desc(vals):
    vals = list(vals)
    for i, j in _sort_pairs(0, len(vals) - 1):
        vals[i], vals[j] = jnp.maximum(vals[i], vals[j]), jnp.minimum(vals[i], vals[j])
    return vals


def _merge_top(a, b):
    n = len(a)
    c = [jnp.maximum(a[k], b[n - 1 - k]) for k in range(n)]
    stride = n // 2
    while stride:
        for i in range(n):
            if not i & stride:
                c[i], c[i + stride] = jnp.maximum(c[i], c[i + stride]), jnp.minimum(c[i], c[i + stride])
        stride //= 2
    return c


def _top_keys(ref):
    k = PEER_TOPK
    groups = [_sort_desc([ref[0, g * k + r] for r in range(k)]) for g in range(N_KEYS // k)]
    while len(groups) > 1:
        groups = [_merge_top(groups[i], groups[i + 1]) for i in range(0, len(groups), 2)]
    return groups[0]


def _route_kernel(s1_ref, s2_ref, o_ref, t2_ref):
    k = PEER_TOPK
    t1, t2 = _top_keys(s1_ref), _top_keys(s2_ref)
    cands = [t1[a] + t2[b] for a in range(k) for b in range(k) if (a + 1) * (b + 1) <= k]
    size = 1 << (len(cands) - 1).bit_length()
    cands += [jnp.full_like(t1[0], -jnp.inf)] * (size - len(cands))
    best = _sort_desc(cands)
    z = jnp.ones_like(best[0])
    for v in best[1:k]:
        z = z + jnp.exp(v - best[0])
    stats = (best[k - 1], t1[0], t2[0], 1.0 / z)
    for c in range(8):
        lanes = slice(c * LANES, (c + 1) * LANES)
        for n, v in enumerate(stats):
            o_ref[n, 0, :, lanes] = v[c:c + 1]
        for b in range(k):
            t2_ref[0, b:b + 1, lanes] = t2[b][c:c + 1]


def _route(s1d, s2d):
    n_rows = s1d.shape[2]
    ntok = n_rows * LANES
    blk = 8 * LANES
    spec = pl.BlockSpec((1, N_KEYS, 8, LANES), lambda i, h: (h, 0, i, 0))
    return pl.pallas_call(
        _route_kernel,
        grid=(n_rows // 8, PEER_HEADS),
        in_specs=[spec, spec],
        out_specs=[pl.BlockSpec((4, 1, 1, blk), lambda i, h: (0, h, 0, i)),
                   pl.BlockSpec((1, PEER_TOPK, blk), lambda i, h: (h, 0, i))],
        out_shape=[jax.ShapeDtypeStruct((4, PEER_HEADS, 1, ntok), F32),
                   jax.ShapeDtypeStruct((PEER_HEADS, PEER_TOPK, ntok), F32)],
        compiler_params=_cparams(("parallel", "parallel")),
        name="peer_route",
    )(s1d, s2d)


def _peer_kernel(h2_ref, s1_ref, s2_ref, t2_ref, st_ref, u_ref, v_ref, x1_ref, m_ref, fn_ref, yc_ref, yl_ref,
                 e2_s, act_s, aw_s, acc_s, *, ctx_tiles):
    k = pl.program_id(1)
    chunks = [slice(c * LANES, (c + 1) * LANES) for c in range(PEER_TN // LANES)]

    @pl.when(k == 0)
    def _():
        for hd in range(PEER_HEADS):
            e2_s[hd] = jnp.exp(s2_ref[hd] - st_ref[2, hd])
        acc_s[...] = jnp.zeros_like(acc_s)

    floor, gain = [], []
    for hd in range(PEER_HEADS):
        s1 = s1_ref[hd]
        low = jnp.full_like(s1, jnp.inf)
        for b in range(PEER_TOPK):
            t2b = t2_ref[hd, b:b + 1, :]
            low = jnp.where(s1 + t2b >= st_ref[0, hd], t2b, low)
        floor.append(low)
        gain.append(jnp.exp(s1 - st_ref[1, hd]) * (0.5 * st_ref[3, hd]))

    act_s[...] = lax.dot_general(u_ref[...].astype(BF16), h2_ref[...], (((1,), (1,)), ((), ())),
                                 preferred_element_type=F32)
    for ii in range(PEER_ROWS):
        rows = slice(ii * N_KEYS, (ii + 1) * N_KEYS)
        for cs in chunks:
            w = None
            for hd in range(PEER_HEADS):
                hit = s2_ref[hd, :, cs] >= floor[hd][ii:ii + 1, cs]
                term = jnp.where(hit, e2_s[hd, :, cs], 0.0) * gain[hd][ii:ii + 1, cs]
                w = term if w is None else w + term
            x = act_s[rows, cs]
            t = jnp.tanh(x * (GELU_C0 + GELU_C1 * (x * x)))
            aw_s[rows, cs] = ((x * w) * (1.0 + t)).astype(BF16)
    acc_s[...] += lax.dot_general(aw_s[...], v_ref[...].astype(BF16), (((0,), (0,)), ((), ())),
                                  preferred_element_type=F32)

    last = k == pl.num_programs(1) - 1
    is_ctx = pl.program_id(0) < ctx_tiles
    for cond, y_ref in ((is_ctx, yc_ref), (jnp.logical_not(is_ctx), yl_ref)):
        @pl.when(last & cond)
        def _(y_ref=y_ref):
            y_ref[...] = _rms(x1_ref[...] + m_ref[0][5:6] * acc_s[...], fn_ref[...])


def _peer(h2, s1t, s2t, t2, stats, u_tab, v_tab, x1, m6, mrow, final_norm, ctx_tokens):
    ntok = h2.shape[0]
    tn, te = PEER_TN, PEER_TE
    nc = ctx_tokens // tn
    tok = pl.BlockSpec((tn, D_MODEL), lambda t, k: (t, 0))
    tab = pl.BlockSpec((te, D_MODEL), lambda t, k: (k, 0))
    keys = pltpu.VMEM((PEER_HEADS, N_KEYS, tn), F32)
    return pl.pallas_call(
        functools.partial(_peer_kernel, ctx_tiles=nc),
        grid=(ntok // tn, N_EXPERTS // te),
        in_specs=[tok,
                  pl.BlockSpec((PEER_HEADS, PEER_ROWS, tn), lambda t, k: (0, k, t)),
                  pl.BlockSpec((PEER_HEADS, N_KEYS, tn), lambda t, k: (0, 0, t)),
                  pl.BlockSpec((PEER_HEADS, PEER_TOPK, tn), lambda t, k: (0, 0, t)),
                  pl.BlockSpec((4, PEER_HEADS, 1, tn), lambda t, k: (0, 0, 0, t)),
                  tab, tab, tok,
                  pl.BlockSpec((1, 6, D_MODEL), lambda t, k: (mrow(t), 0, 0)),
                  pl.BlockSpec((1, D_MODEL), lambda t, k: (0, 0))],
        out_specs=[pl.BlockSpec((tn, D_MODEL), lambda t, k: (jnp.minimum(t, nc - 1), 0)),
                   pl.BlockSpec((tn, D_MODEL), lambda t, k: (jnp.maximum(t - nc, 0), 0))],
        out_shape=[jax.ShapeDtypeStruct((ctx_tokens, D_MODEL), F32),
                   jax.ShapeDtypeStruct((ntok - ctx_tokens, D_MODEL), F32)],
        scratch_shapes=[keys, pltpu.VMEM((te, tn), F32), pltpu.VMEM((te, tn), BF16),
                        pltpu.VMEM((tn, D_MODEL), F32)],
        compiler_params=_cparams(("arbitrary", "arbitrary")),
        name="peer_dense",
    )(h2, s1t, s2t, t2, stats, u_tab, v_tab, x1, m6, final_norm)


def _rot_cols(w):
    j = np.arange(QK_ROPE)
    first = (j % (QK_ROPE // 2)) < (QK_ROPE // 4)
    perm = np.where(first, j + QK_ROPE // 4, j - QK_ROPE // 4)
    sign = np.where(first, -1.0, 1.0).astype(np.float32)
    return w[..., perm] * sign


def _rope_slot(w_rope):
    pad = [(0, 0)] * (w_rope.ndim - 1)
    return jnp.pad(w_rope, pad + [(QK_NOPE, HEAD_PAD - QK_NOPE - QK_ROPE)])


def _rope_tables(n_tokens):
    n_rows = n_tokens // GRID_W
    rows = jnp.repeat(jnp.arange(n_rows, dtype=F32), GRID_W)
    cols = jnp.tile(jnp.arange(GRID_W, dtype=F32), n_rows)
    half = QK_ROPE // 2
    inv_freq = 1.0 / (ROPE_BASE ** (jnp.arange(0, half, 2, dtype=F32) / half))
    ang_r = rows[:, None] * inv_freq
    ang_c = cols[:, None] * inv_freq
    ang = jnp.concatenate([ang_r, ang_r, ang_c, ang_c], axis=-1)
    lead = (QK_NOPE, HEAD_PAD - QK_NOPE - QK_ROPE)
    cos = jnp.pad(jnp.cos(ang), [(0, 0), lead], constant_values=1.0)
    sin = jnp.pad(jnp.sin(ang), [(0, 0), lead])
    return cos, sin


def kernel(x_prompt, x_sample, c, cache_ckv, cache_krope, state_ssm, c_ctx, w_mod, b_mod, norm1, w_in,
           ssm_lam_re, ssm_lam_im, ssm_log_dt, ssm_b_re, ssm_b_im, ssm_c_re, ssm_c_im, ssm_d, w_glu,
           q_norm, w_uq, kv_norm, w_uk, w_uv, w_out, norm2, w_query, sub_keys, u_table, v_table,
           final_norm):
    bc, tc_len, _ = x_prompt.shape
    bl, tl_len, _ = x_sample.shape
    l = 0
    row = lambda a: a.reshape(1, -1)
    assert w_mod.shape[0] == 1
    big = lambda a: a.reshape(a.shape[1:])

    n_mod = 8
    cvec = jnp.concatenate([c_ctx[None], c, jnp.zeros((n_mod - 1 - bl, D_MODEL), F32)], 0)
    m6 = _modulation(cvec, big(w_mod), row(b_mod[l])).reshape(n_mod, 6, D_MODEL)

    wi = w_in[l]
    o_kr = SSM_WIDTH + Q_RANK + KV_RANK
    w_kr = wi[:, o_kr:]
    w_in_ext = jnp.concatenate([wi[:, :o_kr], _rope_slot(w_kr), _rope_slot(_rot_cols(w_kr))], 1).astype(BF16)
    wq3 = w_uq[l].reshape(Q_RANK, MLA_HEADS, QK_NOPE + QK_ROPE)
    wq_main = jnp.pad(wq3, [(0, 0), (0, 0), (0, HEAD_PAD - QK_NOPE - QK_ROPE)])
    wq_rot = _rope_slot(_rot_cols(wq3[..., QK_NOPE:]))
    w_uq_ext = jnp.concatenate([wq_main.reshape(Q_RANK, -1), wq_rot.reshape(Q_RANK, -1)], 1).astype(BF16)
    w_uk_ext = jnp.pad(w_uk[l].reshape(KV_RANK, MLA_HEADS, QK_NOPE),
                       [(0, 0), (0, 0), (0, HEAD_PAD - QK_NOPE)]).reshape(KV_RANK, -1).astype(BF16)
    w_uv_b = w_uv[l].astype(BF16)
    w_glu_b = w_glu[l].astype(BF16)
    w_out_s = w_out[l][:SSM_WIDTH].astype(BF16)
    w_out_a = w_out[l][SSM_WIDTH:].astype(BF16)
    w_query_b = w_query[l].astype(BF16)
    sub_keys_b = sub_keys[l].astype(BF16)
    u_tab, v_tab = big(u_table), big(v_table)
    cos, sin = _rope_tables(tl_len)

    bw, cw, a_blk = _s5_params(ssm_lam_re[l], ssm_lam_im[l], ssm_log_dt[l], ssm_b_re[l], ssm_b_im[l],
                               ssm_c_re[l], ssm_c_im[l])

    ctx_row = lambda i: 0
    lat_tiles = tl_len // TOK_TILE
    lat_row = lambda i: 1 + i // lat_tiles
    xc = x_prompt.reshape(bc * tc_len, D_MODEL)
    xl = x_sample.reshape(bl * tl_len, D_MODEL)

    u_c, q_c, ckv_c, kr_c = _pre(xc, m6, ctx_row, row(norm1[l]), w_in_ext, row(q_norm[l]), row(kv_norm[l]),
                                 w_uq_ext[:, :MLA_HEADS * HEAD_PAD], bc, tc_len, None)
    u_l, q_l, ckv_l, kr_l = _pre(xl, m6, lat_row, row(norm1[l]), w_in_ext, row(q_norm[l]), row(kv_norm[l]),
                                 w_uq_ext, bl, tl_len, (cos, sin), pad_rows=True)

    rl = 2 * bl
    h0_c = jnp.zeros((2, 2, S5_BLOCKS_PER_HALF, 2, bc, LANES), F32)
    yf_c, yb_c, st_c = _s5_scan(u_c.reshape(tc_len * bc, SSM_WIDTH), bw, cw, a_blk, h0_c, bc, tc_len,
                                S5_TILE_ROWS // bc)
    h0_l = _s5_state_in(state_ssm[:, l])
    h0_l = jnp.stack([h0_l, jnp.zeros_like(h0_l)], axis=-2).reshape(h0_l.shape[:4] + (rl, LANES))
    yf_l, yb_l, _ = _s5_scan(u_l.reshape(tl_len * rl, SSM_WIDTH), bw, cw, a_blk, h0_l, rl, tl_len,
                             S5_TILE_ROWS // rl)

    at_c = _attention(q_c, ckv_c.reshape(bc, tc_len, KV_RANK), kr_c.reshape(bc, tc_len, HEAD_PAD),
                      w_uk_ext, w_uv_b, bc, tc_len)
    ckv_all = jnp.concatenate([cache_ckv[:, l], ckv_l.reshape(bl, tl_len, KV_RANK)], 1)
    kr_all = jnp.concatenate([_rope_slot(cache_krope[:, l]), kr_l.reshape(bl, tl_len, HEAD_PAD)], 1)
    at_l = _attention(q_l, ckv_all, kr_all, w_uk_ext, w_uv_b, bl, tl_len)

    tm_c = lambda a: a.reshape(tc_len, bc * SSM_WIDTH)
    tm_l = lambda a: a.reshape(tl_len, rl * SSM_WIDTH)
    ntok = bc * tc_len + bl * tl_len
    x1, h2, s1t, s2t = _post((xc, at_c, tm_c(u_c), tm_c(yf_c), tm_c(yb_c), bc, tc_len, 1),
                             (xl, at_l, u_l, tm_l(yf_l), tm_l(yb_l), bl, tl_len, 2),
                             m6, row(ssm_d[l]), w_glu_b, w_out_s, w_out_a, row(norm2[l]), w_query_b, sub_keys_b)

    dense = lambda a: a.reshape(PEER_HEADS, N_KEYS, ntok // LANES, LANES)
    stats, t2 = _route(dense(s1t), dense(s2t))
    ctx_peer_tiles = bc * tc_len // PEER_TN
    lat_peer_tiles = tl_len // PEER_TN
    peer_row = lambda t: jnp.where(t < ctx_peer_tiles, 0, 1 + (t - ctx_peer_tiles) // lat_peer_tiles)
    y_c, y_l = _peer(h2, s1t, s2t, t2, stats, u_tab, v_tab, x1, m6, peer_row, row(final_norm), bc * tc_len)

    y_prompt = y_c.reshape(bc, tc_len, D_MODEL)
    y_sample = y_l.reshape(bl, tl_len, D_MODEL)
    new_ckv = ckv_c.reshape(bc, 1, tc_len, KV_RANK)
    new_krope = kr_c[:, QK_NOPE:QK_NOPE + QK_ROPE].reshape(bc, 1, tc_len, QK_ROPE)
    new_ssm = _s5_state_out(st_c)[:, None]
    return (y_prompt, y_sample, new_ckv, new_krope, new_ssm)
```

```python
import functools
import math

import jax
import jax.numpy as jnp
import numpy as np
from jax import lax
from jax.experimental import pallas as pl
from jax.experimental.pallas import tpu as pltpu

F32 = jnp.float32
BF16 = jnp.bfloat16

D_MODEL = 1024
GRID_W = 64
EPS = 1e-6
SSM_WIDTH = 512
SSM_GROUP = 16
SSM_GROUPS = 32
SSM_STATE = 64
MLA_HEADS = 8
QK_NOPE = 64
QK_ROPE = 32
V_DIM = 64
Q_RANK = 384
KV_RANK = 256
ROPE_BASE = 10000.0
N_KEYS = 128
N_EXPERTS = N_KEYS * N_KEYS
PEER_HEADS = 8
PEER_TOPK = 16
KEY_DIM = 128

LANES = 128
HEAD_PAD = 128
TOK_TILE = 256
S5_PAIR = 2
S5_BLOCKS = SSM_GROUPS // S5_PAIR
S5_HALF = SSM_WIDTH // 2
S5_BLOCKS_PER_HALF = S5_BLOCKS // 2
S5_TILE_ROWS = 1024
PEER_TN = 512
PEER_ROWS = 8
PEER_TE = PEER_ROWS * N_KEYS
GELU_C0 = math.sqrt(2.0 / math.pi)
GELU_C1 = 0.044715 * GELU_C0
VMEM_LIMIT = 54 * 1024 * 1024


def _cparams(sem):
    return pltpu.CompilerParams(dimension_semantics=sem, vmem_limit_bytes=VMEM_LIMIT)


def _rms(x, g):
    return x * lax.rsqrt(jnp.mean(x * x, axis=-1, keepdims=True) + EPS) * g


def _mod_kernel(c_ref, w_ref, b_ref, o_ref):
    o_ref[...] = jnp.dot(jax.nn.silu(c_ref[...]), w_ref[...], preferred_element_type=F32) + b_ref[...]


def _modulation(cvec, w_mod, b_mod):
    rows, d = cvec.shape
    n = w_mod.shape[1]
    tn = 1536
    return pl.pallas_call(
        _mod_kernel,
        grid=(n // tn,),
        in_specs=[pl.BlockSpec((rows, d), lambda j: (0, 0)),
                  pl.BlockSpec((d, tn), lambda j: (0, j)),
                  pl.BlockSpec((1, tn), lambda j: (0, j))],
        out_specs=pl.BlockSpec((rows, tn), lambda j: (0, j)),
        out_shape=jax.ShapeDtypeStruct((rows, n), F32),
        compiler_params=_cparams(("parallel",)),
        name="modulation",
    )(cvec, w_mod, b_mod)


def _pre_kernel(use_rope, pad_rows, x_ref, m_ref, n1_ref, win_ref, qn_ref, kvn_ref, wuq_ref, *rest):
    if use_rope:
        cos_ref, sin_ref, u_ref, q_ref, ckv_ref, kr_ref = rest
    else:
        u_ref, q_ref, ckv_ref, kr_ref = rest
    m = m_ref[0]
    h = _rms(x_ref[...], n1_ref[...]) * (1.0 + m[1:2]) + m[0:1]
    z = jnp.dot(h.astype(BF16), win_ref[...], preferred_element_type=F32)
    u_ref[:, :SSM_WIDTH] = z[:, :SSM_WIDTH]
    if pad_rows:
        u_ref[:, SSM_WIDTH:] = jnp.zeros((TOK_TILE, SSM_WIDTH), F32)
    o_q, o_kv, o_kr = SSM_WIDTH, SSM_WIDTH + Q_RANK, SSM_WIDTH + Q_RANK + KV_RANK
    cqn = _rms(z[:, o_q:o_kv], qn_ref[...])
    qq = jnp.dot(cqn.astype(BF16), wuq_ref[...], preferred_element_type=F32)
    ckv_ref[...] = _rms(z[:, o_kv:o_kr], kvn_ref[...])
    kr = z[:, o_kr:o_kr + HEAD_PAD]
    scale = (QK_NOPE + QK_ROPE) ** -0.5
    nq = MLA_HEADS * HEAD_PAD
    if use_rope:
        cos, sin = cos_ref[...], sin_ref[...]
        kr = kr * cos + z[:, o_kr + HEAD_PAD:o_kr + 2 * HEAD_PAD] * sin
        for hd in range(MLA_HEADS):
            sl = slice(hd * HEAD_PAD, (hd + 1) * HEAD_PAD)
            qh = qq[:, sl] * cos + qq[:, nq + hd * HEAD_PAD:nq + (hd + 1) * HEAD_PAD] * sin
            q_ref[:, sl] = (qh * scale).astype(BF16)
    else:
        q_ref[...] = (qq[:, :nq] * scale).astype(BF16)
    kr_ref[...] = kr


def _pre(x2d, m6, mrow, norm1, w_in_ext, q_norm, kv_norm, w_uq_ext, batch, seq, rope, pad_rows=False):
    ntok = batch * seq
    u_cols = SSM_WIDTH * (2 if pad_rows else 1)
    n_t = seq // TOK_TILE
    use_rope = rope is not None
    full = lambda a: pl.BlockSpec(a.shape, lambda i: (0,) * a.ndim)
    in_specs = [pl.BlockSpec((TOK_TILE, D_MODEL), lambda i: (i, 0)),
                pl.BlockSpec((1, 6, D_MODEL), lambda i: (mrow(i), 0, 0)),
                full(norm1), full(w_in_ext), full(q_norm), full(kv_norm), full(w_uq_ext)]
    args = [x2d, m6, norm1, w_in_ext, q_norm, kv_norm, w_uq_ext]
    if use_rope:
        in_specs += [pl.BlockSpec((TOK_TILE, HEAD_PAD), lambda i: (i % n_t, 0))] * 2
        args += list(rope)
    out_specs = [pl.BlockSpec((TOK_TILE, u_cols), lambda i: (i % n_t, i // n_t)),
                 pl.BlockSpec((TOK_TILE, MLA_HEADS * HEAD_PAD), lambda i: (i, 0)),
                 pl.BlockSpec((TOK_TILE, KV_RANK), lambda i: (i, 0)),
                 pl.BlockSpec((TOK_TILE, HEAD_PAD), lambda i: (i, 0))]
    out_shape = [jax.ShapeDtypeStruct((seq, batch * u_cols), F32),
                 jax.ShapeDtypeStruct((ntok, MLA_HEADS * HEAD_PAD), BF16),
                 jax.ShapeDtypeStruct((ntok, KV_RANK), F32),
                 jax.ShapeDtypeStruct((ntok, HEAD_PAD), F32)]
    return pl.pallas_call(
        functools.partial(_pre_kernel, use_rope, pad_rows),
        grid=(ntok // TOK_TILE,),
        in_specs=in_specs, out_specs=out_specs, out_shape=out_shape,
        compiler_params=_cparams(("parallel",)),
        name="pre_rope" if use_rope else "pre",
    )(*args)


def _s5_param_kernel(lr_ref, li_ref, ldt_ref, lrx_ref, lix_ref, ldtx_ref, bre_ref, bim_ref, cim_ref,
                     abr_ref, abi_ref, bfr_ref, bfi_ref, ncim_ref):
    def disc(lr, li, ldt):
        dt = jnp.exp(ldt)
        mag = jnp.exp(lr * dt)
        ab_re, ab_im = mag * jnp.cos(li * dt), mag * jnp.sin(li * dt)
        den = lr * lr + li * li
        br, bi = lr / den, -li / den
        ar = ab_re - 1.0
        return ab_re, ab_im, ar * br - ab_im * bi, ar * bi + ab_im * br

    ab_re, ab_im, _, _ = disc(lr_ref[...], li_ref[...], ldt_ref[...])
    abr_ref[...] = ab_re
    abi_ref[...] = ab_im
    _, _, f_re, f_im = disc(lrx_ref[...], lix_ref[...], ldtx_ref[...])
    b_re, b_im = bre_ref[...], bim_ref[...]
    bfr_ref[...] = f_re * b_re - f_im * b_im
    bfi_ref[...] = f_re * b_im + f_im * b_re
    ncim_ref[...] = -cim_ref[...]


def _s5_params(lam_re, lam_im, log_dt, b_re, b_im, c_re, c_im):
    dg = 2 * SSM_GROUPS
    n, p = SSM_STATE, SSM_GROUP
    lr = lam_re.reshape(dg, n)
    li = lam_im.reshape(dg, n)
    ldt = jnp.broadcast_to(log_dt.reshape(dg, 1), (dg, n))
    rep = lambda a: jnp.repeat(a, p, axis=1)
    args = [lr, li, ldt, rep(lr), rep(li), rep(ldt),
            b_re.reshape(dg, n * p), b_im.reshape(dg, n * p), c_im.reshape(dg, p * n)]
    small = jax.ShapeDtypeStruct((dg, n), F32)
    big = jax.ShapeDtypeStruct((dg, n * p), F32)
    ab_re, ab_im, bf_re, bf_im, ncim = pl.pallas_call(
        _s5_param_kernel, out_shape=[small, small, big, big, big], name="s5_params")(*args)

    nb, gp = S5_BLOCKS, S5_PAIR
    eye = jnp.eye(gp, dtype=F32)
    pos = np.arange(nb) % S5_BLOCKS_PER_HALF
    bf = jnp.stack([bf_re, bf_im], 0).reshape(2, 2, nb, gp, n, p)
    bc = jnp.einsum('adbgnp,gh->bdgpahn', bf, eye).reshape(nb, 2, gp * p, 2 * gp * n)
    bw = jnp.zeros((nb, 2, S5_BLOCKS_PER_HALF, gp * p, 2 * gp * n), F32)
    bw = bw.at[np.arange(nb), :, pos].set(bc).reshape(nb, 2, S5_HALF, 2 * gp * n)
    cc = jnp.stack([c_re.reshape(2, nb, gp, p, n), ncim.reshape(2, nb, gp, p, n)], 0)
    cc = jnp.einsum('adbgpn,gh->bdagnhp', cc, eye).reshape(nb, 2, 2 * gp * n, gp * p)
    cw = jnp.zeros((nb, 2, 2 * gp * n, S5_BLOCKS_PER_HALF, gp * p), F32)
    cw = cw.at[np.arange(nb), :, :, pos].set(cc).reshape(nb, 2, 2 * gp * n, S5_HALF)
    a = jnp.stack([ab_re, ab_im], 0).reshape(2, 2, nb, gp * n).transpose(2, 1, 0, 3)
    hb = S5_BLOCKS_PER_HALF
    bw = bw.reshape(2, hb, 2, S5_HALF, 2 * gp * n).transpose(0, 2, 3, 1, 4).reshape(2, 2, S5_HALF, -1)
    cw = cw.reshape(2, hb, 2, 2 * gp * n, S5_HALF).transpose(0, 2, 1, 3, 4).reshape(2, 2, -1, S5_HALF)
    a = a.reshape(2, hb, 2, 2, 1, gp * n).transpose(0, 2, 1, 3, 4, 5)
    return bw.astype(BF16), cw.astype(BF16), a


def _s5_state_in(h):
    bsz = h.shape[0]
    return h.reshape(bsz, 2, 2, 2, S5_BLOCKS_PER_HALF, LANES).transpose(3, 1, 4, 2, 0, 5)


def _s5_state_out(st):
    bsz = st.shape[4]
    return st.transpose(4, 1, 3, 0, 2, 5).reshape(bsz, 2, 2, SSM_GROUPS, SSM_STATE)


def _s5_kernel(rows_per_step, steps, uf_ref, ub_ref, bw_ref, cw_ref, a_ref, h0_ref,
               yf_ref, yb_ref, st_ref, buf, coef, carry):
    r = rows_per_step
    nb = S5_BLOCKS_PER_HALF
    width = 2 * LANES

    @pl.when(pl.program_id(1) == 0)
    def _():
        carry[...] = h0_ref[0]
        coef[...] = jnp.broadcast_to(a_ref[0], coef.shape)

    for d, u_ref in enumerate((uf_ref, ub_ref)):
        buf[d] = jnp.dot(u_ref[...].astype(BF16), bw_ref[0, d], preferred_element_type=F32)

    chains = [(d, b) for d in range(2) for b in range(nb)]
    for g in range(r // 8):
        rows8 = slice(g * 8, (g + 1) * 8)

        def body(k, hs, rows8=rows8):
            out = []
            for (d, b), (h_re, h_im) in zip(chains, hs):
                t = k if d == 0 else steps - 1 - k
                r0 = pl.multiple_of(t * r + g * 8, 8)
                a_re, a_im = coef[d, b, 0], coef[d, b, 1]
                re_l, im_l = slice(b * width, b * width + LANES), slice(b * width + LANES, (b + 1) * width)
                n_re = a_re * h_re - a_im * h_im + buf[d, pl.ds(r0, 8), re_l]
                n_im = a_re * h_im + a_im * h_re + buf[d, pl.ds(r0, 8), im_l]
                buf[d, pl.ds(r0, 8), re_l] = n_re
                buf[d, pl.ds(r0, 8), im_l] = n_im
                out.append((n_re, n_im))
            return tuple(out)

        init = tuple((carry[d, b, 0, rows8], carry[d, b, 1, rows8]) for d, b in chains)
        final = lax.fori_loop(0, steps, body, init)
        for (d, b), (h_re, h_im) in zip(chains, final):
            carry[d, b, 0, rows8] = h_re
            carry[d, b, 1, rows8] = h_im

    for d, y_ref in enumerate((yf_ref, yb_ref)):
        y_ref[...] = jnp.dot(buf[d].astype(BF16), cw_ref[0, d], preferred_element_type=F32)
    st_ref[0] = carry[...]


def _s5_scan(u_tm, bw, cw, a, h0, rows_per_step, seq, steps):
    r = rows_per_step
    n_t = seq // steps
    rows = steps * r
    nb = S5_BLOCKS_PER_HALF
    wide = nb * 2 * LANES
    half4 = lambda h, t: (h, 0, 0, 0)
    half6 = lambda h, t: (h, 0, 0, 0, 0, 0)
    state = (2, nb, 2, r, LANES)
    in_specs = [pl.BlockSpec((rows, S5_HALF), lambda h, t: (t, h)),
                pl.BlockSpec((rows, S5_HALF), lambda h, t: (n_t - 1 - t, h)),
                pl.BlockSpec((1, 2, S5_HALF, wide), half4),
                pl.BlockSpec((1, 2, wide, S5_HALF), half4),
                pl.BlockSpec((1, 2, nb, 2, 1, LANES), half6),
                pl.BlockSpec((1,) + state, half6)]
    out_specs = [pl.BlockSpec((rows, S5_HALF), lambda h, t: (t, h)),
                 pl.BlockSpec((rows, S5_HALF), lambda h, t: (n_t - 1 - t, h)),
                 pl.BlockSpec((1,) + state, half6)]
    out_shape = [jax.ShapeDtypeStruct(u_tm.shape, F32), jax.ShapeDtypeStruct(u_tm.shape, F32),
                 jax.ShapeDtypeStruct((2,) + state, F32)]
    return pl.pallas_call(
        functools.partial(_s5_kernel, r, steps),
        grid=(2, n_t),
        in_specs=in_specs, out_specs=out_specs, out_shape=out_shape,
        scratch_shapes=[pltpu.VMEM((2, rows, wide), F32), pltpu.VMEM((2, nb, 2, 8, LANES), F32),
                        pltpu.VMEM(state, F32)],
        compiler_params=_cparams(("arbitrary", "arbitrary")),
        name="s5_scan",
    )(u_tm, u_tm, bw, cw, a, h0)


def _attn_kernel(q_ref, ckv_ref, kr_ref, wuk_ref, wuv_ref, o_ref, k_s, v_s):
    @pl.when(pl.program_id(1) == 0)
    def _():
        kv = ckv_ref[0].astype(BF16)
        kn = jnp.dot(kv, wuk_ref[...], preferred_element_type=F32)
        kr = kr_ref[0]
        for hd in range(MLA_HEADS):
            k_s[hd] = (kn[:, hd * HEAD_PAD:(hd + 1) * HEAD_PAD] + kr).astype(BF16)
        v_s[...] = jnp.dot(kv, wuv_ref[...], preferred_element_type=F32).astype(BF16)

    for hd in range(MLA_HEADS):
        qh = q_ref[:, hd * HEAD_PAD:(hd + 1) * HEAD_PAD]
        s = lax.dot_general(qh, k_s[hd], (((1,), (1,)), ((), ())), preferred_element_type=F32)
        p = jnp.exp(s - jnp.max(s, axis=-1, keepdims=True))
        l = jnp.sum(p, axis=-1, keepdims=True)
        o = jnp.dot(p.astype(BF16), v_s[:, hd * V_DIM:(hd + 1) * V_DIM], preferred_element_type=F32)
        o_ref[:, hd * V_DIM:(hd + 1) * V_DIM] = (o / l).astype(BF16)


def _attention(q, ckv_all, kr_all, w_uk_ext, w_uv, batch, seq):
    s_len = ckv_all.shape[1]
    n_q = seq // TOK_TILE
    return pl.pallas_call(
        _attn_kernel,
        grid=(batch, n_q),
        in_specs=[pl.BlockSpec((TOK_TILE, MLA_HEADS * HEAD_PAD), lambda b, i: (b * n_q + i, 0)),
                  pl.BlockSpec((1, s_len, KV_RANK), lambda b, i: (b, 0, 0)),
                  pl.BlockSpec((1, s_len, HEAD_PAD), lambda b, i: (b, 0, 0)),
                  pl.BlockSpec(w_uk_ext.shape, lambda b, i: (0, 0)),
                  pl.BlockSpec(w_uv.shape, lambda b, i: (0, 0))],
        out_specs=pl.BlockSpec((TOK_TILE, MLA_HEADS * V_DIM), lambda b, i: (b * n_q + i, 0)),
        out_shape=jax.ShapeDtypeStruct((batch * seq, MLA_HEADS * V_DIM), BF16),
        scratch_shapes=[pltpu.VMEM((MLA_HEADS, s_len, HEAD_PAD), BF16),
                        pltpu.VMEM((s_len, MLA_HEADS * V_DIM), BF16)],
        compiler_params=_cparams(("parallel", "arbitrary")),
        name="attention",
    )(q, ckv_all, kr_all, w_uk_ext, w_uv)


def _post_kernel(n_ctx, xc_ref, atc_ref, uc_ref, yfc_ref, ybc_ref, xl_ref, atl_ref, ul_ref, yfl_ref, ybl_ref,
                 m_ref, d_ref, wglu_ref, wos_ref, woa_ref, n2_ref, wq_ref, sk_ref,
                 x1_ref, h2_ref, s1_ref, s2_ref):
    is_ctx = pl.program_id(0) < n_ctx
    pick = lambda c_ref, l_ref: jnp.where(is_ctx, c_ref[...], l_ref[...])
    m = m_ref[0]
    y = pick(yfc_ref, yfl_ref) + pick(ybc_ref, ybl_ref) + pick(uc_ref, ul_ref) * d_ref[...]
    yg = jax.nn.gelu(y)
    gate = jax.nn.sigmoid(jnp.dot(yg.astype(BF16), wglu_ref[...], preferred_element_type=F32))
    mix = (jnp.dot((yg * gate).astype(BF16), wos_ref[...], preferred_element_type=F32)
           + jnp.dot(pick(atc_ref, atl_ref), woa_ref[...], preferred_element_type=F32))
    x1 = pick(xc_ref, xl_ref) + m[2:3] * mix
    x1_ref[...] = x1
    h2 = (_rms(x1, n2_ref[...]) * (1.0 + m[4:5]) + m[3:4]).astype(BF16)
    h2_ref[...] = h2
    qp = jnp.dot(h2, wq_ref[...], preferred_element_type=F32).astype(BF16)
    for hd in range(PEER_HEADS):
        for half, s_ref in enumerate((s1_ref, s2_ref)):
            c0 = (hd * 2 + half) * KEY_DIM
            s_ref[hd] = lax.dot_general(sk_ref[hd, half], qp[:, c0:c0 + KEY_DIM],
                                        (((1,), (1,)), ((), ())), preferred_element_type=F32)


def _post(ctx, lat, m6, ssm_d, w_glu, w_out_s, w_out_a, norm2, w_query, sub_keys):
    specs, args = [], []
    n_ctx, n_lat = (p[5] * p[6] // TOK_TILE for p in (ctx, lat))
    for pass_id, (x2d, attn, u_tm, yf, yb, batch, seq, stride) in enumerate((ctx, lat)):
        n_t = seq // TOK_TILE
        loc = (lambda i: jnp.minimum(i, n_ctx - 1)) if pass_id == 0 else (lambda i: jnp.maximum(i - n_ctx, 0))
        tok = lambda w, loc=loc: pl.BlockSpec((TOK_TILE, w), lambda i: (loc(i), 0))
        tm = pl.BlockSpec((TOK_TILE, SSM_WIDTH),
                          lambda i, loc=loc, n_t=n_t, stride=stride: (loc(i) % n_t, (loc(i) // n_t) * stride))
        specs += [tok(D_MODEL), tok(MLA_HEADS * V_DIM), tm, tm, tm]
        args += [x2d, attn, u_tm, yf, yb]
    lat_tiles_per_row = lat[6] // TOK_TILE
    mrow = lambda i: jnp.where(i < n_ctx, 0, 1 + (i - n_ctx) // lat_tiles_per_row)
    full = lambda a: pl.BlockSpec(a.shape, lambda i: (0,) * a.ndim)
    weights = [ssm_d, w_glu, w_out_s, w_out_a, norm2, w_query, sub_keys]
    total = (n_ctx + n_lat) * TOK_TILE
    tok_out = lambda w: pl.BlockSpec((TOK_TILE, w), lambda i: (i, 0))
    sc = pl.BlockSpec((PEER_HEADS, N_KEYS, TOK_TILE), lambda i: (0, 0, i))
    return pl.pallas_call(
        functools.partial(_post_kernel, n_ctx),
        grid=(n_ctx + n_lat,),
        in_specs=specs + [pl.BlockSpec((1, 6, D_MODEL), lambda i: (mrow(i), 0, 0))] + [full(w) for w in weights],
        out_specs=[tok_out(D_MODEL), tok_out(D_MODEL), sc, sc],
        out_shape=[jax.ShapeDtypeStruct((total, D_MODEL), F32), jax.ShapeDtypeStruct((total, D_MODEL), BF16),
                   jax.ShapeDtypeStruct((PEER_HEADS, N_KEYS, total), F32),
                   jax.ShapeDtypeStruct((PEER_HEADS, N_KEYS, total), F32)],
        compiler_params=_cparams(("arbitrary",)),
        name="post",
    )(*args, m6, *weights)


def _sort_pairs(lo, hi):
    def merge(lo, hi, r):
        step = r * 2
        if step < hi - lo:
            yield from merge(lo, hi, step)
            yield from merge(lo + r, hi, step)
            for i in range(lo + r, hi - r, step):
                yield (i, i + r)
        else:
            yield (lo, lo + r)

    if hi - lo >= 1:
        mid = lo + (hi - lo) // 2
        yield from _sort_pairs(lo, mid)
        yield from _sort_pairs(mid + 1, hi)
        yield from merge(lo, hi, 1)


def _sort_desc(vals):
    vals = list(vals)
    for i, j in _sort_pairs(0, len(vals) - 1):
        vals[i], vals[j] = jnp.maximum(vals[i], vals[j]), jnp.minimum(vals[i], vals[j])
    return vals


def _merge_top(a, b):
    n = len(a)
    c = [jnp.maximum(a[k], b[n - 1 - k]) for k in range(n)]
    stride = n // 2
    while stride:
        for i in range(n):
            if not i & stride:
                c[i], c[i + stride] = jnp.maximum(c[i], c[i + stride]), jnp.minimum(c[i], c[i + stride])
        stride //= 2
    return c


def _top_keys(ref):
    k = PEER_TOPK
    groups = [_sort_desc([ref[0, g * k + r] for r in range(k)]) for g in range(N_KEYS // k)]
    while len(groups) > 1:
        groups = [_merge_top(groups[i], groups[i + 1]) for i in range(0, len(groups), 2)]
    return groups[0]


def _route_kernel(s1_ref, s2_ref, o_ref, t2_ref):
    k = PEER_TOPK
    t1, t2 = _top_keys(s1_ref), _top_keys(s2_ref)
    cands = [t1[a] + t2[b] for a in range(k) for b in range(k) if (a + 1) * (b + 1) <= k]
    size = 1 << (len(cands) - 1).bit_length()
    cands += [jnp.full_like(t1[0], -jnp.inf)] * (size - len(cands))
    best = _sort_desc(cands)
    z = jnp.ones_like(best[0])
    for v in best[1:k]:
        z = z + jnp.exp(v - best[0])
    stats = (best[k - 1], t1[0], t2[0], 1.0 / z)
    for c in range(8):
        lanes = slice(c * LANES, (c + 1) * LANES)
        for n, v in enumerate(stats):
            o_ref[n, 0, :, lanes] = v[c:c + 1]
        for b in range(k):
            t2_ref[0, b:b + 1, lanes] = t2[b][c:c + 1]


def _route(s1d, s2d):
    n_rows = s1d.shape[2]
    ntok = n_rows * LANES
    blk = 8 * LANES
    spec = pl.BlockSpec((1, N_KEYS, 8, LANES), lambda i, h: (h, 0, i, 0))
    return pl.pallas_call(
        _route_kernel,
        grid=(n_rows // 8, PEER_HEADS),
        in_specs=[spec, spec],
        out_specs=[pl.BlockSpec((4, 1, 1, blk), lambda i, h: (0, h, 0, i)),
                   pl.BlockSpec((1, PEER_TOPK, blk), lambda i, h: (h, 0, i))],
        out_shape=[jax.ShapeDtypeStruct((4, PEER_HEADS, 1, ntok), F32),
                   jax.ShapeDtypeStruct((PEER_HEADS, PEER_TOPK, ntok), F32)],
        compiler_params=_cparams(("parallel", "parallel")),
        name="peer_route",
    )(s1d, s2d)


def _peer_kernel(h2_ref, s1_ref, s2_ref, t2_ref, st_ref, u_ref, v_ref, x1_ref, m_ref, fn_ref, yc_ref, yl_ref,
                 e2_s, act_s, aw_s, acc_s, *, ctx_tiles):
    k = pl.program_id(1)
    chunks = [slice(c * LANES, (c + 1) * LANES) for c in range(PEER_TN // LANES)]

    @pl.when(k == 0)
    def _():
        for hd in range(PEER_HEADS):
            e2_s[hd] = jnp.exp(s2_ref[hd] - st_ref[2, hd])
        acc_s[...] = jnp.zeros_like(acc_s)

    floor, gain = [], []
    for hd in range(PEER_HEADS):
        s1 = s1_ref[hd]
        low = jnp.full_like(s1, jnp.inf)
        for b in range(PEER_TOPK):
            t2b = t2_ref[hd, b:b + 1, :]
            low = jnp.where(s1 + t2b >= st_ref[0, hd], t2b, low)
        floor.append(low)
        gain.append(jnp.exp(s1 - st_ref[1, hd]) * (0.5 * st_ref[3, hd]))

    act_s[...] = lax.dot_general(u_ref[...].astype(BF16), h2_ref[...], (((1,), (1,)), ((), ())),
                                 preferred_element_type=F32)
    for ii in range(PEER_ROWS):
        rows = slice(ii * N_KEYS, (ii + 1) * N_KEYS)
        for cs in chunks:
            w = None
            for hd in range(PEER_HEADS):
                hit = s2_ref[hd, :, cs] >= floor[hd][ii:ii + 1, cs]
                term = jnp.where(hit, e2_s[hd, :, cs], 0.0) * gain[hd][ii:ii + 1, cs]
                w = term if w is None else w + term
            x = act_s[rows, cs]
            t = jnp.tanh(x * (GELU_C0 + GELU_C1 * (x * x)))
            aw_s[rows, cs] = ((x * w) * (1.0 + t)).astype(BF16)
    acc_s[...] += lax.dot_general(aw_s[...], v_ref[...].astype(BF16), (((0,), (0,)), ((), ())),
                                  preferred_element_type=F32)

    last = k == pl.num_programs(1) - 1
    is_ctx = pl.program_id(0) < ctx_tiles
    for cond, y_ref in ((is_ctx, yc_ref), (jnp.logical_not(is_ctx), yl_ref)):
        @pl.when(last & cond)
        def _(y_ref=y_ref):
            y_ref[...] = _rms(x1_ref[...] + m_ref[0][5:6] * acc_s[...], fn_ref[...])


def _peer(h2, s1t, s2t, t2, stats, u_tab, v_tab, x1, m6, mrow, final_norm, ctx_tokens):
    ntok = h2.shape[0]
    tn, te = PEER_TN, PEER_TE
    nc = ctx_tokens // tn
    tok = pl.BlockSpec((tn, D_MODEL), lambda t, k: (t, 0))
    tab = pl.BlockSpec((te, D_MODEL), lambda t, k: (k, 0))
    keys = pltpu.VMEM((PEER_HEADS, N_KEYS, tn), F32)
    return pl.pallas_call(
        functools.partial(_peer_kernel, ctx_tiles=nc),
        grid=(ntok // tn, N_EXPERTS // te),
        in_specs=[tok,
                  pl.BlockSpec((PEER_HEADS, PEER_ROWS, tn), lambda t, k: (0, k, t)),
                  pl.BlockSpec((PEER_HEADS, N_KEYS, tn), lambda t, k: (0, 0, t)),
                  pl.BlockSpec((PEER_HEADS, PEER_TOPK, tn), lambda t, k: (0, 0, t)),
                  pl.BlockSpec((4, PEER_HEADS, 1, tn), lambda t, k: (0, 0, 0, t)),
                  tab, tab, tok,
                  pl.BlockSpec((1, 6, D_MODEL), lambda t, k: (mrow(t), 0, 0)),
                  pl.BlockSpec((1, D_MODEL), lambda t, k: (0, 0))],
        out_specs=[pl.BlockSpec((tn, D_MODEL), lambda t, k: (jnp.minimum(t, nc - 1), 0)),
                   pl.BlockSpec((tn, D_MODEL), lambda t, k: (jnp.maximum(t - nc, 0), 0))],
        out_shape=[jax.ShapeDtypeStruct((ctx_tokens, D_MODEL), F32),
                   jax.ShapeDtypeStruct((ntok - ctx_tokens, D_MODEL), F32)],
        scratch_shapes=[keys, pltpu.VMEM((te, tn), F32), pltpu.VMEM((te, tn), BF16),
                        pltpu.VMEM((tn, D_MODEL), F32)],
        compiler_params=_cparams(("arbitrary", "arbitrary")),
        name="peer_dense",
    )(h2, s1t, s2t, t2, stats, u_tab, v_tab, x1, m6, final_norm)


def _rot_cols(w):
    j = np.arange(QK_ROPE)
    first = (j % (QK_ROPE // 2)) < (QK_ROPE // 4)
    perm = np.where(first, j + QK_ROPE // 4, j - QK_ROPE // 4)
    sign = np.where(first, -1.0, 1.0).astype(np.float32)
    return w[..., perm] * sign


def _rope_slot(w_rope):
    pad = [(0, 0)] * (w_rope.ndim - 1)
    return jnp.pad(w_rope, pad + [(QK_NOPE, HEAD_PAD - QK_NOPE - QK_ROPE)])


def _rope_tables(n_tokens):
    n_rows = n_tokens // GRID_W
    rows = jnp.repeat(jnp.arange(n_rows, dtype=F32), GRID_W)
    cols = jnp.tile(jnp.arange(GRID_W, dtype=F32), n_rows)
    half = QK_ROPE // 2
    inv_freq = 1.0 / (ROPE_BASE ** (jnp.arange(0, half, 2, dtype=F32) / half))
    ang_r = rows[:, None] * inv_freq
    ang_c = cols[:, None] * inv_freq
    ang = jnp.concatenate([ang_r, ang_r, ang_c, ang_c], axis=-1)
    lead = (QK_NOPE, HEAD_PAD - QK_NOPE - QK_ROPE)
    cos = jnp.pad(jnp.cos(ang), [(0, 0), lead], constant_values=1.0)
    sin = jnp.pad(jnp.sin(ang), [(0, 0), lead])
    return cos, sin


def kernel(x_prompt, x_sample, c, cache_ckv, cache_krope, state_ssm, c_ctx, w_mod, b_mod, norm1, w_in,
           ssm_lam_re, ssm_lam_im, ssm_log_dt, ssm_b_re, ssm_b_im, ssm_c_re, ssm_c_im, ssm_d, w_glu,
           q_norm, w_uq, kv_norm, w_uk, w_uv, w_out, norm2, w_query, sub_keys, u_table, v_table,
           final_norm):
    bc, tc_len, _ = x_prompt.shape
    bl, tl_len, _ = x_sample.shape
    l = 0
    row = lambda a: a.reshape(1, -1)
    assert w_mod.shape[0] == 1
    big = lambda a: a.reshape(a.shape[1:])

    n_mod = 8
    cvec = jnp.concatenate([c_ctx[None], c, jnp.zeros((n_mod - 1 - bl, D_MODEL), F32)], 0)
    m6 = _modulation(cvec, big(w_mod), row(b_mod[l])).reshape(n_mod, 6, D_MODEL)

    wi = w_in[l]
    o_kr = SSM_WIDTH + Q_RANK + KV_RANK
    w_kr = wi[:, o_kr:]
    w_in_ext = jnp.concatenate([wi[:, :o_kr], _rope_slot(w_kr), _rope_slot(_rot_cols(w_kr))], 1).astype(BF16)
    wq3 = w_uq[l].reshape(Q_RANK, MLA_HEADS, QK_NOPE + QK_ROPE)
    wq_main = jnp.pad(wq3, [(0, 0), (0, 0), (0, HEAD_PAD - QK_NOPE - QK_ROPE)])
    wq_rot = _rope_slot(_rot_cols(wq3[..., QK_NOPE:]))
    w_uq_ext = jnp.concatenate([wq_main.reshape(Q_RANK, -1), wq_rot.reshape(Q_RANK, -1)], 1).astype(BF16)
    w_uk_ext = jnp.pad(w_uk[l].reshape(KV_RANK, MLA_HEADS, QK_NOPE),
                       [(0, 0), (0, 0), (0, HEAD_PAD - QK_NOPE)]).reshape(KV_RANK, -1).astype(BF16)
    w_uv_b = w_uv[l].astype(BF16)
    w_glu_b = w_glu[l].astype(BF16)
    w_out_s = w_out[l][:SSM_WIDTH].astype(BF16)
    w_out_a = w_out[l][SSM_WIDTH:].astype(BF16)
    w_query_b = w_query[l].astype(BF16)
    sub_keys_b = sub_keys[l].astype(BF16)
    u_tab, v_tab = big(u_table), big(v_table)
    cos, sin = _rope_tables(tl_len)

    bw, cw, a_blk = _s5_params(ssm_lam_re[l], ssm_lam_im[l], ssm_log_dt[l], ssm_b_re[l], ssm_b_im[l],
                               ssm_c_re[l], ssm_c_im[l])

    ctx_row = lambda i: 0
    lat_tiles = tl_len // TOK_TILE
    lat_row = lambda i: 1 + i // lat_tiles
    xc = x_prompt.reshape(bc * tc_len, D_MODEL)
    xl = x_sample.reshape(bl * tl_len, D_MODEL)

    u_c, q_c, ckv_c, kr_c = _pre(xc, m6, ctx_row, row(norm1[l]), w_in_ext, row(q_norm[l]), row(kv_norm[l]),
                                 w_uq_ext[:, :MLA_HEADS * HEAD_PAD], bc, tc_len, None)
    u_l, q_l, ckv_l, kr_l = _pre(xl, m6, lat_row, row(norm1[l]), w_in_ext, row(q_norm[l]), row(kv_norm[l]),
                                 w_uq_ext, bl, tl_len, (cos, sin), pad_rows=True)

    rl = 2 * bl
    h0_c = jnp.zeros((2, 2, S5_BLOCKS_PER_HALF, 2, bc, LANES), F32)
    yf_c, yb_c, st_c = _s5_scan(u_c.reshape(tc_len * bc, SSM_WIDTH), bw, cw, a_blk, h0_c, bc, tc_len,
                                S5_TILE_ROWS // bc)
    h0_l = _s5_state_in(state_ssm[:, l])
    h0_l = jnp.stack([h0_l, jnp.zeros_like(h0_l)], axis=-2).reshape(h0_l.shape[:4] + (rl, LANES))
    yf_l, yb_l, _ = _s5_scan(u_l.reshape(tl_len * rl, SSM_WIDTH), bw, cw, a_blk, h0_l, rl, tl_len,
                             S5_TILE_ROWS // rl)

    at_c = _attention(q_c, ckv_c.reshape(bc, tc_len, KV_RANK), kr_c.reshape(bc, tc_len, HEAD_PAD),
                      w_uk_ext, w_uv_b, bc, tc_len)
    ckv_all = jnp.concatenate([cache_ckv[:, l], ckv_l.reshape(bl, tl_len, KV_RANK)], 1)
    kr_all = jnp.concatenate([_rope_slot(cache_krope[:, l]), kr_l.reshape(bl, tl_len, HEAD_PAD)], 1)
    at_l = _attention(q_l, ckv_all, kr_all, w_uk_ext, w_uv_b, bl, tl_len)

    tm_c = lambda a: a.reshape(tc_len, bc * SSM_WIDTH)
    tm_l = lambda a: a.reshape(tl_len, rl * SSM_WIDTH)
    ntok = bc * tc_len + bl * tl_len
    x1, h2, s1t, s2t = _post((xc, at_c, tm_c(u_c), tm_c(yf_c), tm_c(yb_c), bc, tc_len, 1),
                             (xl, at_l, u_l, tm_l(yf_l), tm_l(yb_l), bl, tl_len, 2),
                             m6, row(ssm_d[l]), w_glu_b, w_out_s, w_out_a, row(norm2[l]), w_query_b, sub_keys_b)

    dense = lambda a: a.reshape(PEER_HEADS, N_KEYS, ntok // LANES, LANES)
    stats, t2 = _route(dense(s1t), dense(s2t))
    ctx_peer_tiles = bc * tc_len // PEER_TN
    lat_peer_tiles = tl_len // PEER_TN
    peer_row = lambda t: jnp.where(t < ctx_peer_tiles, 0, 1 + (t - ctx_peer_tiles) // lat_peer_tiles)
    y_c, y_l = _peer(h2, s1t, s2t, t2, stats, u_tab, v_tab, x1, m6, peer_row, row(final_norm), bc * tc_len)

    y_prompt = y_c.reshape(bc, tc_len, D_MODEL)
    y_sample = y_l.reshape(bl, tl_len, D_MODEL)
    new_ckv = ckv_c.reshape(bc, 1, tc_len, KV_RANK)
    new_krope = kr_c[:, QK_NOPE:QK_NOPE + QK_ROPE].reshape(bc, 1, tc_len, QK_ROPE)
    new_ssm = _s5_state_out(st_c)[:, None]
    return (y_prompt, y_sample, new_ckv, new_krope, new_ssm)
```

```python
import functools
import math

import jax
import jax.numpy as jnp
import numpy as np
from jax import lax
from jax.experimental import pallas as pl
from jax.experimental.pallas import tpu as pltpu

F32 = jnp.float32
BF16 = jnp.bfloat16

D_MODEL = 1024
GRID_W = 64
EPS = 1e-6
SSM_WIDTH = 512
SSM_GROUP = 16
SSM_GROUPS = 32
SSM_STATE = 64
MLA_HEADS = 8
QK_NOPE = 64
QK_ROPE = 32
V_DIM = 64
Q_RANK = 384
KV_RANK = 256
ROPE_BASE = 10000.0
N_KEYS = 128
N_EXPERTS = N_KEYS * N_KEYS
PEER_HEADS = 8
PEER_TOPK = 16
KEY_DIM = 128

LANES = 128
HEAD_PAD = 128
TOK_TILE = 256
ATTN_TILE = 512
PRE_TILE = 512
S5_PAIR = 2
S5_BLOCKS = SSM_GROUPS // S5_PAIR
S5_HALF = SSM_WIDTH // 2
S5_BLOCKS_PER_HALF = S5_BLOCKS // 2
S5_TILE_ROWS = 1024
PEER_TN = 512
PEER_ROWS = 8
PEER_TE = PEER_ROWS * N_KEYS
GELU_C0 = math.sqrt(2.0 / math.pi)
GELU_C1 = 0.044715 * GELU_C0
VMEM_LIMIT = 54 * 1024 * 1024


def _cparams(sem):
    return pltpu.CompilerParams(dimension_semantics=sem, vmem_limit_bytes=VMEM_LIMIT)


def _rms(x, g):
    return x * lax.rsqrt(jnp.mean(x * x, axis=-1, keepdims=True) + EPS) * g


def _mod_kernel(c_ref, w_ref, b_ref, o_ref):
    o_ref[...] = jnp.dot(jax.nn.silu(c_ref[...]), w_ref[...], preferred_element_type=F32) + b_ref[...]


def _modulation(cvec, w_mod, b_mod):
    rows, d = cvec.shape
    n = w_mod.shape[1]
    tn = 1536
    return pl.pallas_call(
        _mod_kernel,
        grid=(n // tn,),
        in_specs=[pl.BlockSpec((rows, d), lambda j: (0, 0)),
                  pl.BlockSpec((d, tn), lambda j: (0, j)),
                  pl.BlockSpec((1, tn), lambda j: (0, j))],
        out_specs=pl.BlockSpec((rows, tn), lambda j: (0, j)),
        out_shape=jax.ShapeDtypeStruct((rows, n), F32),
        compiler_params=_cparams(("parallel",)),
        name="modulation",
    )(cvec, w_mod, b_mod)


def _pre_kernel(use_rope, pad_rows, x_ref, m_ref, n1_ref, win_ref, qn_ref, kvn_ref, wuq_ref, *rest):
    if use_rope:
        cos_ref, sin_ref, u_ref, q_ref, ckv_ref, kr_ref = rest
    else:
        u_ref, q_ref, ckv_ref, kr_ref = rest
    m = m_ref[0]
    h = _rms(x_ref[...], n1_ref[...]) * (1.0 + m[1:2]) + m[0:1]
    z = jnp.dot(h.astype(BF16), win_ref[...], preferred_element_type=F32)
    u_ref[:, :SSM_WIDTH] = z[:, :SSM_WIDTH]
    if pad_rows:
        u_ref[:, SSM_WIDTH:] = jnp.zeros((u_ref.shape[0], SSM_WIDTH), F32)
    o_q, o_kv, o_kr = SSM_WIDTH, SSM_WIDTH + Q_RANK, SSM_WIDTH + Q_RANK + KV_RANK
    cqn = _rms(z[:, o_q:o_kv], qn_ref[...])
    qq = jnp.dot(cqn.astype(BF16), wuq_ref[...], preferred_element_type=F32)
    ckv_ref[...] = _rms(z[:, o_kv:o_kr], kvn_ref[...])
    kr = z[:, o_kr:o_kr + HEAD_PAD]
    scale = (QK_NOPE + QK_ROPE) ** -0.5
    nq = MLA_HEADS * HEAD_PAD
    if use_rope:
        cos, sin = cos_ref[...], sin_ref[...]
        kr = kr * cos + z[:, o_kr + HEAD_PAD:o_kr + 2 * HEAD_PAD] * sin
        for hd in range(MLA_HEADS):
            sl = slice(hd * HEAD_PAD, (hd + 1) * HEAD_PAD)
            qh = qq[:, sl] * cos + qq[:, nq + hd * HEAD_PAD:nq + (hd + 1) * HEAD_PAD] * sin
            q_ref[:, sl] = (qh * scale).astype(BF16)
    else:
        q_ref[...] = (qq[:, :nq] * scale).astype(BF16)
    kr_ref[...] = kr


def _pre(x2d, m6, mrow, norm1, w_in_ext, q_norm, kv_norm, w_uq_ext, batch, seq, rope, pad_rows=False):
    ntok = batch * seq
    u_cols = SSM_WIDTH * (2 if pad_rows else 1)
    tile = min(seq, PRE_TILE)
    n_t = seq // tile
    use_rope = rope is not None
    full = lambda a: pl.BlockSpec(a.shape, lambda i: (0,) * a.ndim)
    in_specs = [pl.BlockSpec((tile, D_MODEL), lambda i: (i, 0)),
                pl.BlockSpec((1, 6, D_MODEL), lambda i: (mrow(i // n_t), 0, 0)),
                full(norm1), full(w_in_ext), full(q_norm), full(kv_norm), full(w_uq_ext)]
    args = [x2d, m6, norm1, w_in_ext, q_norm, kv_norm, w_uq_ext]
    if use_rope:
        in_specs += [pl.BlockSpec((tile, HEAD_PAD), lambda i: (i % n_t, 0))] * 2
        args += list(rope)
    out_specs = [pl.BlockSpec((tile, u_cols), lambda i: (i % n_t, i // n_t)),
                 pl.BlockSpec((tile, MLA_HEADS * HEAD_PAD), lambda i: (i, 0)),
                 pl.BlockSpec((tile, KV_RANK), lambda i: (i, 0)),
                 pl.BlockSpec((tile, HEAD_PAD), lambda i: (i, 0))]
    out_shape = [jax.ShapeDtypeStruct((seq, batch * u_cols), F32),
                 jax.ShapeDtypeStruct((ntok, MLA_HEADS * HEAD_PAD), BF16),
                 jax.ShapeDtypeStruct((ntok, KV_RANK), F32),
                 jax.ShapeDtypeStruct((ntok, HEAD_PAD), F32)]
    return pl.pallas_call(
        functools.partial(_pre_kernel, use_rope, pad_rows),
        grid=(ntok // tile,),
        in_specs=in_specs, out_specs=out_specs, out_shape=out_shape,
        compiler_params=_cparams(("parallel",)),
        name="pre_rope" if use_rope else "pre",
    )(*args)


def _s5_param_kernel(lr_ref, li_ref, ldt_ref, lrx_ref, lix_ref, ldtx_ref, bre_ref, bim_ref, cim_ref,
                     abr_ref, abi_ref, bfr_ref, bfi_ref, ncim_ref):
    def disc(lr, li, ldt):
        dt = jnp.exp(ldt)
        mag = jnp.exp(lr * dt)
        ab_re, ab_im = mag * jnp.cos(li * dt), mag * jnp.sin(li * dt)
        den = lr * lr + li * li
        br, bi = lr / den, -li / den
        ar = ab_re - 1.0
        return ab_re, ab_im, ar * br - ab_im * bi, ar * bi + ab_im * br

    ab_re, ab_im, _, _ = disc(lr_ref[...], li_ref[...], ldt_ref[...])
    abr_ref[...] = ab_re
    abi_ref[...] = ab_im
    _, _, f_re, f_im = disc(lrx_ref[...], lix_ref[...], ldtx_ref[...])
    b_re, b_im = bre_ref[...], bim_ref[...]
    bfr_ref[...] = f_re * b_re - f_im * b_im
    bfi_ref[...] = f_re * b_im + f_im * b_re
    ncim_ref[...] = -cim_ref[...]


def _s5_params(lam_re, lam_im, log_dt, b_re, b_im, c_re, c_im):
    dg = 2 * SSM_GROUPS
    n, p = SSM_STATE, SSM_GROUP
    lr = lam_re.reshape(dg, n)
    li = lam_im.reshape(dg, n)
    ldt = jnp.broadcast_to(log_dt.reshape(dg, 1), (dg, n))
    rep = lambda a: jnp.repeat(a, p, axis=1)
    args = [lr, li, ldt, rep(lr), rep(li), rep(ldt),
            b_re.reshape(dg, n * p), b_im.reshape(dg, n * p), c_im.reshape(dg, p * n)]
    small = jax.ShapeDtypeStruct((dg, n), F32)
    big = jax.ShapeDtypeStruct((dg, n * p), F32)
    ab_re, ab_im, bf_re, bf_im, ncim = pl.pallas_call(
        _s5_param_kernel, out_shape=[small, small, big, big, big], name="s5_params")(*args)

    nb, gp = S5_BLOCKS, S5_PAIR
    eye = jnp.eye(gp, dtype=F32)
    pos = np.arange(nb) % S5_BLOCKS_PER_HALF
    bf = jnp.stack([bf_re, bf_im], 0).reshape(2, 2, nb, gp, n, p)
    bc = jnp.einsum('adbgnp,gh->bdgpahn', bf, eye).reshape(nb, 2, gp * p, 2 * gp * n)
    bw = jnp.zeros((nb, 2, S5_BLOCKS_PER_HALF, gp * p, 2 * gp * n), F32)
    bw = bw.at[np.arange(nb), :, pos].set(bc).reshape(nb, 2, S5_HALF, 2 * gp * n)
    cc = jnp.stack([c_re.reshape(2, nb, gp, p, n), ncim.reshape(2, nb, gp, p, n)], 0)
    cc = jnp.einsum('adbgpn,gh->bdagnhp', cc, eye).reshape(nb, 2, 2 * gp * n, gp * p)
    cw = jnp.zeros((nb, 2, 2 * gp * n, S5_BLOCKS_PER_HALF, gp * p), F32)
    cw = cw.at[np.arange(nb), :, :, pos].set(cc).reshape(nb, 2, 2 * gp * n, S5_HALF)
    a = jnp.stack([ab_re, ab_im], 0).reshape(2, 2, nb, gp * n).transpose(2, 1, 0, 3)
    hb = S5_BLOCKS_PER_HALF
    bw = bw.reshape(2, hb, 2, S5_HALF, 2 * gp * n).transpose(0, 2, 3, 1, 4).reshape(2, 2, S5_HALF, -1)
    cw = cw.reshape(2, hb, 2, 2 * gp * n, S5_HALF).transpose(0, 2, 1, 3, 4).reshape(2, 2, -1, S5_HALF)
    a = a.reshape(2, hb, 2, 2, 1, gp * n).transpose(0, 2, 1, 3, 4, 5)
    return bw.astype(BF16), cw.astype(BF16), a


def _s5_state_in(h):
    bsz = h.shape[0]
    return h.reshape(bsz, 2, 2, 2, S5_BLOCKS_PER_HALF, LANES).transpose(3, 1, 4, 2, 0, 5)


def _s5_state_out(st):
    bsz = st.shape[4]
    return st.transpose(4, 1, 3, 0, 2, 5).reshape(bsz, 2, 2, SSM_GROUPS, SSM_STATE)


def _s5_kernel(rows_per_step, steps, uf_ref, ub_ref, bw_ref, cw_ref, a_ref, h0_ref,
               yf_ref, yb_ref, st_ref, buf, coef, carry):
    r = rows_per_step
    nb = S5_BLOCKS_PER_HALF
    width = 2 * LANES

    @pl.when(pl.program_id(1) == 0)
    def _():
        carry[...] = h0_ref[0]
        coef[...] = jnp.broadcast_to(a_ref[0], coef.shape)

    for d, u_ref in enumerate((uf_ref, ub_ref)):
        buf[d] = jnp.dot(u_ref[...].astype(BF16), bw_ref[0, d], preferred_element_type=F32)

    chains = [(d, b) for d in range(2) for b in range(nb)]
    for g in range(r // 8):
        rows8 = slice(g * 8, (g + 1) * 8)

        def body(k, hs, rows8=rows8):
            out = []
            for (d, b), (h_re, h_im) in zip(chains, hs):
                t = k if d == 0 else steps - 1 - k
                r0 = pl.multiple_of(t * r + g * 8, 8)
                a_re, a_im = coef[d, b, 0], coef[d, b, 1]
                re_l, im_l = slice(b * width, b * width + LANES), slice(b * width + LANES, (b + 1) * width)
                n_re = a_re * h_re - a_im * h_im + buf[d, pl.ds(r0, 8), re_l]
                n_im = a_re * h_im + a_im * h_re + buf[d, pl.ds(r0, 8), im_l]
                buf[d, pl.ds(r0, 8), re_l] = n_re
                buf[d, pl.ds(r0, 8), im_l] = n_im
                out.append((n_re, n_im))
            return tuple(out)

        init = tuple((carry[d, b, 0, rows8], carry[d, b, 1, rows8]) for d, b in chains)
        final = lax.fori_loop(0, steps, body, init)
        for (d, b), (h_re, h_im) in zip(chains, final):
            carry[d, b, 0, rows8] = h_re
            carry[d, b, 1, rows8] = h_im

    for d, y_ref in enumerate((yf_ref, yb_ref)):
        y_ref[...] = jnp.dot(buf[d].astype(BF16), cw_ref[0, d], preferred_element_type=F32)
    st_ref[0] = carry[...]


def _s5_scan(u_tm, bw, cw, a, h0, rows_per_step, seq, steps):
    r = rows_per_step
    n_t = seq // steps
    rows = steps * r
    nb = S5_BLOCKS_PER_HALF
    wide = nb * 2 * LANES
    half4 = lambda h, t: (h, 0, 0, 0)
    half6 = lambda h, t: (h, 0, 0, 0, 0, 0)
    state = (2, nb, 2, r, LANES)
    in_specs = [pl.BlockSpec((rows, S5_HALF), lambda h, t: (t, h)),
                pl.BlockSpec((rows, S5_HALF), lambda h, t: (n_t - 1 - t, h)),
                pl.BlockSpec((1, 2, S5_HALF, wide), half4),
                pl.BlockSpec((1, 2, wide, S5_HALF), half4),
                pl.BlockSpec((1, 2, nb, 2, 1, LANES), half6),
                pl.BlockSpec((1,) + state, half6)]
    out_specs = [pl.BlockSpec((rows, S5_HALF), lambda h, t: (t, h)),
                 pl.BlockSpec((rows, S5_HALF), lambda h, t: (n_t - 1 - t, h)),
                 pl.BlockSpec((1,) + state, half6)]
    out_shape = [jax.ShapeDtypeStruct(u_tm.shape, F32), jax.ShapeDtypeStruct(u_tm.shape, F32),
                 jax.ShapeDtypeStruct((2,) + state, F32)]
    return pl.pallas_call(
        functools.partial(_s5_kernel, r, steps),
        grid=(2, n_t),
        in_specs=in_specs, out_specs=out_specs, out_shape=out_shape,
        scratch_shapes=[pltpu.VMEM((2, rows, wide), F32), pltpu.VMEM((2, nb, 2, 8, LANES), F32),
                        pltpu.VMEM(state, F32)],
        compiler_params=_cparams(("arbitrary", "arbitrary")),
        name="s5_scan",
    )(u_tm, u_tm, bw, cw, a, h0)


def _attn_kernel(q_ref, ckv_ref, kr_ref, wuk_ref, wuv_ref, o_ref, k_s, v_s):
    @pl.when(pl.program_id(1) == 0)
    def _():
        kv = ckv_ref[0].astype(BF16)
        kn = jnp.dot(kv, wuk_ref[...], preferred_element_type=F32)
        kr = kr_ref[0]
        for hd in range(MLA_HEADS):
            k_s[hd] = (kn[:, hd * HEAD_PAD:(hd + 1) * HEAD_PAD] + kr).astype(BF16)
        v_s[...] = jnp.dot(kv, wuv_ref[...], preferred_element_type=F32).astype(BF16)

    for hd in range(MLA_HEADS):
        qh = q_ref[:, hd * HEAD_PAD:(hd + 1) * HEAD_PAD]
        s = lax.dot_general(qh, k_s[hd], (((1,), (1,)), ((), ())), preferred_element_type=F32)
        p = jnp.exp(s - jnp.max(s, axis=-1, keepdims=True))
        l = jnp.sum(p, axis=-1, keepdims=True)
        o = jnp.dot(p.astype(BF16), v_s[:, hd * V_DIM:(hd + 1) * V_DIM], preferred_element_type=F32)
        o_ref[:, hd * V_DIM:(hd + 1) * V_DIM] = (o / l).astype(BF16)


def _attention(q, ckv_all, kr_all, w_uk_ext, w_uv, batch, seq):
    s_len = ckv_all.shape[1]
    tq = min(seq, ATTN_TILE)
    n_q = seq // tq
    return pl.pallas_call(
        _attn_kernel,
        grid=(batch, n_q),
        in_specs=[pl.BlockSpec((tq, MLA_HEADS * HEAD_PAD), lambda b, i: (b * n_q + i, 0)),
                  pl.BlockSpec((1, s_len, KV_RANK), lambda b, i: (b, 0, 0)),
                  pl.BlockSpec((1, s_len, HEAD_PAD), lambda b, i: (b, 0, 0)),
                  pl.BlockSpec(w_uk_ext.shape, lambda b, i: (0, 0)),
                  pl.BlockSpec(w_uv.shape, lambda b, i: (0, 0))],
        out_specs=pl.BlockSpec((tq, MLA_HEADS * V_DIM), lambda b, i: (b * n_q + i, 0)),
        out_shape=jax.ShapeDtypeStruct((batch * seq, MLA_HEADS * V_DIM), BF16),
        scratch_shapes=[pltpu.VMEM((MLA_HEADS, s_len, HEAD_PAD), BF16),
                        pltpu.VMEM((s_len, MLA_HEADS * V_DIM), BF16)],
        compiler_params=_cparams(("parallel", "arbitrary")),
        name="attention",
    )(q, ckv_all, kr_all, w_uk_ext, w_uv)


def _post_kernel(n_ctx, xc_ref, atc_ref, uc_ref, yfc_ref, ybc_ref, xl_ref, atl_ref, ul_ref, yfl_ref, ybl_ref,
                 m_ref, d_ref, wglu_ref, wos_ref, woa_ref, n2_ref, wq_ref, sk_ref,
                 x1_ref, h2_ref, s1_ref, s2_ref):
    is_ctx = pl.program_id(0) < n_ctx
    pick = lambda c_ref, l_ref: jnp.where(is_ctx, c_ref[...], l_ref[...])
    m = m_ref[0]
    y = pick(yfc_ref, yfl_ref) + pick(ybc_ref, ybl_ref) + pick(uc_ref, ul_ref) * d_ref[...]
    yg = jax.nn.gelu(y)
    gate = jax.nn.sigmoid(jnp.dot(yg.astype(BF16), wglu_ref[...], preferred_element_type=F32))
    mix = (jnp.dot((yg * gate).astype(BF16), wos_ref[...], preferred_element_type=F32)
           + jnp.dot(pick(atc_ref, atl_ref), woa_ref[...], preferred_element_type=F32))
    x1 = pick(xc_ref, xl_ref) + m[2:3] * mix
    x1_ref[...] = x1
    h2 = (_rms(x1, n2_ref[...]) * (1.0 + m[4:5]) + m[3:4]).astype(BF16)
    h2_ref[...] = h2
    qp = jnp.dot(h2, wq_ref[...], preferred_element_type=F32).astype(BF16)
    for hd in range(PEER_HEADS):
        for half, s_ref in enumerate((s1_ref, s2_ref)):
            c0 = (hd * 2 + half) * KEY_DIM
            s_ref[hd] = lax.dot_general(sk_ref[hd, half], qp[:, c0:c0 + KEY_DIM],
                                        (((1,), (1,)), ((), ())), preferred_element_type=F32)


def _post(ctx, lat, m6, ssm_d, w_glu, w_out_s, w_out_a, norm2, w_query, sub_keys):
    specs, args = [], []
    n_ctx, n_lat = (p[5] * p[6] // TOK_TILE for p in (ctx, lat))
    for pass_id, (x2d, attn, u_tm, yf, yb, batch, seq, stride) in enumerate((ctx, lat)):
        n_t = seq // TOK_TILE
        loc = (lambda i: jnp.minimum(i, n_ctx - 1)) if pass_id == 0 else (lambda i: jnp.maximum(i - n_ctx, 0))
        tok = lambda w, loc=loc: pl.BlockSpec((TOK_TILE, w), lambda i: (loc(i), 0))
        tm = pl.BlockSpec((TOK_TILE, SSM_WIDTH),
                          lambda i, loc=loc, n_t=n_t, stride=stride: (loc(i) % n_t, (loc(i) // n_t) * stride))
        specs += [tok(D_MODEL), tok(MLA_HEADS * V_DIM), tm, tm, tm]
        args += [x2d, attn, u_tm, yf, yb]
    lat_tiles_per_row = lat[6] // TOK_TILE
    mrow = lambda i: jnp.where(i < n_ctx, 0, 1 + (i - n_ctx) // lat_tiles_per_row)
    full = lambda a: pl.BlockSpec(a.shape, lambda i: (0,) * a.ndim)
    weights = [ssm_d, w_glu, w_out_s, w_out_a, norm2, w_query, sub_keys]
    total = (n_ctx + n_lat) * TOK_TILE
    tok_out = lambda w: pl.BlockSpec((TOK_TILE, w), lambda i: (i, 0))
    sc = pl.BlockSpec((PEER_HEADS, N_KEYS, TOK_TILE), lambda i: (0, 0, i))
    return pl.pallas_call(
        functools.partial(_post_kernel, n_ctx),
        grid=(n_ctx + n_lat,),
        in_specs=specs + [pl.BlockSpec((1, 6, D_MODEL), lambda i: (mrow(i), 0, 0))] + [full(w) for w in weights],
        out_specs=[tok_out(D_MODEL), tok_out(D_MODEL), sc, sc],
        out_shape=[jax.ShapeDtypeStruct((total, D_MODEL), F32), jax.ShapeDtypeStruct((total, D_MODEL), BF16),
                   jax.ShapeDtypeStruct((PEER_HEADS, N_KEYS, total), F32),
                   jax.ShapeDtypeStruct((PEER_HEADS, N_KEYS, total), F32)],
        compiler_params=_cparams(("arbitrary",)),
        name="post",
    )(*args, m6, *weights)


def _sort_pairs(lo, hi):
    def merge(lo, hi, r):
        step = r * 2
        if step < hi - lo:
            yield from merge(lo, hi, step)
            yield from merge(lo + r, hi, step)
            for i in range(lo + r, hi - r, step):
                yield (i, i + r)
        else:
            yield (lo, lo + r)

    if hi - lo >= 1:
        mid = lo + (hi - lo) // 2
        yield from _sort_pairs(lo, mid)
        yield from _sort_pairs(mid + 1, hi)
        yield from merge(lo, hi, 1)


def _sort_desc(vals):
    vals = list(vals)
    for i, j in _sort_pairs(0, len(vals) - 1):
        vals[i], vals[j] = jnp.maximum(vals[i], vals[j]), jnp.minimum(vals[i], vals[j])
    return vals


def _merge_top(a, b):
    n = len(a)
    c = [jnp.maximum(a[k], b[n - 1 - k]) for k in range(n)]
    stride = n // 2
    while stride:
        for i in range(n):
            if not i & stride:
                c[i], c[i + stride] = jnp.maximum(c[i], c[i + stride]), jnp.minimum(c[i], c[i + stride])
        stride //= 2
    return c


def _top_keys(ref):
    k = PEER_TOPK
    groups = [_sort_desc([ref[0, g * k + r] for r in range(k)]) for g in range(N_KEYS // k)]
    while len(groups) > 1:
        groups = [_merge_top(groups[i], groups[i + 1]) for i in range(0, len(groups), 2)]
    return groups[0]


def _route_kernel(s1_ref, s2_ref, o_ref, t2_ref):
    k = PEER_TOPK
    t1, t2 = _top_keys(s1_ref), _top_keys(s2_ref)
    cands = [t1[a] + t2[b] for a in range(k) for b in range(k) if (a + 1) * (b + 1) <= k]
    size = 1 << (len(cands) - 1).bit_length()
    cands += [jnp.full_like(t1[0], -jnp.inf)] * (size - len(cands))
    best = _sort_desc(cands)
    z = jnp.ones_like(best[0])
    for v in best[1:k]:
        z = z + jnp.exp(v - best[0])
    stats = (best[k - 1], t1[0], t2[0], 1.0 / z)
    for c in range(8):
        lanes = slice(c * LANES, (c + 1) * LANES)
        for n, v in enumerate(stats):
            o_ref[n, 0, :, lanes] = v[c:c + 1]
        for b in range(k):
            t2_ref[0, b:b + 1, lanes] = t2[b][c:c + 1]


def _route(s1d, s2d):
    n_rows = s1d.shape[2]
    ntok = n_rows * LANES
    blk = 8 * LANES
    spec = pl.BlockSpec((1, N_KEYS, 8, LANES), lambda i, h: (h, 0, i, 0))
    return pl.pallas_call(
        _route_kernel,
        grid=(n_rows // 8, PEER_HEADS),
        in_specs=[spec, spec],
        out_specs=[pl.BlockSpec((4, 1, 1, blk), lambda i, h: (0, h, 0, i)),
                   pl.BlockSpec((1, PEER_TOPK, blk), lambda i, h: (h, 0, i))],
        out_shape=[jax.ShapeDtypeStruct((4, PEER_HEADS, 1, ntok), F32),
                   jax.ShapeDtypeStruct((PEER_HEADS, PEER_TOPK, ntok), F32)],
        compiler_params=_cparams(("parallel", "parallel")),
        name="peer_route",
    )(s1d, s2d)


def _peer_kernel(h2_ref, s1_ref, s2_ref, t2_ref, st_ref, u_ref, v_ref, x1_ref, m_ref, fn_ref, yc_ref, yl_ref,
                 e2_s, act_s, aw_s, acc_s, *, ctx_tiles):
    k = pl.program_id(1)
    chunks = [slice(c * LANES, (c + 1) * LANES) for c in range(PEER_TN // LANES)]

    @pl.when(k == 0)
    def _():
        for hd in range(PEER_HEADS):
            e2_s[hd] = jnp.exp(s2_ref[hd] - st_ref[2, hd])
        acc_s[...] = jnp.zeros_like(acc_s)

    floor, gain = [], []
    for hd in range(PEER_HEADS):
        s1 = s1_ref[hd]
        low = jnp.full_like(s1, jnp.inf)
        for b in range(PEER_TOPK):
            t2b = t2_ref[hd, b:b + 1, :]
            low = jnp.where(s1 + t2b >= st_ref[0, hd], t2b, low)
        floor.append(low)
        gain.append(jnp.exp(s1 - st_ref[1, hd]) * (0.5 * st_ref[3, hd]))

    act_s[...] = lax.dot_general(u_ref[...].astype(BF16), h2_ref[...], (((1,), (1,)), ((), ())),
                                 preferred_element_type=F32)
    for ii in range(PEER_ROWS):
        rows = slice(ii * N_KEYS, (ii + 1) * N_KEYS)
        for cs in chunks:
            w = None
            for hd in range(PEER_HEADS):
                hit = s2_ref[hd, :, cs] >= floor[hd][ii:ii + 1, cs]
                term = jnp.where(hit, e2_s[hd, :, cs], 0.0) * gain[hd][ii:ii + 1, cs]
                w = term if w is None else w + term
            x = act_s[rows, cs]
            t = jnp.tanh(x * (GELU_C0 + GELU_C1 * (x * x)))
            aw_s[rows, cs] = ((x * w) * (1.0 + t)).astype(BF16)
    acc_s[...] += lax.dot_general(aw_s[...], v_ref[...].astype(BF16), (((0,), (0,)), ((), ())),
                                  preferred_element_type=F32)

    last = k == pl.num_programs(1) - 1
    is_ctx = pl.program_id(0) < ctx_tiles
    for cond, y_ref in ((is_ctx, yc_ref), (jnp.logical_not(is_ctx), yl_ref)):
        @pl.when(last & cond)
        def _(y_ref=y_ref):
            y_ref[...] = _rms(x1_ref[...] + m_ref[0][5:6] * acc_s[...], fn_ref[...])


def _peer(h2, s1t, s2t, t2, stats, u_tab, v_tab, x1, m6, mrow, final_norm, ctx_tokens):
    ntok = h2.shape[0]
    tn, te = PEER_TN, PEER_TE
    nc = ctx_tokens // tn
    tok = pl.BlockSpec((tn, D_MODEL), lambda t, k: (t, 0))
    tab = pl.BlockSpec((te, D_MODEL), lambda t, k: (k, 0))
    keys = pltpu.VMEM((PEER_HEADS, N_KEYS, tn), F32)
    return pl.pallas_call(
        functools.partial(_peer_kernel, ctx_tiles=nc),
        grid=(ntok // tn, N_EXPERTS // te),
        in_specs=[tok,
                  pl.BlockSpec((PEER_HEADS, PEER_ROWS, tn), lambda t, k: (0, k, t)),
                  pl.BlockSpec((PEER_HEADS, N_KEYS, tn), lambda t, k: (0, 0, t)),
                  pl.BlockSpec((PEER_HEADS, PEER_TOPK, tn), lambda t, k: (0, 0, t)),
                  pl.BlockSpec((4, PEER_HEADS, 1, tn), lambda t, k: (0, 0, 0, t)),
                  tab, tab, tok,
                  pl.BlockSpec((1, 6, D_MODEL), lambda t, k: (mrow(t), 0, 0)),
                  pl.BlockSpec((1, D_MODEL), lambda t, k: (0, 0))],
        out_specs=[pl.BlockSpec((tn, D_MODEL), lambda t, k: (jnp.minimum(t, nc - 1), 0)),
                   pl.BlockSpec((tn, D_MODEL), lambda t, k: (jnp.maximum(t - nc, 0), 0))],
        out_shape=[jax.ShapeDtypeStruct((ctx_tokens, D_MODEL), F32),
                   jax.ShapeDtypeStruct((ntok - ctx_tokens, D_MODEL), F32)],
        scratch_shapes=[keys, pltpu.VMEM((te, tn), F32), pltpu.VMEM((te, tn), BF16),
                        pltpu.VMEM((tn, D_MODEL), F32)],
        compiler_params=_cparams(("arbitrary", "arbitrary")),
        name="peer_dense",
    )(h2, s1t, s2t, t2, stats, u_tab, v_tab, x1, m6, final_norm)


def _rot_cols(w):
    j = np.arange(QK_ROPE)
    first = (j % (QK_ROPE // 2)) < (QK_ROPE // 4)
    perm = np.where(first, j + QK_ROPE // 4, j - QK_ROPE // 4)
    sign = np.where(first, -1.0, 1.0).astype(np.float32)
    return w[..., perm] * sign


def _rope_slot(w_rope):
    pad = [(0, 0)] * (w_rope.ndim - 1)
    return jnp.pad(w_rope, pad + [(QK_NOPE, HEAD_PAD - QK_NOPE - QK_ROPE)])


def _rope_tables(n_tokens):
    n_rows = n_tokens // GRID_W
    rows = jnp.repeat(jnp.arange(n_rows, dtype=F32), GRID_W)
    cols = jnp.tile(jnp.arange(GRID_W, dtype=F32), n_rows)
    half = QK_ROPE // 2
    inv_freq = 1.0 / (ROPE_BASE ** (jnp.arange(0, half, 2, dtype=F32) / half))
    ang_r = rows[:, None] * inv_freq
    ang_c = cols[:, None] * inv_freq
    ang = jnp.concatenate([ang_r, ang_r, ang_c, ang_c], axis=-1)
    lead = (QK_NOPE, HEAD_PAD - QK_NOPE - QK_ROPE)
    cos = jnp.pad(jnp.cos(ang), [(0, 0), lead], constant_values=1.0)
    sin = jnp.pad(jnp.sin(ang), [(0, 0), lead])
    return cos, sin


def kernel(x_prompt, x_sample, c, cache_ckv, cache_krope, state_ssm, c_ctx, w_mod, b_mod, norm1, w_in,
           ssm_lam_re, ssm_lam_im, ssm_log_dt, ssm_b_re, ssm_b_im, ssm_c_re, ssm_c_im, ssm_d, w_glu,
           q_norm, w_uq, kv_norm, w_uk, w_uv, w_out, norm2, w_query, sub_keys, u_table, v_table,
           final_norm):
    bc, tc_len, _ = x_prompt.shape
    bl, tl_len, _ = x_sample.shape
    l = 0
    row = lambda a: a.reshape(1, -1)
    assert w_mod.shape[0] == 1
    big = lambda a: a.reshape(a.shape[1:])

    n_mod = 8
    cvec = jnp.concatenate([c_ctx[None], c, jnp.zeros((n_mod - 1 - bl, D_MODEL), F32)], 0)
    m6 = _modulation(cvec, big(w_mod), row(b_mod[l])).reshape(n_mod, 6, D_MODEL)

    wi = w_in[l]
    o_kr = SSM_WIDTH + Q_RANK + KV_RANK
    w_kr = wi[:, o_kr:]
    w_in_ext = jnp.concatenate([wi[:, :o_kr], _rope_slot(w_kr), _rope_slot(_rot_cols(w_kr))], 1).astype(BF16)
    wq3 = w_uq[l].reshape(Q_RANK, MLA_HEADS, QK_NOPE + QK_ROPE)
    wq_main = jnp.pad(wq3, [(0, 0), (0, 0), (0, HEAD_PAD - QK_NOPE - QK_ROPE)])
    wq_rot = _rope_slot(_rot_cols(wq3[..., QK_NOPE:]))
    w_uq_ext = jnp.concatenate([wq_main.reshape(Q_RANK, -1), wq_rot.reshape(Q_RANK, -1)], 1).astype(BF16)
    w_uk_ext = jnp.pad(w_uk[l].reshape(KV_RANK, MLA_HEADS, QK_NOPE),
                       [(0, 0), (0, 0), (0, HEAD_PAD - QK_NOPE)]).reshape(KV_RANK, -1).astype(BF16)
    w_uv_b = w_uv[l].astype(BF16)
    w_glu_b = w_glu[l].astype(BF16)
    w_out_s = w_out[l][:SSM_WIDTH].astype(BF16)
    w_out_a = w_out[l][SSM_WIDTH:].astype(BF16)
    w_query_b = w_query[l].astype(BF16)
    sub_keys_b = sub_keys[l].astype(BF16)
    u_tab, v_tab = big(u_table), big(v_table)
    cos, sin = _rope_tables(tl_len)

    bw, cw, a_blk = _s5_params(ssm_lam_re[l], ssm_lam_im[l], ssm_log_dt[l], ssm_b_re[l], ssm_b_im[l],
                               ssm_c_re[l], ssm_c_im[l])

    ctx_row = lambda b: 0
    lat_row = lambda b: 1 + b
    xc = x_prompt.reshape(bc * tc_len, D_MODEL)
    xl = x_sample.reshape(bl * tl_len, D_MODEL)

    u_c, q_c, ckv_c, kr_c = _pre(xc, m6, ctx_row, row(norm1[l]), w_in_ext, row(q_norm[l]), row(kv_norm[l]),
                                 w_uq_ext[:, :MLA_HEADS * HEAD_PAD], bc, tc_len, None)
    u_l, q_l, ckv_l, kr_l = _pre(xl, m6, lat_row, row(norm1[l]), w_in_ext, row(q_norm[l]), row(kv_norm[l]),
                                 w_uq_ext, bl, tl_len, (cos, sin), pad_rows=True)

    rl = 2 * bl
    h0_c = jnp.zeros((2, 2, S5_BLOCKS_PER_HALF, 2, bc, LANES), F32)
    yf_c, yb_c, st_c = _s5_scan(u_c.reshape(tc_len * bc, SSM_WIDTH), bw, cw, a_blk, h0_c, bc, tc_len,
                                S5_TILE_ROWS // bc)
    h0_l = _s5_state_in(state_ssm[:, l])
    h0_l = jnp.stack([h0_l, jnp.zeros_like(h0_l)], axis=-2).reshape(h0_l.shape[:4] + (rl, LANES))
    yf_l, yb_l, _ = _s5_scan(u_l.reshape(tl_len * rl, SSM_WIDTH), bw, cw, a_blk, h0_l, rl, tl_len,
                             S5_TILE_ROWS // rl)

    at_c = _attention(q_c, ckv_c.reshape(bc, tc_len, KV_RANK), kr_c.reshape(bc, tc_len, HEAD_PAD),
                      w_uk_ext, w_uv_b, bc, tc_len)
    ckv_all = jnp.concatenate([cache_ckv[:, l], ckv_l.reshape(bl, tl_len, KV_RANK)], 1)
    kr_all = jnp.concatenate([_rope_slot(cache_krope[:, l]), kr_l.reshape(bl, tl_len, HEAD_PAD)], 1)
    at_l = _attention(q_l, ckv_all, kr_all, w_uk_ext, w_uv_b, bl, tl_len)

    tm_c = lambda a: a.reshape(tc_len, bc * SSM_WIDTH)
    tm_l = lambda a: a.reshape(tl_len, rl * SSM_WIDTH)
    ntok = bc * tc_len + bl * tl_len
    x1, h2, s1t, s2t = _post((xc, at_c, tm_c(u_c), tm_c(yf_c), tm_c(yb_c), bc, tc_len, 1),
                             (xl, at_l, u_l, tm_l(yf_l), tm_l(yb_l), bl, tl_len, 2),
                             m6, row(ssm_d[l]), w_glu_b, w_out_s, w_out_a, row(norm2[l]), w_query_b, sub_keys_b)

    dense = lambda a: a.reshape(PEER_HEADS, N_KEYS, ntok // LANES, LANES)
    stats, t2 = _route(dense(s1t), dense(s2t))
    ctx_peer_tiles = bc * tc_len // PEER_TN
    lat_peer_tiles = tl_len // PEER_TN
    peer_row = lambda t: jnp.where(t < ctx_peer_tiles, 0, 1 + (t - ctx_peer_tiles) // lat_peer_tiles)
    y_c, y_l = _peer(h2, s1t, s2t, t2, stats, u_tab, v_tab, x1, m6, peer_row, row(final_norm), bc * tc_len)

    y_prompt = y_c.reshape(bc, tc_len, D_MODEL)
    y_sample = y_l.reshape(bl, tl_len, D_MODEL)
    new_ckv = ckv_c.reshape(bc, 1, tc_len, KV_RANK)
    new_krope = kr_c[:, QK_NOPE:QK_NOPE + QK_ROPE].reshape(bc, 1, tc_len, QK_ROPE)
    new_ssm = _s5_state_out(st_c)[:, None]
    return (y_prompt, y_sample, new_ckv, new_krope, new_ssm)
```

```python
import functools
import math

import jax
import jax.numpy as jnp
import numpy as np
from jax import lax
from jax.experimental import pallas as pl
from jax.experimental.pallas import tpu as pltpu

F32 = jnp.float32
BF16 = jnp.bfloat16

D_MODEL = 1024
GRID_W = 64
EPS = 1e-6
SSM_WIDTH = 512
SSM_GROUP = 16
SSM_GROUPS = 32
SSM_STATE = 64
MLA_HEADS = 8
QK_NOPE = 64
QK_ROPE = 32
V_DIM = 64
Q_RANK = 384
KV_RANK = 256
ROPE_BASE = 10000.0
N_KEYS = 128
N_EXPERTS = N_KEYS * N_KEYS
PEER_HEADS = 8
PEER_TOPK = 16
KEY_DIM = 128

LANES = 128
HEAD_PAD = 128
TOK_TILE = 256
ATTN_TILE = 512
PRE_TILE = 512
S5_PAIR = 2
S5_BLOCKS = SSM_GROUPS // S5_PAIR
S5_HALF = SSM_WIDTH // 2
S5_BLOCKS_PER_HALF = S5_BLOCKS // 2
S5_TILE_ROWS = 1024
PEER_TN = 512
PEER_ROWS = 8
PEER_TE = PEER_ROWS * N_KEYS
GELU_C0 = math.sqrt(2.0 / math.pi)
GELU_C1 = 0.044715 * GELU_C0
VMEM_LIMIT = 54 * 1024 * 1024


def _cparams(sem):
    return pltpu.CompilerParams(dimension_semantics=sem, vmem_limit_bytes=VMEM_LIMIT)


def _rms(x, g):
    return x * lax.rsqrt(jnp.mean(x * x, axis=-1, keepdims=True) + EPS) * g


def _mod_kernel(c_ref, w_ref, b_ref, o_ref):
    o_ref[...] = jnp.dot(jax.nn.silu(c_ref[...]), w_ref[...], preferred_element_type=F32) + b_ref[...]


def _modulation(cvec, w_mod, b_mod):
    rows, d = cvec.shape
    n = w_mod.shape[1]
    tn = 1536
    return pl.pallas_call(
        _mod_kernel,
        grid=(n // tn,),
        in_specs=[pl.BlockSpec((rows, d), lambda j: (0, 0)),
                  pl.BlockSpec((d, tn), lambda j: (0, j)),
                  pl.BlockSpec((1, tn), lambda j: (0, j))],
        out_specs=pl.BlockSpec((rows, tn), lambda j: (0, j)),
        out_shape=jax.ShapeDtypeStruct((rows, n), F32),
        compiler_params=_cparams(("parallel",)),
        name="modulation",
    )(cvec, w_mod, b_mod)


def _pre_kernel(use_rope, pad_rows, x_ref, m_ref, n1_ref, win_ref, qn_ref, kvn_ref, wuq_ref, *rest):
    if use_rope:
        cos_ref, sin_ref, u_ref, q_ref, ckv_ref, kr_ref = rest
    else:
        u_ref, q_ref, ckv_ref, kr_ref = rest
    m = m_ref[0]
    h = _rms(x_ref[...], n1_ref[...]) * (1.0 + m[1:2]) + m[0:1]
    z = jnp.dot(h.astype(BF16), win_ref[...], preferred_element_type=F32)
    u_ref[:, :SSM_WIDTH] = z[:, :SSM_WIDTH]
    if pad_rows:
        u_ref[:, SSM_WIDTH:] = jnp.zeros((u_ref.shape[0], SSM_WIDTH), F32)
    o_q, o_kv, o_kr = SSM_WIDTH, SSM_WIDTH + Q_RANK, SSM_WIDTH + Q_RANK + KV_RANK
    cqn = _rms(z[:, o_q:o_kv], qn_ref[...])
    qq = jnp.dot(cqn.astype(BF16), wuq_ref[...], preferred_element_type=F32)
    ckv_ref[...] = _rms(z[:, o_kv:o_kr], kvn_ref[...])
    kr = z[:, o_kr:o_kr + HEAD_PAD]
    scale = (QK_NOPE + QK_ROPE) ** -0.5
    nq = MLA_HEADS * HEAD_PAD
    if use_rope:
        cos, sin = cos_ref[...], sin_ref[...]
        kr = kr * cos + z[:, o_kr + HEAD_PAD:o_kr + 2 * HEAD_PAD] * sin
        for hd in range(MLA_HEADS):
            sl = slice(hd * HEAD_PAD, (hd + 1) * HEAD_PAD)
            qh = qq[:, sl] * cos + qq[:, nq + hd * HEAD_PAD:nq + (hd + 1) * HEAD_PAD] * sin
            q_ref[:, sl] = (qh * scale).astype(BF16)
    else:
        q_ref[...] = (qq[:, :nq] * scale).astype(BF16)
    kr_ref[...] = kr


def _pre(x2d, m6, mrow, norm1, w_in_ext, q_norm, kv_norm, w_uq_ext, batch, seq, rope, pad_rows=False):
    ntok = batch * seq
    u_cols = SSM_WIDTH * (2 if pad_rows else 1)
    tile = min(seq, PRE_TILE)
    n_t = seq // tile
    use_rope = rope is not None
    full = lambda a: pl.BlockSpec(a.shape, lambda i: (0,) * a.ndim)
    in_specs = [pl.BlockSpec((tile, D_MODEL), lambda i: (i, 0)),
                pl.BlockSpec((1, 6, D_MODEL), lambda i: (mrow(i // n_t), 0, 0)),
                full(norm1), full(w_in_ext), full(q_norm), full(kv_norm), full(w_uq_ext)]
    args = [x2d, m6, norm1, w_in_ext, q_norm, kv_norm, w_uq_ext]
    if use_rope:
        in_specs += [pl.BlockSpec((tile, HEAD_PAD), lambda i: (i % n_t, 0))] * 2
        args += list(rope)
    out_specs = [pl.BlockSpec((tile, u_cols), lambda i: (i % n_t, i // n_t)),
                 pl.BlockSpec((tile, MLA_HEADS * HEAD_PAD), lambda i: (i, 0)),
                 pl.BlockSpec((tile, KV_RANK), lambda i: (i, 0)),
                 pl.BlockSpec((tile, HEAD_PAD), lambda i: (i, 0))]
    out_shape = [jax.ShapeDtypeStruct((seq, batch * u_cols), F32),
                 jax.ShapeDtypeStruct((ntok, MLA_HEADS * HEAD_PAD), BF16),
                 jax.ShapeDtypeStruct((ntok, KV_RANK), F32),
                 jax.ShapeDtypeStruct((ntok, HEAD_PAD), F32)]
    return pl.pallas_call(
        functools.partial(_pre_kernel, use_rope, pad_rows),
        grid=(ntok // tile,),
        in_specs=in_specs, out_specs=out_specs, out_shape=out_shape,
        compiler_params=_cparams(("parallel",)),
        name="pre_rope" if use_rope else "pre",
    )(*args)


def _s5_param_kernel(lr_ref, li_ref, ldt_ref, lrx_ref, lix_ref, ldtx_ref, bre_ref, bim_ref, cim_ref,
                     abr_ref, abi_ref, bfr_ref, bfi_ref, ncim_ref):
    def disc(lr, li, ldt):
        dt = jnp.exp(ldt)
        mag = jnp.exp(lr * dt)
        ab_re, ab_im = mag * jnp.cos(li * dt), mag * jnp.sin(li * dt)
        den = lr * lr + li * li
        br, bi = lr / den, -li / den
        ar = ab_re - 1.0
        return ab_re, ab_im, ar * br - ab_im * bi, ar * bi + ab_im * br

    ab_re, ab_im, _, _ = disc(lr_ref[...], li_ref[...], ldt_ref[...])
    abr_ref[...] = ab_re
    abi_ref[...] = ab_im
    _, _, f_re, f_im = disc(lrx_ref[...], lix_ref[...], ldtx_ref[...])
    b_re, b_im = bre_ref[...], bim_ref[...]
    bfr_ref[...] = f_re * b_re - f_im * b_im
    bfi_ref[...] = f_re * b_im + f_im * b_re
    ncim_ref[...] = -cim_ref[...]


def _s5_params(lam_re, lam_im, log_dt, b_re, b_im, c_re, c_im):
    dg = 2 * SSM_GROUPS
    n, p = SSM_STATE, SSM_GROUP
    lr = lam_re.reshape(dg, n)
    li = lam_im.reshape(dg, n)
    ldt = jnp.broadcast_to(log_dt.reshape(dg, 1), (dg, n))
    rep = lambda a: jnp.repeat(a, p, axis=1)
    args = [lr, li, ldt, rep(lr), rep(li), rep(ldt),
            b_re.reshape(dg, n * p), b_im.reshape(dg, n * p), c_im.reshape(dg, p * n)]
    small = jax.ShapeDtypeStruct((dg, n), F32)
    big = jax.ShapeDtypeStruct((dg, n * p), F32)
    ab_re, ab_im, bf_re, bf_im, ncim = pl.pallas_call(
        _s5_param_kernel, out_shape=[small, small, big, big, big], name="s5_params")(*args)

    nb, gp = S5_BLOCKS, S5_PAIR
    eye = jnp.eye(gp, dtype=F32)
    pos = np.arange(nb) % S5_BLOCKS_PER_HALF
    bf = jnp.stack([bf_re, bf_im], 0).reshape(2, 2, nb, gp, n, p)
    bc = jnp.einsum('adbgnp,gh->bdgpahn', bf, eye).reshape(nb, 2, gp * p, 2 * gp * n)
    bw = jnp.zeros((nb, 2, S5_BLOCKS_PER_HALF, gp * p, 2 * gp * n), F32)
    bw = bw.at[np.arange(nb), :, pos].set(bc).reshape(nb, 2, S5_HALF, 2 * gp * n)
    cc = jnp.stack([c_re.reshape(2, nb, gp, p, n), ncim.reshape(2, nb, gp, p, n)], 0)
    cc = jnp.einsum('adbgpn,gh->bdagnhp', cc, eye).reshape(nb, 2, 2 * gp * n, gp * p)
    cw = jnp.zeros((nb, 2, 2 * gp * n, S5_BLOCKS_PER_HALF, gp * p), F32)
    cw = cw.at[np.arange(nb), :, :, pos].set(cc).reshape(nb, 2, 2 * gp * n, S5_HALF)
    a = jnp.stack([ab_re, ab_im], 0).reshape(2, 2, nb, gp * n).transpose(2, 1, 0, 3)
    hb = S5_BLOCKS_PER_HALF
    bw = bw.reshape(2, hb, 2, S5_HALF, 2 * gp * n).transpose(0, 2, 3, 1, 4).reshape(2, 2, S5_HALF, -1)
    cw = cw.reshape(2, hb, 2, 2 * gp * n, S5_HALF).transpose(0, 2, 1, 3, 4).reshape(2, 2, -1, S5_HALF)
    a = a.reshape(2, hb, 2, 2, 1, gp * n).transpose(0, 2, 1, 3, 4, 5)
    return bw.astype(BF16), cw.astype(BF16), a


def _s5_state_in(h):
    bsz = h.shape[0]
    return h.reshape(bsz, 2, 2, 2, S5_BLOCKS_PER_HALF, LANES).transpose(3, 1, 4, 2, 0, 5)


def _s5_state_out(st):
    bsz = st.shape[4]
    return st.transpose(4, 1, 3, 0, 2, 5).reshape(bsz, 2, 2, SSM_GROUPS, SSM_STATE)


def _s5_kernel(rows_per_step, steps, uf_ref, ub_ref, bw_ref, cw_ref, a_ref, h0_ref,
               yf_ref, yb_ref, st_ref, buf, coef, carry):
    r = rows_per_step
    nb = S5_BLOCKS_PER_HALF
    width = 2 * LANES

    @pl.when(pl.program_id(1) == 0)
    def _():
        carry[...] = h0_ref[0]
        coef[...] = jnp.broadcast_to(a_ref[0], coef.shape)

    for d, u_ref in enumerate((uf_ref, ub_ref)):
        buf[d] = jnp.dot(u_ref[...].astype(BF16), bw_ref[0, d], preferred_element_type=F32)

    chains = [(d, b) for d in range(2) for b in range(nb)]
    for g in range(r // 8):
        rows8 = slice(g * 8, (g + 1) * 8)

        def body(k, hs, rows8=rows8):
            out = []
            for (d, b), (h_re, h_im) in zip(chains, hs):
                t = k if d == 0 else steps - 1 - k
                r0 = pl.multiple_of(t * r + g * 8, 8)
                a_re, a_im = coef[d, b, 0], coef[d, b, 1]
                re_l, im_l = slice(b * width, b * width + LANES), slice(b * width + LANES, (b + 1) * width)
                n_re = a_re * h_re - a_im * h_im + buf[d, pl.ds(r0, 8), re_l]
                n_im = a_re * h_im + a_im * h_re + buf[d, pl.ds(r0, 8), im_l]
                buf[d, pl.ds(r0, 8), re_l] = n_re
                buf[d, pl.ds(r0, 8), im_l] = n_im
                out.append((n_re, n_im))
            return tuple(out)

        init = tuple((carry[d, b, 0, rows8], carry[d, b, 1, rows8]) for d, b in chains)
        final = lax.fori_loop(0, steps, body, init, unroll=8)
        for (d, b), (h_re, h_im) in zip(chains, final):
            carry[d, b, 0, rows8] = h_re
            carry[d, b, 1, rows8] = h_im

    for d, y_ref in enumerate((yf_ref, yb_ref)):
        y_ref[...] = jnp.dot(buf[d].astype(BF16), cw_ref[0, d], preferred_element_type=F32)
    st_ref[0] = carry[...]


def _s5_scan(u_tm, bw, cw, a, h0, rows_per_step, seq, steps):
    r = rows_per_step
    n_t = seq // steps
    rows = steps * r
    nb = S5_BLOCKS_PER_HALF
    wide = nb * 2 * LANES
    half4 = lambda h, t: (h, 0, 0, 0)
    half6 = lambda h, t: (h, 0, 0, 0, 0, 0)
    state = (2, nb, 2, r, LANES)
    in_specs = [pl.BlockSpec((rows, S5_HALF), lambda h, t: (t, h)),
                pl.BlockSpec((rows, S5_HALF), lambda h, t: (n_t - 1 - t, h)),
                pl.BlockSpec((1, 2, S5_HALF, wide), half4),
                pl.BlockSpec((1, 2, wide, S5_HALF), half4),
                pl.BlockSpec((1, 2, nb, 2, 1, LANES), half6),
                pl.BlockSpec((1,) + state, half6)]
    out_specs = [pl.BlockSpec((rows, S5_HALF), lambda h, t: (t, h)),
                 pl.BlockSpec((rows, S5_HALF), lambda h, t: (n_t - 1 - t, h)),
                 pl.BlockSpec((1,) + state, half6)]
    out_shape = [jax.ShapeDtypeStruct(u_tm.shape, F32), jax.ShapeDtypeStruct(u_tm.shape, F32),
                 jax.ShapeDtypeStruct((2,) + state, F32)]
    return pl.pallas_call(
        functools.partial(_s5_kernel, r, steps),
        grid=(2, n_t),
        in_specs=in_specs, out_specs=out_specs, out_shape=out_shape,
        scratch_shapes=[pltpu.VMEM((2, rows, wide), F32), pltpu.VMEM((2, nb, 2, 8, LANES), F32),
                        pltpu.VMEM(state, F32)],
        compiler_params=_cparams(("arbitrary", "arbitrary")),
        name="s5_scan",
    )(u_tm, u_tm, bw, cw, a, h0)


def _attn_kernel(q_ref, ckv_ref, kr_ref, wuk_ref, wuv_ref, o_ref, k_s, v_s):
    @pl.when(pl.program_id(1) == 0)
    def _():
        kv = ckv_ref[0].astype(BF16)
        kn = jnp.dot(kv, wuk_ref[...], preferred_element_type=F32)
        kr = kr_ref[0]
        for hd in range(MLA_HEADS):
            k_s[hd] = (kn[:, hd * HEAD_PAD:(hd + 1) * HEAD_PAD] + kr).astype(BF16)
        v_s[...] = jnp.dot(kv, wuv_ref[...], preferred_element_type=F32).astype(BF16)

    for hd in range(MLA_HEADS):
        qh = q_ref[:, hd * HEAD_PAD:(hd + 1) * HEAD_PAD]
        s = lax.dot_general(qh, k_s[hd], (((1,), (1,)), ((), ())), preferred_element_type=F32)
        p = jnp.exp(s - jnp.max(s, axis=-1, keepdims=True))
        l = jnp.sum(p, axis=-1, keepdims=True)
        o = jnp.dot(p.astype(BF16), v_s[:, hd * V_DIM:(hd + 1) * V_DIM], preferred_element_type=F32)
        o_ref[:, hd * V_DIM:(hd + 1) * V_DIM] = (o / l).astype(BF16)


def _attention(q, ckv_all, kr_all, w_uk_ext, w_uv, batch, seq):
    s_len = ckv_all.shape[1]
    tq = min(seq, ATTN_TILE)
    n_q = seq // tq
    return pl.pallas_call(
        _attn_kernel,
        grid=(batch, n_q),
        in_specs=[pl.BlockSpec((tq, MLA_HEADS * HEAD_PAD), lambda b, i: (b * n_q + i, 0)),
                  pl.BlockSpec((1, s_len, KV_RANK), lambda b, i: (b, 0, 0)),
                  pl.BlockSpec((1, s_len, HEAD_PAD), lambda b, i: (b, 0, 0)),
                  pl.BlockSpec(w_uk_ext.shape, lambda b, i: (0, 0)),
                  pl.BlockSpec(w_uv.shape, lambda b, i: (0, 0))],
        out_specs=pl.BlockSpec((tq, MLA_HEADS * V_DIM), lambda b, i: (b * n_q + i, 0)),
        out_shape=jax.ShapeDtypeStruct((batch * seq, MLA_HEADS * V_DIM), BF16),
        scratch_shapes=[pltpu.VMEM((MLA_HEADS, s_len, HEAD_PAD), BF16),
                        pltpu.VMEM((s_len, MLA_HEADS * V_DIM), BF16)],
        compiler_params=_cparams(("parallel", "arbitrary")),
        name="attention",
    )(q, ckv_all, kr_all, w_uk_ext, w_uv)


def _post_kernel(n_ctx, xc_ref, atc_ref, uc_ref, yfc_ref, ybc_ref, xl_ref, atl_ref, ul_ref, yfl_ref, ybl_ref,
                 m_ref, d_ref, wglu_ref, wos_ref, woa_ref, n2_ref, wq_ref, sk_ref,
                 x1_ref, h2_ref, s1_ref, s2_ref):
    is_ctx = pl.program_id(0) < n_ctx
    pick = lambda c_ref, l_ref: jnp.where(is_ctx, c_ref[...], l_ref[...])
    m = m_ref[0]
    y = pick(yfc_ref, yfl_ref) + pick(ybc_ref, ybl_ref) + pick(uc_ref, ul_ref) * d_ref[...]
    yg = jax.nn.gelu(y)
    gate = jax.nn.sigmoid(jnp.dot(yg.astype(BF16), wglu_ref[...], preferred_element_type=F32))
    mix = (jnp.dot((yg * gate).astype(BF16), wos_ref[...], preferred_element_type=F32)
           + jnp.dot(pick(atc_ref, atl_ref), woa_ref[...], preferred_element_type=F32))
    x1 = pick(xc_ref, xl_ref) + m[2:3] * mix
    x1_ref[...] = x1
    h2 = (_rms(x1, n2_ref[...]) * (1.0 + m[4:5]) + m[3:4]).astype(BF16)
    h2_ref[...] = h2
    qp = jnp.dot(h2, wq_ref[...], preferred_element_type=F32).astype(BF16)
    for hd in range(PEER_HEADS):
        for half, s_ref in enumerate((s1_ref, s2_ref)):
            c0 = (hd * 2 + half) * KEY_DIM
            s_ref[hd] = lax.dot_general(sk_ref[hd, half], qp[:, c0:c0 + KEY_DIM],
                                        (((1,), (1,)), ((), ())), preferred_element_type=F32)


def _post(ctx, lat, m6, ssm_d, w_glu, w_out_s, w_out_a, norm2, w_query, sub_keys):
    specs, args = [], []
    n_ctx, n_lat = (p[5] * p[6] // TOK_TILE for p in (ctx, lat))
    for pass_id, (x2d, attn, u_tm, yf, yb, batch, seq, stride) in enumerate((ctx, lat)):
        n_t = seq // TOK_TILE
        loc = (lambda i: jnp.minimum(i, n_ctx - 1)) if pass_id == 0 else (lambda i: jnp.maximum(i - n_ctx, 0))
        tok = lambda w, loc=loc: pl.BlockSpec((TOK_TILE, w), lambda i: (loc(i), 0))
        tm = pl.BlockSpec((TOK_TILE, SSM_WIDTH),
                          lambda i, loc=loc, n_t=n_t, stride=stride: (loc(i) % n_t, (loc(i) // n_t) * stride))
        specs += [tok(D_MODEL), tok(MLA_HEADS * V_DIM), tm, tm, tm]
        args += [x2d, attn, u_tm, yf, yb]
    lat_tiles_per_row = lat[6] // TOK_TILE
    mrow = lambda i: jnp.where(i < n_ctx, 0, 1 + (i - n_ctx) // lat_tiles_per_row)
    full = lambda a: pl.BlockSpec(a.shape, lambda i: (0,) * a.ndim)
    weights = [ssm_d, w_glu, w_out_s, w_out_a, norm2, w_query, sub_keys]
    total = (n_ctx + n_lat) * TOK_TILE
    tok_out = lambda w: pl.BlockSpec((TOK_TILE, w), lambda i: (i, 0))
    sc = pl.BlockSpec((PEER_HEADS, N_KEYS, TOK_TILE), lambda i: (0, 0, i))
    return pl.pallas_call(
        functools.partial(_post_kernel, n_ctx),
        grid=(n_ctx + n_lat,),
        in_specs=specs + [pl.BlockSpec((1, 6, D_MODEL), lambda i: (mrow(i), 0, 0))] + [full(w) for w in weights],
        out_specs=[tok_out(D_MODEL), tok_out(D_MODEL), sc, sc],
        out_shape=[jax.ShapeDtypeStruct((total, D_MODEL), F32), jax.ShapeDtypeStruct((total, D_MODEL), BF16),
                   jax.ShapeDtypeStruct((PEER_HEADS, N_KEYS, total), F32),
                   jax.ShapeDtypeStruct((PEER_HEADS, N_KEYS, total), F32)],
        compiler_params=_cparams(("arbitrary",)),
        name="post",
    )(*args, m6, *weights)


def _sort_pairs(lo, hi):
    def merge(lo, hi, r):
        step = r * 2
        if step < hi - lo:
            yield from merge(lo, hi, step)
            yield from merge(lo + r, hi, step)
            for i in range(lo + r, hi - r, step):
                yield (i, i + r)
        else:
            yield (lo, lo + r)

    if hi - lo >= 1:
        mid = lo + (hi - lo) // 2
        yield from _sort_pairs(lo, mid)
        yield from _sort_pairs(mid + 1, hi)
        yield from merge(lo, hi, 1)


def _sort_desc(vals):
    vals = list(vals)
    for i, j in _sort_pairs(0, len(vals) - 1):
        vals[i], vals[j] = jnp.maximum(vals[i], vals[j]), jnp.minimum(vals[i], vals[j])
    return vals


def _merge_top(a, b):
    n = len(a)
    c = [jnp.maximum(a[k], b[n - 1 - k]) for k in range(n)]
    stride = n // 2
    while stride:
        for i in range(n):
            if not i & stride:
                c[i], c[i + stride] = jnp.maximum(c[i], c[i + stride]), jnp.minimum(c[i], c[i + stride])
        stride //= 2
    return c


def _top_keys(ref):
    k = PEER_TOPK
    groups = [_sort_desc([ref[0, g * k + r] for r in range(k)]) for g in range(N_KEYS // k)]
    while len(groups) > 1:
        groups = [_merge_top(groups[i], groups[i + 1]) for i in range(0, len(groups), 2)]
    return groups[0]


def _route_kernel(s1_ref, s2_ref, o_ref, t2_ref):
    k = PEER_TOPK
    t1, t2 = _top_keys(s1_ref), _top_keys(s2_ref)
    cands = [t1[a] + t2[b] for a in range(k) for b in range(k) if (a + 1) * (b + 1) <= k]
    size = 1 << (len(cands) - 1).bit_length()
    cands += [jnp.full_like(t1[0], -jnp.inf)] * (size - len(cands))
    best = _sort_desc(cands)
    z = jnp.ones_like(best[0])
    for v in best[1:k]:
        z = z + jnp.exp(v - best[0])
    stats = (best[k - 1], t1[0], t2[0], 1.0 / z)
    for c in range(8):
        lanes = slice(c * LANES, (c + 1) * LANES)
        for n, v in enumerate(stats):
            o_ref[n, 0, :, lanes] = v[c:c + 1]
        for b in range(k):
            t2_ref[0, b:b + 1, lanes] = t2[b][c:c + 1]


def _route(s1d, s2d):
    n_rows = s1d.shape[2]
    ntok = n_rows * LANES
    blk = 8 * LANES
    spec = pl.BlockSpec((1, N_KEYS, 8, LANES), lambda i, h: (h, 0, i, 0))
    return pl.pallas_call(
        _route_kernel,
        grid=(n_rows // 8, PEER_HEADS),
        in_specs=[spec, spec],
        out_specs=[pl.BlockSpec((4, 1, 1, blk), lambda i, h: (0, h, 0, i)),
                   pl.BlockSpec((1, PEER_TOPK, blk), lambda i, h: (h, 0, i))],
        out_shape=[jax.ShapeDtypeStruct((4, PEER_HEADS, 1, ntok), F32),
                   jax.ShapeDtypeStruct((PEER_HEADS, PEER_TOPK, ntok), F32)],
        compiler_params=_cparams(("parallel", "parallel")),
        name="peer_route",
    )(s1d, s2d)


def _peer_kernel(h2_ref, s1_ref, s2_ref, t2_ref, st_ref, u_ref, v_ref, x1_ref, m_ref, fn_ref, yc_ref, yl_ref,
                 e2_s, act_s, aw_s, acc_s, *, ctx_tiles):
    k = pl.program_id(1)
    chunks = [slice(c * LANES, (c + 1) * LANES) for c in range(PEER_TN // LANES)]

    @pl.when(k == 0)
    def _():
        for hd in range(PEER_HEADS):
            e2_s[hd] = jnp.exp(s2_ref[hd] - st_ref[2, hd])
        acc_s[...] = jnp.zeros_like(acc_s)

    floor, gain = [], []
    for hd in range(PEER_HEADS):
        s1 = s1_ref[hd]
        low = jnp.full_like(s1, jnp.inf)
        for b in range(PEER_TOPK):
            t2b = t2_ref[hd, b:b + 1, :]
            low = jnp.where(s1 + t2b >= st_ref[0, hd], t2b, low)
        floor.append(low)
        gain.append(jnp.exp(s1 - st_ref[1, hd]) * (0.5 * st_ref[3, hd]))

    act_s[...] = lax.dot_general(u_ref[...].astype(BF16), h2_ref[...], (((1,), (1,)), ((), ())),
                                 preferred_element_type=F32)
    for cs in chunks:
        for ii in range(PEER_ROWS):
            rows = slice(ii * N_KEYS, (ii + 1) * N_KEYS)
            w = None
            for hd in range(PEER_HEADS):
                hit = s2_ref[hd, :, cs] >= floor[hd][ii:ii + 1, cs]
                term = jnp.where(hit, e2_s[hd, :, cs], 0.0) * gain[hd][ii:ii + 1, cs]
                w = term if w is None else w + term
            x = act_s[rows, cs]
            t = jnp.tanh(x * (GELU_C0 + GELU_C1 * (x * x)))
            aw_s[rows, cs] = ((x * w) * (1.0 + t)).astype(BF16)
    acc_s[...] += lax.dot_general(aw_s[...], v_ref[...].astype(BF16), (((0,), (0,)), ((), ())),
                                  preferred_element_type=F32)

    last = k == pl.num_programs(1) - 1
    is_ctx = pl.program_id(0) < ctx_tiles
    for cond, y_ref in ((is_ctx, yc_ref), (jnp.logical_not(is_ctx), yl_ref)):
        @pl.when(last & cond)
        def _(y_ref=y_ref):
            y_ref[...] = _rms(x1_ref[...] + m_ref[0][5:6] * acc_s[...], fn_ref[...])


def _peer(h2, s1t, s2t, t2, stats, u_tab, v_tab, x1, m6, mrow, final_norm, ctx_tokens):
    ntok = h2.shape[0]
    tn, te = PEER_TN, PEER_TE
    nc = ctx_tokens // tn
    tok = pl.BlockSpec((tn, D_MODEL), lambda t, k: (t, 0))
    tab = pl.BlockSpec((te, D_MODEL), lambda t, k: (k, 0))
    keys = pltpu.VMEM((PEER_HEADS, N_KEYS, tn), F32)
    return pl.pallas_call(
        functools.partial(_peer_kernel, ctx_tiles=nc),
        grid=(ntok // tn, N_EXPERTS // te),
        in_specs=[tok,
                  pl.BlockSpec((PEER_HEADS, PEER_ROWS, tn), lambda t, k: (0, k, t)),
                  pl.BlockSpec((PEER_HEADS, N_KEYS, tn), lambda t, k: (0, 0, t)),
                  pl.BlockSpec((PEER_HEADS, PEER_TOPK, tn), lambda t, k: (0, 0, t)),
                  pl.BlockSpec((4, PEER_HEADS, 1, tn), lambda t, k: (0, 0, 0, t)),
                  tab, tab, tok,
                  pl.BlockSpec((1, 6, D_MODEL), lambda t, k: (mrow(t), 0, 0)),
                  pl.BlockSpec((1, D_MODEL), lambda t, k: (0, 0))],
        out_specs=[pl.BlockSpec((tn, D_MODEL), lambda t, k: (jnp.minimum(t, nc - 1), 0)),
                   pl.BlockSpec((tn, D_MODEL), lambda t, k: (jnp.maximum(t - nc, 0), 0))],
        out_shape=[jax.ShapeDtypeStruct((ctx_tokens, D_MODEL), F32),
                   jax.ShapeDtypeStruct((ntok - ctx_tokens, D_MODEL), F32)],
        scratch_shapes=[keys, pltpu.VMEM((te, tn), F32), pltpu.VMEM((te, tn), BF16),
                        pltpu.VMEM((tn, D_MODEL), F32)],
        compiler_params=_cparams(("arbitrary", "arbitrary")),
        name="peer_dense",
    )(h2, s1t, s2t, t2, stats, u_tab, v_tab, x1, m6, final_norm)


def _rot_cols(w):
    j = np.arange(QK_ROPE)
    first = (j % (QK_ROPE // 2)) < (QK_ROPE // 4)
    perm = np.where(first, j + QK_ROPE // 4, j - QK_ROPE // 4)
    sign = np.where(first, -1.0, 1.0).astype(np.float32)
    return w[..., perm] * sign


def _rope_slot(w_rope):
    pad = [(0, 0)] * (w_rope.ndim - 1)
    return jnp.pad(w_rope, pad + [(QK_NOPE, HEAD_PAD - QK_NOPE - QK_ROPE)])


def _rope_tables(n_tokens):
    n_rows = n_tokens // GRID_W
    rows = jnp.repeat(jnp.arange(n_rows, dtype=F32), GRID_W)
    cols = jnp.tile(jnp.arange(GRID_W, dtype=F32), n_rows)
    half = QK_ROPE // 2
    inv_freq = 1.0 / (ROPE_BASE ** (jnp.arange(0, half, 2, dtype=F32) / half))
    ang_r = rows[:, None] * inv_freq
    ang_c = cols[:, None] * inv_freq
    ang = jnp.concatenate([ang_r, ang_r, ang_c, ang_c], axis=-1)
    lead = (QK_NOPE, HEAD_PAD - QK_NOPE - QK_ROPE)
    cos = jnp.pad(jnp.cos(ang), [(0, 0), lead], constant_values=1.0)
    sin = jnp.pad(jnp.sin(ang), [(0, 0), lead])
    return cos, sin


def kernel(x_prompt, x_sample, c, cache_ckv, cache_krope, state_ssm, c_ctx, w_mod, b_mod, norm1, w_in,
           ssm_lam_re, ssm_lam_im, ssm_log_dt, ssm_b_re, ssm_b_im, ssm_c_re, ssm_c_im, ssm_d, w_glu,
           q_norm, w_uq, kv_norm, w_uk, w_uv, w_out, norm2, w_query, sub_keys, u_table, v_table,
           final_norm):
    bc, tc_len, _ = x_prompt.shape
    bl, tl_len, _ = x_sample.shape
    l = 0
    row = lambda a: a.reshape(1, -1)
    assert w_mod.shape[0] == 1
    big = lambda a: a.reshape(a.shape[1:])

    n_mod = 8
    cvec = jnp.concatenate([c_ctx[None], c, jnp.zeros((n_mod - 1 - bl, D_MODEL), F32)], 0)
    m6 = _modulation(cvec, big(w_mod), row(b_mod[l])).reshape(n_mod, 6, D_MODEL)

    wi = w_in[l]
    o_kr = SSM_WIDTH + Q_RANK + KV_RANK
    w_kr = wi[:, o_kr:]
    w_in_ext = jnp.concatenate([wi[:, :o_kr], _rope_slot(w_kr), _rope_slot(_rot_cols(w_kr))], 1).astype(BF16)
    wq3 = w_uq[l].reshape(Q_RANK, MLA_HEADS, QK_NOPE + QK_ROPE)
    wq_main = jnp.pad(wq3, [(0, 0), (0, 0), (0, HEAD_PAD - QK_NOPE - QK_ROPE)])
    wq_rot = _rope_slot(_rot_cols(wq3[..., QK_NOPE:]))
    w_uq_ext = jnp.concatenate([wq_main.reshape(Q_RANK, -1), wq_rot.reshape(Q_RANK, -1)], 1).astype(BF16)
    w_uk_ext = jnp.pad(w_uk[l].reshape(KV_RANK, MLA_HEADS, QK_NOPE),
                       [(0, 0), (0, 0), (0, HEAD_PAD - QK_NOPE)]).reshape(KV_RANK, -1).astype(BF16)
    w_uv_b = w_uv[l].astype(BF16)
    w_glu_b = w_glu[l].astype(BF16)
    w_out_s = w_out[l][:SSM_WIDTH].astype(BF16)
    w_out_a = w_out[l][SSM_WIDTH:].astype(BF16)
    w_query_b = w_query[l].astype(BF16)
    sub_keys_b = sub_keys[l].astype(BF16)
    u_tab, v_tab = big(u_table), big(v_table)
    cos, sin = _rope_tables(tl_len)

    bw, cw, a_blk = _s5_params(ssm_lam_re[l], ssm_lam_im[l], ssm_log_dt[l], ssm_b_re[l], ssm_b_im[l],
                               ssm_c_re[l], ssm_c_im[l])

    ctx_row = lambda b: 0
    lat_row = lambda b: 1 + b
    xc = x_prompt.reshape(bc * tc_len, D_MODEL)
    xl = x_sample.reshape(bl * tl_len, D_MODEL)

    u_c, q_c, ckv_c, kr_c = _pre(xc, m6, ctx_row, row(norm1[l]), w_in_ext, row(q_norm[l]), row(kv_norm[l]),
                                 w_uq_ext[:, :MLA_HEADS * HEAD_PAD], bc, tc_len, None)
    u_l, q_l, ckv_l, kr_l = _pre(xl, m6, lat_row, row(norm1[l]), w_in_ext, row(q_norm[l]), row(kv_norm[l]),
                                 w_uq_ext, bl, tl_len, (cos, sin), pad_rows=True)

    rl = 2 * bl
    h0_c = jnp.zeros((2, 2, S5_BLOCKS_PER_HALF, 2, bc, LANES), F32)
    yf_c, yb_c, st_c = _s5_scan(u_c.reshape(tc_len * bc, SSM_WIDTH), bw, cw, a_blk, h0_c, bc, tc_len,
                                S5_TILE_ROWS // bc)
    h0_l = _s5_state_in(state_ssm[:, l])
    h0_l = jnp.stack([h0_l, jnp.zeros_like(h0_l)], axis=-2).reshape(h0_l.shape[:4] + (rl, LANES))
    yf_l, yb_l, _ = _s5_scan(u_l.reshape(tl_len * rl, SSM_WIDTH), bw, cw, a_blk, h0_l, rl, tl_len,
                             S5_TILE_ROWS // rl)

    at_c = _attention(q_c, ckv_c.reshape(bc, tc_len, KV_RANK), kr_c.reshape(bc, tc_len, HEAD_PAD),
                      w_uk_ext, w_uv_b, bc, tc_len)
    ckv_all = jnp.concatenate([cache_ckv[:, l], ckv_l.reshape(bl, tl_len, KV_RANK)], 1)
    kr_all = jnp.concatenate([_rope_slot(cache_krope[:, l]), kr_l.reshape(bl, tl_len, HEAD_PAD)], 1)
    at_l = _attention(q_l, ckv_all, kr_all, w_uk_ext, w_uv_b, bl, tl_len)

    tm_c = lambda a: a.reshape(tc_len, bc * SSM_WIDTH)
    tm_l = lambda a: a.reshape(tl_len, rl * SSM_WIDTH)
    ntok = bc * tc_len + bl * tl_len
    x1, h2, s1t, s2t = _post((xc, at_c, tm_c(u_c), tm_c(yf_c), tm_c(yb_c), bc, tc_len, 1),
                             (xl, at_l, u_l, tm_l(yf_l), tm_l(yb_l), bl, tl_len, 2),
                             m6, row(ssm_d[l]), w_glu_b, w_out_s, w_out_a, row(norm2[l]), w_query_b, sub_keys_b)

    dense = lambda a: a.reshape(PEER_HEADS, N_KEYS, ntok // LANES, LANES)
    stats, t2 = _route(dense(s1t), dense(s2t))
    ctx_peer_tiles = bc * tc_len // PEER_TN
    lat_peer_tiles = tl_len // PEER_TN
    peer_row = lambda t: jnp.where(t < ctx_peer_tiles, 0, 1 + (t - ctx_peer_tiles) // lat_peer_tiles)
    y_c, y_l = _peer(h2, s1t, s2t, t2, stats, u_tab, v_tab, x1, m6, peer_row, row(final_norm), bc * tc_len)

    y_prompt = y_c.reshape(bc, tc_len, D_MODEL)
    y_sample = y_l.reshape(bl, tl_len, D_MODEL)
    new_ckv = ckv_c.reshape(bc, 1, tc_len, KV_RANK)
    new_krope = kr_c[:, QK_NOPE:QK_NOPE + QK_ROPE].reshape(bc, 1, tc_len, QK_ROPE)
    new_ssm = _s5_state_out(st_c)[:, None]
    return (y_prompt, y_sample, new_ckv, new_krope, new_ssm)
```

```python
import functools
import math

import jax
import jax.numpy as jnp
import numpy as np
from jax import lax
from jax.experimental import pallas as pl
from jax.experimental.pallas import tpu as pltpu

F32 = jnp.float32
BF16 = jnp.bfloat16

D_MODEL = 1024
GRID_W = 64
EPS = 1e-6
SSM_WIDTH = 512
SSM_GROUP = 16
SSM_GROUPS = 32
SSM_STATE = 64
MLA_HEADS = 8
QK_NOPE = 64
QK_ROPE = 32
V_DIM = 64
Q_RANK = 384
KV_RANK = 256
ROPE_BASE = 10000.0
N_KEYS = 128
N_EXPERTS = N_KEYS * N_KEYS
PEER_HEADS = 8
PEER_TOPK = 16
KEY_DIM = 128

LANES = 128
HEAD_PAD = 128
TOK_TILE = 256
ATTN_TILE = 512
PRE_TILE = 512
S5_PAIR = 2
S5_BLOCKS = SSM_GROUPS // S5_PAIR
S5_HALF = SSM_WIDTH // 2
S5_BLOCKS_PER_HALF = S5_BLOCKS // 2
S5_TILE_ROWS = 1024
PEER_TN = 512
PEER_ROWS = 8
PEER_TE = PEER_ROWS * N_KEYS
GELU_C0 = math.sqrt(2.0 / math.pi)
GELU_C1 = 0.044715 * GELU_C0
VMEM_LIMIT = 54 * 1024 * 1024


def _cparams(sem):
    return pltpu.CompilerParams(dimension_semantics=sem, vmem_limit_bytes=VMEM_LIMIT)


def _rms(x, g):
    return x * lax.rsqrt(jnp.mean(x * x, axis=-1, keepdims=True) + EPS) * g


def _mod_kernel(c_ref, w_ref, b_ref, o_ref):
    o_ref[...] = jnp.dot(jax.nn.silu(c_ref[...]), w_ref[...], preferred_element_type=F32) + b_ref[...]


def _modulation(cvec, w_mod, b_mod):
    rows, d = cvec.shape
    n = w_mod.shape[1]
    tn = 1536
    return pl.pallas_call(
        _mod_kernel,
        grid=(n // tn,),
        in_specs=[pl.BlockSpec((rows, d), lambda j: (0, 0)),
                  pl.BlockSpec((d, tn), lambda j: (0, j)),
                  pl.BlockSpec((1, tn), lambda j: (0, j))],
        out_specs=pl.BlockSpec((rows, tn), lambda j: (0, j)),
        out_shape=jax.ShapeDtypeStruct((rows, n), F32),
        compiler_params=_cparams(("parallel",)),
        name="modulation",
    )(cvec, w_mod, b_mod)


def _pre_kernel(use_rope, pad_rows, x_ref, m_ref, n1_ref, win_ref, qn_ref, kvn_ref, wuq_ref, *rest):
    if use_rope:
        cos_ref, sin_ref, u_ref, q_ref, ckv_ref, kr_ref = rest
    else:
        u_ref, q_ref, ckv_ref, kr_ref = rest
    m = m_ref[0]
    h = _rms(x_ref[...], n1_ref[...]) * (1.0 + m[1:2]) + m[0:1]
    z = jnp.dot(h.astype(BF16), win_ref[...], preferred_element_type=F32)
    u_ref[:, :SSM_WIDTH] = z[:, :SSM_WIDTH]
    if pad_rows:
        u_ref[:, SSM_WIDTH:] = jnp.zeros((u_ref.shape[0], SSM_WIDTH), F32)
    o_q, o_kv, o_kr = SSM_WIDTH, SSM_WIDTH + Q_RANK, SSM_WIDTH + Q_RANK + KV_RANK
    cqn = _rms(z[:, o_q:o_kv], qn_ref[...])
    qq = jnp.dot(cqn.astype(BF16), wuq_ref[...], preferred_element_type=F32)
    ckv_ref[...] = _rms(z[:, o_kv:o_kr], kvn_ref[...])
    kr = z[:, o_kr:o_kr + HEAD_PAD]
    scale = (QK_NOPE + QK_ROPE) ** -0.5
    nq = MLA_HEADS * HEAD_PAD
    if use_rope:
        cos, sin = cos_ref[...], sin_ref[...]
        kr = kr * cos + z[:, o_kr + HEAD_PAD:o_kr + 2 * HEAD_PAD] * sin
        for hd in range(MLA_HEADS):
            sl = slice(hd * HEAD_PAD, (hd + 1) * HEAD_PAD)
            qh = qq[:, sl] * cos + qq[:, nq + hd * HEAD_PAD:nq + (hd + 1) * HEAD_PAD] * sin
            q_ref[:, sl] = (qh * scale).astype(BF16)
    else:
        q_ref[...] = (qq[:, :nq] * scale).astype(BF16)
    kr_ref[...] = kr


def _pre(x2d, m6, mrow, norm1, w_in_ext, q_norm, kv_norm, w_uq_ext, batch, seq, rope, pad_rows=False):
    ntok = batch * seq
    u_cols = SSM_WIDTH * (2 if pad_rows else 1)
    tile = min(seq, PRE_TILE)
    n_t = seq // tile
    use_rope = rope is not None
    full = lambda a: pl.BlockSpec(a.shape, lambda i: (0,) * a.ndim)
    in_specs = [pl.BlockSpec((tile, D_MODEL), lambda i: (i, 0)),
                pl.BlockSpec((1, 6, D_MODEL), lambda i: (mrow(i // n_t), 0, 0)),
                full(norm1), full(w_in_ext), full(q_norm), full(kv_norm), full(w_uq_ext)]
    args = [x2d, m6, norm1, w_in_ext, q_norm, kv_norm, w_uq_ext]
    if use_rope:
        in_specs += [pl.BlockSpec((tile, HEAD_PAD), lambda i: (i % n_t, 0))] * 2
        args += list(rope)
    out_specs = [pl.BlockSpec((tile, u_cols), lambda i: (i % n_t, i // n_t)),
                 pl.BlockSpec((tile, MLA_HEADS * HEAD_PAD), lambda i: (i, 0)),
                 pl.BlockSpec((tile, KV_RANK), lambda i: (i, 0)),
                 pl.BlockSpec((tile, HEAD_PAD), lambda i: (i, 0))]
    out_shape = [jax.ShapeDtypeStruct((seq, batch * u_cols), F32),
                 jax.ShapeDtypeStruct((ntok, MLA_HEADS * HEAD_PAD), BF16),
                 jax.ShapeDtypeStruct((ntok, KV_RANK), F32),
                 jax.ShapeDtypeStruct((ntok, HEAD_PAD), F32)]
    return pl.pallas_call(
        functools.partial(_pre_kernel, use_rope, pad_rows),
        grid=(ntok // tile,),
        in_specs=in_specs, out_specs=out_specs, out_shape=out_shape,
        compiler_params=_cparams(("parallel",)),
        name="pre_rope" if use_rope else "pre",
    )(*args)


def _s5_param_kernel(lr_ref, li_ref, ldt_ref, lrx_ref, lix_ref, ldtx_ref, bre_ref, bim_ref, cim_ref,
                     abr_ref, abi_ref, bfr_ref, bfi_ref, ncim_ref):
    def disc(lr, li, ldt):
        dt = jnp.exp(ldt)
        mag = jnp.exp(lr * dt)
        ab_re, ab_im = mag * jnp.cos(li * dt), mag * jnp.sin(li * dt)
        den = lr * lr + li * li
        br, bi = lr / den, -li / den
        ar = ab_re - 1.0
        return ab_re, ab_im, ar * br - ab_im * bi, ar * bi + ab_im * br

    ab_re, ab_im, _, _ = disc(lr_ref[...], li_ref[...], ldt_ref[...])
    abr_ref[...] = ab_re
    abi_ref[...] = ab_im
    _, _, f_re, f_im = disc(lrx_ref[...], lix_ref[...], ldtx_ref[...])
    b_re, b_im = bre_ref[...], bim_ref[...]
    bfr_ref[...] = f_re * b_re - f_im * b_im
    bfi_ref[...] = f_re * b_im + f_im * b_re
    ncim_ref[...] = -cim_ref[...]


def _s5_params(lam_re, lam_im, log_dt, b_re, b_im, c_re, c_im):
    dg = 2 * SSM_GROUPS
    n, p = SSM_STATE, SSM_GROUP
    lr = lam_re.reshape(dg, n)
    li = lam_im.reshape(dg, n)
    ldt = jnp.broadcast_to(log_dt.reshape(dg, 1), (dg, n))
    rep = lambda a: jnp.repeat(a, p, axis=1)
    args = [lr, li, ldt, rep(lr), rep(li), rep(ldt),
            b_re.reshape(dg, n * p), b_im.reshape(dg, n * p), c_im.reshape(dg, p * n)]
    small = jax.ShapeDtypeStruct((dg, n), F32)
    big = jax.ShapeDtypeStruct((dg, n * p), F32)
    ab_re, ab_im, bf_re, bf_im, ncim = pl.pallas_call(
        _s5_param_kernel, out_shape=[small, small, big, big, big], name="s5_params")(*args)

    nb, gp = S5_BLOCKS, S5_PAIR
    eye = jnp.eye(gp, dtype=F32)
    pos = np.arange(nb) % S5_BLOCKS_PER_HALF
    bf = jnp.stack([bf_re, bf_im], 0).reshape(2, 2, nb, gp, n, p)
    bc = jnp.einsum('adbgnp,gh->bdgpahn', bf, eye).reshape(nb, 2, gp * p, 2 * gp * n)
    bw = jnp.zeros((nb, 2, S5_BLOCKS_PER_HALF, gp * p, 2 * gp * n), F32)
    bw = bw.at[np.arange(nb), :, pos].set(bc).reshape(nb, 2, S5_HALF, 2 * gp * n)
    cc = jnp.stack([c_re.reshape(2, nb, gp, p, n), ncim.reshape(2, nb, gp, p, n)], 0)
    cc = jnp.einsum('adbgpn,gh->bdagnhp', cc, eye).reshape(nb, 2, 2 * gp * n, gp * p)
    cw = jnp.zeros((nb, 2, 2 * gp * n, S5_BLOCKS_PER_HALF, gp * p), F32)
    cw = cw.at[np.arange(nb), :, :, pos].set(cc).reshape(nb, 2, 2 * gp * n, S5_HALF)
    a = jnp.stack([ab_re, ab_im], 0).reshape(2, 2, nb, gp * n).transpose(2, 1, 0, 3)
    hb = S5_BLOCKS_PER_HALF
    bw = bw.reshape(2, hb, 2, S5_HALF, 2 * gp * n).transpose(0, 2, 3, 1, 4).reshape(2, 2, S5_HALF, -1)
    cw = cw.reshape(2, hb, 2, 2 * gp * n, S5_HALF).transpose(0, 2, 1, 3, 4).reshape(2, 2, -1, S5_HALF)
    a = a.reshape(2, hb, 2, 2, 1, gp * n).transpose(0, 2, 1, 3, 4, 5)
    return bw.astype(BF16), cw.astype(BF16), a


def _s5_state_in(h):
    bsz = h.shape[0]
    return h.reshape(bsz, 2, 2, 2, S5_BLOCKS_PER_HALF, LANES).transpose(3, 1, 4, 2, 0, 5)


def _s5_state_out(st):
    bsz = st.shape[4]
    return st.transpose(4, 1, 3, 0, 2, 5).reshape(bsz, 2, 2, SSM_GROUPS, SSM_STATE)


def _s5_kernel(rows_per_step, steps, uf_ref, ub_ref, bw_ref, cw_ref, a_ref, h0_ref,
               yf_ref, yb_ref, st_ref, buf, coef, carry):
    r = rows_per_step
    nb = S5_BLOCKS_PER_HALF
    width = 2 * LANES

    @pl.when(pl.program_id(1) == 0)
    def _():
        carry[...] = h0_ref[0]
        coef[...] = jnp.broadcast_to(a_ref[0], coef.shape)

    for d, u_ref in enumerate((uf_ref, ub_ref)):
        buf[d] = jnp.dot(u_ref[...].astype(BF16), bw_ref[0, d], preferred_element_type=F32)

    chains = [(d, b) for d in range(2) for b in range(nb)]
    for g in range(r // 8):
        rows8 = slice(g * 8, (g + 1) * 8)

        def body(k, hs, rows8=rows8):
            out = []
            for (d, b), (h_re, h_im) in zip(chains, hs):
                t = k if d == 0 else steps - 1 - k
                r0 = pl.multiple_of(t * r + g * 8, 8)
                a_re, a_im = coef[d, b, 0], coef[d, b, 1]
                re_l, im_l = slice(b * width, b * width + LANES), slice(b * width + LANES, (b + 1) * width)
                n_re = a_re * h_re - a_im * h_im + buf[d, pl.ds(r0, 8), re_l]
                n_im = a_re * h_im + a_im * h_re + buf[d, pl.ds(r0, 8), im_l]
                buf[d, pl.ds(r0, 8), re_l] = n_re
                buf[d, pl.ds(r0, 8), im_l] = n_im
                out.append((n_re, n_im))
            return tuple(out)

        init = tuple((carry[d, b, 0, rows8], carry[d, b, 1, rows8]) for d, b in chains)
        final = lax.fori_loop(0, steps, body, init, unroll=8)
        for (d, b), (h_re, h_im) in zip(chains, final):
            carry[d, b, 0, rows8] = h_re
            carry[d, b, 1, rows8] = h_im

    for d, y_ref in enumerate((yf_ref, yb_ref)):
        y_ref[...] = jnp.dot(buf[d].astype(BF16), cw_ref[0, d], preferred_element_type=F32)
    st_ref[0] = carry[...]


def _s5_scan(u_tm, bw, cw, a, h0, rows_per_step, seq, steps):
    r = rows_per_step
    n_t = seq // steps
    rows = steps * r
    nb = S5_BLOCKS_PER_HALF
    wide = nb * 2 * LANES
    half4 = lambda h, t: (h, 0, 0, 0)
    half6 = lambda h, t: (h, 0, 0, 0, 0, 0)
    state = (2, nb, 2, r, LANES)
    in_specs = [pl.BlockSpec((rows, S5_HALF), lambda h, t: (t, h)),
                pl.BlockSpec((rows, S5_HALF), lambda h, t: (n_t - 1 - t, h)),
                pl.BlockSpec((1, 2, S5_HALF, wide), half4),
                pl.BlockSpec((1, 2, wide, S5_HALF), half4),
                pl.BlockSpec((1, 2, nb, 2, 1, LANES), half6),
                pl.BlockSpec((1,) + state, half6)]
    out_specs = [pl.BlockSpec((rows, S5_HALF), lambda h, t: (t, h)),
                 pl.BlockSpec((rows, S5_HALF), lambda h, t: (n_t - 1 - t, h)),
                 pl.BlockSpec((1,) + state, half6)]
    out_shape = [jax.ShapeDtypeStruct(u_tm.shape, F32), jax.ShapeDtypeStruct(u_tm.shape, F32),
                 jax.ShapeDtypeStruct((2,) + state, F32)]
    return pl.pallas_call(
        functools.partial(_s5_kernel, r, steps),
        grid=(2, n_t),
        in_specs=in_specs, out_specs=out_specs, out_shape=out_shape,
        scratch_shapes=[pltpu.VMEM((2, rows, wide), F32), pltpu.VMEM((2, nb, 2, 8, LANES), F32),
                        pltpu.VMEM(state, F32)],
        compiler_params=_cparams(("arbitrary", "arbitrary")),
        name="s5_scan",
    )(u_tm, u_tm, bw, cw, a, h0)


def _attn_kernel(q_ref, ckv_ref, kr_ref, wuk_ref, wuv_ref, o_ref, k_s, v_s):
    @pl.when(pl.program_id(1) == 0)
    def _():
        kv = ckv_ref[0].astype(BF16)
        kn = jnp.dot(kv, wuk_ref[...], preferred_element_type=F32)
        kr = kr_ref[0]
        for hd in range(MLA_HEADS):
            k_s[hd] = (kn[:, hd * HEAD_PAD:(hd + 1) * HEAD_PAD] + kr).astype(BF16)
        v_s[...] = jnp.dot(kv, wuv_ref[...], preferred_element_type=F32).astype(BF16)

    for hd in range(MLA_HEADS):
        qh = q_ref[:, hd * HEAD_PAD:(hd + 1) * HEAD_PAD]
        s = lax.dot_general(qh, k_s[hd], (((1,), (1,)), ((), ())), preferred_element_type=F32)
        p = jnp.exp(s - jnp.max(s, axis=-1, keepdims=True))
        l = jnp.sum(p, axis=-1, keepdims=True)
        o = jnp.dot(p.astype(BF16), v_s[:, hd * V_DIM:(hd + 1) * V_DIM], preferred_element_type=F32)
        o_ref[:, hd * V_DIM:(hd + 1) * V_DIM] = (o / l).astype(BF16)


def _attention(q, ckv_all, kr_all, w_uk_ext, w_uv, batch, seq):
    s_len = ckv_all.shape[1]
    tq = min(seq, ATTN_TILE)
    n_q = seq // tq
    return pl.pallas_call(
        _attn_kernel,
        grid=(batch, n_q),
        in_specs=[pl.BlockSpec((tq, MLA_HEADS * HEAD_PAD), lambda b, i: (b * n_q + i, 0)),
                  pl.BlockSpec((1, s_len, KV_RANK), lambda b, i: (b, 0, 0)),
                  pl.BlockSpec((1, s_len, HEAD_PAD), lambda b, i: (b, 0, 0)),
                  pl.BlockSpec(w_uk_ext.shape, lambda b, i: (0, 0)),
                  pl.BlockSpec(w_uv.shape, lambda b, i: (0, 0))],
        out_specs=pl.BlockSpec((tq, MLA_HEADS * V_DIM), lambda b, i: (b * n_q + i, 0)),
        out_shape=jax.ShapeDtypeStruct((batch * seq, MLA_HEADS * V_DIM), BF16),
        scratch_shapes=[pltpu.VMEM((MLA_HEADS, s_len, HEAD_PAD), BF16),
                        pltpu.VMEM((s_len, MLA_HEADS * V_DIM), BF16)],
        compiler_params=_cparams(("parallel", "arbitrary")),
        name="attention",
    )(q, ckv_all, kr_all, w_uk_ext, w_uv)


def _post_kernel(n_ctx, xc_ref, atc_ref, uc_ref, yfc_ref, ybc_ref, xl_ref, atl_ref, ul_ref, yfl_ref, ybl_ref,
                 m_ref, d_ref, wglu_ref, wos_ref, woa_ref, n2_ref, wq_ref, sk_ref,
                 x1_ref, h2_ref, s1_ref, s2_ref):
    is_ctx = pl.program_id(0) < n_ctx
    pick = lambda c_ref, l_ref: jnp.where(is_ctx, c_ref[...], l_ref[...])
    m = m_ref[0]
    y = pick(yfc_ref, yfl_ref) + pick(ybc_ref, ybl_ref) + pick(uc_ref, ul_ref) * d_ref[...]
    yg = jax.nn.gelu(y)
    gate = jax.nn.sigmoid(jnp.dot(yg.astype(BF16), wglu_ref[...], preferred_element_type=F32))
    mix = (jnp.dot((yg * gate).astype(BF16), wos_ref[...], preferred_element_type=F32)
           + jnp.dot(pick(atc_ref, atl_ref), woa_ref[...], preferred_element_type=F32))
    x1 = pick(xc_ref, xl_ref) + m[2:3] * mix
    x1_ref[...] = x1
    h2 = (_rms(x1, n2_ref[...]) * (1.0 + m[4:5]) + m[3:4]).astype(BF16)
    h2_ref[...] = h2
    qp = jnp.dot(h2, wq_ref[...], preferred_element_type=F32).astype(BF16)
    for hd in range(PEER_HEADS):
        for half, s_ref in enumerate((s1_ref, s2_ref)):
            c0 = (hd * 2 + half) * KEY_DIM
            s_ref[hd] = lax.dot_general(sk_ref[hd, half], qp[:, c0:c0 + KEY_DIM],
                                        (((1,), (1,)), ((), ())), preferred_element_type=F32)


def _post(ctx, lat, m6, ssm_d, w_glu, w_out_s, w_out_a, norm2, w_query, sub_keys):
    specs, args = [], []
    n_ctx, n_lat = (p[5] * p[6] // TOK_TILE for p in (ctx, lat))
    for pass_id, (x2d, attn, u_tm, yf, yb, batch, seq, stride) in enumerate((ctx, lat)):
        n_t = seq // TOK_TILE
        loc = (lambda i: jnp.minimum(i, n_ctx - 1)) if pass_id == 0 else (lambda i: jnp.maximum(i - n_ctx, 0))
        tok = lambda w, loc=loc: pl.BlockSpec((TOK_TILE, w), lambda i: (loc(i), 0))
        tm = pl.BlockSpec((TOK_TILE, SSM_WIDTH),
                          lambda i, loc=loc, n_t=n_t, stride=stride: (loc(i) % n_t, (loc(i) // n_t) * stride))
        specs += [tok(D_MODEL), tok(MLA_HEADS * V_DIM), tm, tm, tm]
        args += [x2d, attn, u_tm, yf, yb]
    lat_tiles_per_row = lat[6] // TOK_TILE
    mrow = lambda i: jnp.where(i < n_ctx, 0, 1 + (i - n_ctx) // lat_tiles_per_row)
    full = lambda a: pl.BlockSpec(a.shape, lambda i: (0,) * a.ndim)
    weights = [ssm_d, w_glu, w_out_s, w_out_a, norm2, w_query, sub_keys]
    total = (n_ctx + n_lat) * TOK_TILE
    tok_out = lambda w: pl.BlockSpec((TOK_TILE, w), lambda i: (i, 0))
    sc = pl.BlockSpec((PEER_HEADS, N_KEYS, TOK_TILE), lambda i: (0, 0, i))
    return pl.pallas_call(
        functools.partial(_post_kernel, n_ctx),
        grid=(n_ctx + n_lat,),
        in_specs=specs + [pl.BlockSpec((1, 6, D_MODEL), lambda i: (mrow(i), 0, 0))] + [full(w) for w in weights],
        out_specs=[tok_out(D_MODEL), tok_out(D_MODEL), sc, sc],
        out_shape=[jax.ShapeDtypeStruct((total, D_MODEL), F32), jax.ShapeDtypeStruct((total, D_MODEL), BF16),
                   jax.ShapeDtypeStruct((PEER_HEADS, N_KEYS, total), F32),
                   jax.ShapeDtypeStruct((PEER_HEADS, N_KEYS, total), F32)],
        compiler_params=_cparams(("arbitrary",)),
        name="post",
    )(*args, m6, *weights)


def _sort_pairs(lo, hi):
    def merge(lo, hi, r):
        step = r * 2
        if step < hi - lo:
            yield from merge(lo, hi, step)
            yield from merge(lo + r, hi, step)
            for i in range(lo + r, hi - r, step):
                yield (i, i + r)
        else:
            yield (lo, lo + r)

    if hi - lo >= 1:
        mid = lo + (hi - lo) // 2
        yield from _sort_pairs(lo, mid)
        yield from _sort_pairs(mid + 1, hi)
        yield from merge(lo, hi, 1)


def _sort_desc(vals):
    vals = list(vals)
    for i, j in _sort_pairs(0, len(vals) - 1):
        vals[i], vals[j] = jnp.maximum(vals[i], vals[j]), jnp.minimum(vals[i], vals[j])
    return vals


def _merge_top(a, b):
    n = len(a)
    c = [jnp.maximum(a[k], b[n - 1 - k]) for k in range(n)]
    stride = n // 2
    while stride:
        for i in range(n):
            if not i & stride:
                c[i], c[i + stride] = jnp.maximum(c[i], c[i + stride]), jnp.minimum(c[i], c[i + stride])
        stride //= 2
    return c


def _top_keys(ref, hd):
    k = PEER_TOPK
    groups = [_sort_desc([ref[hd, g * k + r] for r in range(k)]) for g in range(N_KEYS // k)]
    while len(groups) > 1:
        groups = [_merge_top(groups[i], groups[i + 1]) for i in range(0, len(groups), 2)]
    return groups[0]


def _route_kernel(s1_ref, s2_ref, o_ref, t2_ref):
    k = PEER_TOPK

    def head(hd, _):
        t1, t2 = _top_keys(s1_ref, hd), _top_keys(s2_ref, hd)
        cands = [t1[a] + t2[b] for a in range(k) for b in range(k) if (a + 1) * (b + 1) <= k]
        size = 1 << (len(cands) - 1).bit_length()
        cands += [jnp.full_like(t1[0], -jnp.inf)] * (size - len(cands))
        best = _sort_desc(cands)
        z = jnp.ones_like(best[0])
        for v in best[1:k]:
            z = z + jnp.exp(v - best[0])
        stats = (best[k - 1], t1[0], t2[0], 1.0 / z)
        for c in range(8):
            lanes = slice(c * LANES, (c + 1) * LANES)
            for n, v in enumerate(stats):
                o_ref[n, hd, :, lanes] = v[c:c + 1]
            for b in range(k):
                t2_ref[hd, b:b + 1, lanes] = t2[b][c:c + 1]
        return 0

    lax.fori_loop(0, PEER_HEADS, head, 0)


def _route(s1d, s2d):
    n_rows = s1d.shape[2]
    ntok = n_rows * LANES
    blk = 8 * LANES
    spec = pl.BlockSpec((PEER_HEADS, N_KEYS, 8, LANES), lambda i: (0, 0, i, 0))
    return pl.pallas_call(
        _route_kernel,
        grid=(n_rows // 8,),
        in_specs=[spec, spec],
        out_specs=[pl.BlockSpec((4, PEER_HEADS, 1, blk), lambda i: (0, 0, 0, i)),
                   pl.BlockSpec((PEER_HEADS, PEER_TOPK, blk), lambda i: (0, 0, i))],
        out_shape=[jax.ShapeDtypeStruct((4, PEER_HEADS, 1, ntok), F32),
                   jax.ShapeDtypeStruct((PEER_HEADS, PEER_TOPK, ntok), F32)],
        compiler_params=_cparams(("parallel",)),
        name="peer_route",
    )(s1d, s2d)


def _peer_kernel(h2_ref, s1_ref, s2_ref, t2_ref, st_ref, u_ref, v_ref, x1_ref, m_ref, fn_ref, yc_ref, yl_ref,
                 e2_s, act_s, aw_s, acc_s, *, ctx_tiles):
    k = pl.program_id(1)
    chunks = [slice(c * LANES, (c + 1) * LANES) for c in range(PEER_TN // LANES)]

    @pl.when(k == 0)
    def _():
        for hd in range(PEER_HEADS):
            e2_s[hd] = jnp.exp(s2_ref[hd] - st_ref[2, hd])
        acc_s[...] = jnp.zeros_like(acc_s)

    floor, gain = [], []
    for hd in range(PEER_HEADS):
        s1 = s1_ref[hd]
        low = jnp.full_like(s1, jnp.inf)
        for b in range(PEER_TOPK):
            t2b = t2_ref[hd, b:b + 1, :]
            low = jnp.where(s1 + t2b >= st_ref[0, hd], t2b, low)
        floor.append(low)
        gain.append(jnp.exp(s1 - st_ref[1, hd]) * (0.5 * st_ref[3, hd]))

    act_s[...] = lax.dot_general(u_ref[...].astype(BF16), h2_ref[...], (((1,), (1,)), ((), ())),
                                 preferred_element_type=F32)
    for cs in chunks:
        for ii in range(PEER_ROWS):
            rows = slice(ii * N_KEYS, (ii + 1) * N_KEYS)
            w = None
            for hd in range(PEER_HEADS):
                hit = s2_ref[hd, :, cs] >= floor[hd][ii:ii + 1, cs]
                term = jnp.where(hit, e2_s[hd, :, cs], 0.0) * gain[hd][ii:ii + 1, cs]
                w = term if w is None else w + term
            x = act_s[rows, cs]
            t = jnp.tanh(x * (GELU_C0 + GELU_C1 * (x * x)))
            aw_s[rows, cs] = ((x * w) * (1.0 + t)).astype(BF16)
    acc_s[...] += lax.dot_general(aw_s[...], v_ref[...].astype(BF16), (((0,), (0,)), ((), ())),
                                  preferred_element_type=F32)

    last = k == pl.num_programs(1) - 1
    is_ctx = pl.program_id(0) < ctx_tiles
    for cond, y_ref in ((is_ctx, yc_ref), (jnp.logical_not(is_ctx), yl_ref)):
        @pl.when(last & cond)
        def _(y_ref=y_ref):
            y_ref[...] = _rms(x1_ref[...] + m_ref[0][5:6] * acc_s[...], fn_ref[...])


def _peer(h2, s1t, s2t, t2, stats, u_tab, v_tab, x1, m6, mrow, final_norm, ctx_tokens):
    ntok = h2.shape[0]
    tn, te = PEER_TN, PEER_TE
    nc = ctx_tokens // tn
    tok = pl.BlockSpec((tn, D_MODEL), lambda t, k: (t, 0))
    tab = pl.BlockSpec((te, D_MODEL), lambda t, k: (k, 0))
    keys = pltpu.VMEM((PEER_HEADS, N_KEYS, tn), F32)
    return pl.pallas_call(
        functools.partial(_peer_kernel, ctx_tiles=nc),
        grid=(ntok // tn, N_EXPERTS // te),
        in_specs=[tok,
                  pl.BlockSpec((PEER_HEADS, PEER_ROWS, tn), lambda t, k: (0, k, t)),
                  pl.BlockSpec((PEER_HEADS, N_KEYS, tn), lambda t, k: (0, 0, t)),
                  pl.BlockSpec((PEER_HEADS, PEER_TOPK, tn), lambda t, k: (0, 0, t)),
                  pl.BlockSpec((4, PEER_HEADS, 1, tn), lambda t, k: (0, 0, 0, t)),
                  tab, tab, tok,
                  pl.BlockSpec((1, 6, D_MODEL), lambda t, k: (mrow(t), 0, 0)),
                  pl.BlockSpec((1, D_MODEL), lambda t, k: (0, 0))],
        out_specs=[pl.BlockSpec((tn, D_MODEL), lambda t, k: (jnp.minimum(t, nc - 1), 0)),
                   pl.BlockSpec((tn, D_MODEL), lambda t, k: (jnp.maximum(t - nc, 0), 0))],
        out_shape=[jax.ShapeDtypeStruct((ctx_tokens, D_MODEL), F32),
                   jax.ShapeDtypeStruct((ntok - ctx_tokens, D_MODEL), F32)],
        scratch_shapes=[keys, pltpu.VMEM((te, tn), F32), pltpu.VMEM((te, tn), BF16),
                        pltpu.VMEM((tn, D_MODEL), F32)],
        compiler_params=_cparams(("arbitrary", "arbitrary")),
        name="peer_dense",
    )(h2, s1t, s2t, t2, stats, u_tab, v_tab, x1, m6, final_norm)


def _rot_cols(w):
    j = np.arange(QK_ROPE)
    first = (j % (QK_ROPE // 2)) < (QK_ROPE // 4)
    perm = np.where(first, j + QK_ROPE // 4, j - QK_ROPE // 4)
    sign = np.where(first, -1.0, 1.0).astype(np.float32)
    return w[..., perm] * sign


def _rope_slot(w_rope):
    pad = [(0, 0)] * (w_rope.ndim - 1)
    return jnp.pad(w_rope, pad + [(QK_NOPE, HEAD_PAD - QK_NOPE - QK_ROPE)])


def _rope_tables(n_tokens):
    n_rows = n_tokens // GRID_W
    rows = jnp.repeat(jnp.arange(n_rows, dtype=F32), GRID_W)
    cols = jnp.tile(jnp.arange(GRID_W, dtype=F32), n_rows)
    half = QK_ROPE // 2
    inv_freq = 1.0 / (ROPE_BASE ** (jnp.arange(0, half, 2, dtype=F32) / half))
    ang_r = rows[:, None] * inv_freq
    ang_c = cols[:, None] * inv_freq
    ang = jnp.concatenate([ang_r, ang_r, ang_c, ang_c], axis=-1)
    lead = (QK_NOPE, HEAD_PAD - QK_NOPE - QK_ROPE)
    cos = jnp.pad(jnp.cos(ang), [(0, 0), lead], constant_values=1.0)
    sin = jnp.pad(jnp.sin(ang), [(0, 0), lead])
    return cos, sin


def kernel(x_prompt, x_sample, c, cache_ckv, cache_krope, state_ssm, c_ctx, w_mod, b_mod, norm1, w_in,
           ssm_lam_re, ssm_lam_im, ssm_log_dt, ssm_b_re, ssm_b_im, ssm_c_re, ssm_c_im, ssm_d, w_glu,
           q_norm, w_uq, kv_norm, w_uk, w_uv, w_out, norm2, w_query, sub_keys, u_table, v_table,
           final_norm):
    bc, tc_len, _ = x_prompt.shape
    bl, tl_len, _ = x_sample.shape
    l = 0
    row = lambda a: a.reshape(1, -1)
    assert w_mod.shape[0] == 1
    big = lambda a: a.reshape(a.shape[1:])

    n_mod = 8
    cvec = jnp.concatenate([c_ctx[None], c, jnp.zeros((n_mod - 1 - bl, D_MODEL), F32)], 0)
    m6 = _modulation(cvec, big(w_mod), row(b_mod[l])).reshape(n_mod, 6, D_MODEL)

    wi = w_in[l]
    o_kr = SSM_WIDTH + Q_RANK + KV_RANK
    w_kr = wi[:, o_kr:]
    w_in_ext = jnp.concatenate([wi[:, :o_kr], _rope_slot(w_kr), _rope_slot(_rot_cols(w_kr))], 1).astype(BF16)
    wq3 = w_uq[l].reshape(Q_RANK, MLA_HEADS, QK_NOPE + QK_ROPE)
    wq_main = jnp.pad(wq3, [(0, 0), (0, 0), (0, HEAD_PAD - QK_NOPE - QK_ROPE)])
    wq_rot = _rope_slot(_rot_cols(wq3[..., QK_NOPE:]))
    w_uq_ext = jnp.concatenate([wq_main.reshape(Q_RANK, -1), wq_rot.reshape(Q_RANK, -1)], 1).astype(BF16)
    w_uk_ext = jnp.pad(w_uk[l].reshape(KV_RANK, MLA_HEADS, QK_NOPE),
                       [(0, 0), (0, 0), (0, HEAD_PAD - QK_NOPE)]).reshape(KV_RANK, -1).astype(BF16)
    w_uv_b = w_uv[l].astype(BF16)
    w_glu_b = w_glu[l].astype(BF16)
    w_out_s = w_out[l][:SSM_WIDTH].astype(BF16)
    w_out_a = w_out[l][SSM_WIDTH:].astype(BF16)
    w_query_b = w_query[l].astype(BF16)
    sub_keys_b = sub_keys[l].astype(BF16)
    u_tab, v_tab = big(u_table), big(v_table)
    cos, sin = _rope_tables(tl_len)

    bw, cw, a_blk = _s5_params(ssm_lam_re[l], ssm_lam_im[l], ssm_log_dt[l], ssm_b_re[l], ssm_b_im[l],
                               ssm_c_re[l], ssm_c_im[l])

    ctx_row = lambda b: 0
    lat_row = lambda b: 1 + b
    xc = x_prompt.reshape(bc * tc_len, D_MODEL)
    xl = x_sample.reshape(bl * tl_len, D_MODEL)

    u_c, q_c, ckv_c, kr_c = _pre(xc, m6, ctx_row, row(norm1[l]), w_in_ext, row(q_norm[l]), row(kv_norm[l]),
                                 w_uq_ext[:, :MLA_HEADS * HEAD_PAD], bc, tc_len, None)
    u_l, q_l, ckv_l, kr_l = _pre(xl, m6, lat_row, row(norm1[l]), w_in_ext, row(q_norm[l]), row(kv_norm[l]),
                                 w_uq_ext, bl, tl_len, (cos, sin), pad_rows=True)

    rl = 2 * bl
    h0_c = jnp.zeros((2, 2, S5_BLOCKS_PER_HALF, 2, bc, LANES), F32)
    yf_c, yb_c, st_c = _s5_scan(u_c.reshape(tc_len * bc, SSM_WIDTH), bw, cw, a_blk, h0_c, bc, tc_len,
                                S5_TILE_ROWS // bc)
    h0_l = _s5_state_in(state_ssm[:, l])
    h0_l = jnp.stack([h0_l, jnp.zeros_like(h0_l)], axis=-2).reshape(h0_l.shape[:4] + (rl, LANES))
    yf_l, yb_l, _ = _s5_scan(u_l.reshape(tl_len * rl, SSM_WIDTH), bw, cw, a_blk, h0_l, rl, tl_len,
                             S5_TILE_ROWS // rl)

    at_c = _attention(q_c, ckv_c.reshape(bc, tc_len, KV_RANK), kr_c.reshape(bc, tc_len, HEAD_PAD),
                      w_uk_ext, w_uv_b, bc, tc_len)
    ckv_all = jnp.concatenate([cache_ckv[:, l], ckv_l.reshape(bl, tl_len, KV_RANK)], 1)
    kr_all = jnp.concatenate([_rope_slot(cache_krope[:, l]), kr_l.reshape(bl, tl_len, HEAD_PAD)], 1)
    at_l = _attention(q_l, ckv_all, kr_all, w_uk_ext, w_uv_b, bl, tl_len)

    tm_c = lambda a: a.reshape(tc_len, bc * SSM_WIDTH)
    tm_l = lambda a: a.reshape(tl_len, rl * SSM_WIDTH)
    ntok = bc * tc_len + bl * tl_len
    x1, h2, s1t, s2t = _post((xc, at_c, tm_c(u_c), tm_c(yf_c), tm_c(yb_c), bc, tc_len, 1),
                             (xl, at_l, u_l, tm_l(yf_l), tm_l(yb_l), bl, tl_len, 2),
                             m6, row(ssm_d[l]), w_glu_b, w_out_s, w_out_a, row(norm2[l]), w_query_b, sub_keys_b)

    dense = lambda a: a.reshape(PEER_HEADS, N_KEYS, ntok // LANES, LANES)
    stats, t2 = _route(dense(s1t), dense(s2t))
    ctx_peer_tiles = bc * tc_len // PEER_TN
    lat_peer_tiles = tl_len // PEER_TN
    peer_row = lambda t: jnp.where(t < ctx_peer_tiles, 0, 1 + (t - ctx_peer_tiles) // lat_peer_tiles)
    y_c, y_l = _peer(h2, s1t, s2t, t2, stats, u_tab, v_tab, x1, m6, peer_row, row(final_norm), bc * tc_len)

    y_prompt = y_c.reshape(bc, tc_len, D_MODEL)
    y_sample = y_l.reshape(bl, tl_len, D_MODEL)
    new_ckv = ckv_c.reshape(bc, 1, tc_len, KV_RANK)
    new_krope = kr_c[:, QK_NOPE:QK_NOPE + QK_ROPE].reshape(bc, 1, tc_len, QK_ROPE)
    new_ssm = _s5_state_out(st_c)[:, None]
    return (y_prompt, y_sample, new_ckv, new_krope, new_ssm)
```

```python
import functools
import math

import jax
import jax.numpy as jnp
import numpy as np
from jax import lax
from jax.experimental import pallas as pl
from jax.experimental.pallas import tpu as pltpu

F32 = jnp.float32
BF16 = jnp.bfloat16

D_MODEL = 1024
GRID_W = 64
EPS = 1e-6
SSM_WIDTH = 512
SSM_GROUP = 16
SSM_GROUPS = 32
SSM_STATE = 64
MLA_HEADS = 8
QK_NOPE = 64
QK_ROPE = 32
V_DIM = 64
Q_RANK = 384
KV_RANK = 256
ROPE_BASE = 10000.0
N_KEYS = 128
N_EXPERTS = N_KEYS * N_KEYS
PEER_HEADS = 8
PEER_TOPK = 16
KEY_DIM = 128

LANES = 128
HEAD_PAD = 128
TOK_TILE = 256
ATTN_TILE = 512
PRE_TILE = 512
S5_PAIR = 2
S5_BLOCKS = SSM_GROUPS // S5_PAIR
S5_HALF = SSM_WIDTH // 2
S5_BLOCKS_PER_HALF = S5_BLOCKS // 2
S5_TILE_ROWS = 1024
PEER_TN = 512
PEER_ROWS = 8
PEER_TE = PEER_ROWS * N_KEYS
GELU_C0 = math.sqrt(2.0 / math.pi)
GELU_C1 = 0.044715 * GELU_C0
VMEM_LIMIT = 54 * 1024 * 1024


def _cparams(sem):
    return pltpu.CompilerParams(dimension_semantics=sem, vmem_limit_bytes=VMEM_LIMIT)


def _rms(x, g):
    return x * lax.rsqrt(jnp.mean(x * x, axis=-1, keepdims=True) + EPS) * g


def _mod_kernel(c_ref, w_ref, b_ref, o_ref):
    o_ref[...] = jnp.dot(jax.nn.silu(c_ref[...]), w_ref[...], preferred_element_type=F32) + b_ref[...]


def _modulation(cvec, w_mod, b_mod):
    rows, d = cvec.shape
    n = w_mod.shape[1]
    tn = 1536
    return pl.pallas_call(
        _mod_kernel,
        grid=(n // tn,),
        in_specs=[pl.BlockSpec((rows, d), lambda j: (0, 0)),
                  pl.BlockSpec((d, tn), lambda j: (0, j)),
                  pl.BlockSpec((1, tn), lambda j: (0, j))],
        out_specs=pl.BlockSpec((rows, tn), lambda j: (0, j)),
        out_shape=jax.ShapeDtypeStruct((rows, n), F32),
        compiler_params=_cparams(("parallel",)),
        name="modulation",
    )(cvec, w_mod, b_mod)


def _pre_kernel(use_rope, pad_rows, x_ref, m_ref, n1_ref, win_ref, qn_ref, kvn_ref, wuq_ref, *rest):
    if use_rope:
        cos_ref, sin_ref, u_ref, q_ref, ckv_ref, kr_ref = rest
    else:
        u_ref, q_ref, ckv_ref, kr_ref = rest
    m = m_ref[0]
    h = _rms(x_ref[...], n1_ref[...]) * (1.0 + m[1:2]) + m[0:1]
    z = jnp.dot(h.astype(BF16), win_ref[...], preferred_element_type=F32)
    u_ref[:, :SSM_WIDTH] = z[:, :SSM_WIDTH]
    if pad_rows:
        u_ref[:, SSM_WIDTH:] = jnp.zeros((u_ref.shape[0], SSM_WIDTH), F32)
    o_q, o_kv, o_kr = SSM_WIDTH, SSM_WIDTH + Q_RANK, SSM_WIDTH + Q_RANK + KV_RANK
    cqn = _rms(z[:, o_q:o_kv], qn_ref[...])
    qq = jnp.dot(cqn.astype(BF16), wuq_ref[...], preferred_element_type=F32)
    ckv_ref[...] = _rms(z[:, o_kv:o_kr], kvn_ref[...])
    kr = z[:, o_kr:o_kr + HEAD_PAD]
    scale = (QK_NOPE + QK_ROPE) ** -0.5
    nq = MLA_HEADS * HEAD_PAD
    if use_rope:
        cos, sin = cos_ref[...], sin_ref[...]
        kr = kr * cos + z[:, o_kr + HEAD_PAD:o_kr + 2 * HEAD_PAD] * sin
        for hd in range(MLA_HEADS):
            sl = slice(hd * HEAD_PAD, (hd + 1) * HEAD_PAD)
            qh = qq[:, sl] * cos + qq[:, nq + hd * HEAD_PAD:nq + (hd + 1) * HEAD_PAD] * sin
            q_ref[:, sl] = (qh * scale).astype(BF16)
    else:
        q_ref[...] = (qq[:, :nq] * scale).astype(BF16)
    kr_ref[...] = kr


def _pre(x2d, m6, mrow, norm1, w_in_ext, q_norm, kv_norm, w_uq_ext, batch, seq, rope, pad_rows=False):
    ntok = batch * seq
    u_cols = SSM_WIDTH * (2 if pad_rows else 1)
    tile = min(seq, PRE_TILE)
    n_t = seq // tile
    use_rope = rope is not None
    full = lambda a: pl.BlockSpec(a.shape, lambda i: (0,) * a.ndim)
    in_specs = [pl.BlockSpec((tile, D_MODEL), lambda i: (i, 0)),
                pl.BlockSpec((1, 6, D_MODEL), lambda i: (mrow(i // n_t), 0, 0)),
                full(norm1), full(w_in_ext), full(q_norm), full(kv_norm), full(w_uq_ext)]
    args = [x2d, m6, norm1, w_in_ext, q_norm, kv_norm, w_uq_ext]
    if use_rope:
        in_specs += [pl.BlockSpec((tile, HEAD_PAD), lambda i: (i % n_t, 0))] * 2
        args += list(rope)
    out_specs = [pl.BlockSpec((tile, u_cols), lambda i: (i % n_t, i // n_t)),
                 pl.BlockSpec((tile, MLA_HEADS * HEAD_PAD), lambda i: (i, 0)),
                 pl.BlockSpec((tile, KV_RANK), lambda i: (i, 0)),
                 pl.BlockSpec((tile, HEAD_PAD), lambda i: (i, 0))]
    out_shape = [jax.ShapeDtypeStruct((seq, batch * u_cols), F32),
                 jax.ShapeDtypeStruct((ntok, MLA_HEADS * HEAD_PAD), BF16),
                 jax.ShapeDtypeStruct((ntok, KV_RANK), F32),
                 jax.ShapeDtypeStruct((ntok, HEAD_PAD), F32)]
    return pl.pallas_call(
        functools.partial(_pre_kernel, use_rope, pad_rows),
        grid=(ntok // tile,),
        in_specs=in_specs, out_specs=out_specs, out_shape=out_shape,
        compiler_params=_cparams(("parallel",)),
        name="pre_rope" if use_rope else "pre",
    )(*args)


def _s5_param_kernel(lr_ref, li_ref, ldt_ref, lrx_ref, lix_ref, ldtx_ref, bre_ref, bim_ref, cim_ref,
                     abr_ref, abi_ref, bfr_ref, bfi_ref, ncim_ref):
    def disc(lr, li, ldt):
        dt = jnp.exp(ldt)
        mag = jnp.exp(lr * dt)
        ab_re, ab_im = mag * jnp.cos(li * dt), mag * jnp.sin(li * dt)
        den = lr * lr + li * li
        br, bi = lr / den, -li / den
        ar = ab_re - 1.0
        return ab_re, ab_im, ar * br - ab_im * bi, ar * bi + ab_im * br

    ab_re, ab_im, _, _ = disc(lr_ref[...], li_ref[...], ldt_ref[...])
    abr_ref[...] = ab_re
    abi_ref[...] = ab_im
    _, _, f_re, f_im = disc(lrx_ref[...], lix_ref[...], ldtx_ref[...])
    b_re, b_im = bre_ref[...], bim_ref[...]
    bfr_ref[...] = f_re * b_re - f_im * b_im
    bfi_ref[...] = f_re * b_im + f_im * b_re
    ncim_ref[...] = -cim_ref[...]


def _s5_params(lam_re, lam_im, log_dt, b_re, b_im, c_re, c_im):
    dg = 2 * SSM_GROUPS
    n, p = SSM_STATE, SSM_GROUP
    lr = lam_re.reshape(dg, n)
    li = lam_im.reshape(dg, n)
    ldt = jnp.broadcast_to(log_dt.reshape(dg, 1), (dg, n))
    rep = lambda a: jnp.repeat(a, p, axis=1)
    args = [lr, li, ldt, rep(lr), rep(li), rep(ldt),
            b_re.reshape(dg, n * p), b_im.reshape(dg, n * p), c_im.reshape(dg, p * n)]
    small = jax.ShapeDtypeStruct((dg, n), F32)
    big = jax.ShapeDtypeStruct((dg, n * p), F32)
    ab_re, ab_im, bf_re, bf_im, ncim = pl.pallas_call(
        _s5_param_kernel, out_shape=[small, small, big, big, big], name="s5_params")(*args)

    nb, gp = S5_BLOCKS, S5_PAIR
    eye = jnp.eye(gp, dtype=F32)
    pos = np.arange(nb) % S5_BLOCKS_PER_HALF
    bf = jnp.stack([bf_re, bf_im], 0).reshape(2, 2, nb, gp, n, p)
    bc = jnp.einsum('adbgnp,gh->bdgpahn', bf, eye).reshape(nb, 2, gp * p, 2 * gp * n)
    bw = jnp.zeros((nb, 2, S5_BLOCKS_PER_HALF, gp * p, 2 * gp * n), F32)
    bw = bw.at[np.arange(nb), :, pos].set(bc).reshape(nb, 2, S5_HALF, 2 * gp * n)
    cc = jnp.stack([c_re.reshape(2, nb, gp, p, n), ncim.reshape(2, nb, gp, p, n)], 0)
    cc = jnp.einsum('adbgpn,gh->bdagnhp', cc, eye).reshape(nb, 2, 2 * gp * n, gp * p)
    cw = jnp.zeros((nb, 2, 2 * gp * n, S5_BLOCKS_PER_HALF, gp * p), F32)
    cw = cw.at[np.arange(nb), :, :, pos].set(cc).reshape(nb, 2, 2 * gp * n, S5_HALF)
    a = jnp.stack([ab_re, ab_im], 0).reshape(2, 2, nb, gp * n).transpose(2, 1, 0, 3)
    hb = S5_BLOCKS_PER_HALF
    bw = bw.reshape(2, hb, 2, S5_HALF, 2 * gp * n).transpose(0, 2, 3, 1, 4).reshape(2, 2, S5_HALF, -1)
    cw = cw.reshape(2, hb, 2, 2 * gp * n, S5_HALF).transpose(0, 2, 1, 3, 4).reshape(2, 2, -1, S5_HALF)
    a = a.reshape(2, hb, 2, 2, 1, gp * n).transpose(0, 2, 1, 3, 4, 5)
    return bw.astype(BF16), cw.astype(BF16), a


def _s5_state_in(h):
    bsz = h.shape[0]
    return h.reshape(bsz, 2, 2, 2, S5_BLOCKS_PER_HALF, LANES).transpose(3, 1, 4, 2, 0, 5)


def _s5_state_out(st):
    bsz = st.shape[4]
    return st.transpose(4, 1, 3, 0, 2, 5).reshape(bsz, 2, 2, SSM_GROUPS, SSM_STATE)


def _s5_kernel(rows_per_step, steps, uf_ref, ub_ref, bw_ref, cw_ref, a_ref, h0_ref,
               yf_ref, yb_ref, st_ref, buf, coef, carry):
    r = rows_per_step
    nb = S5_BLOCKS_PER_HALF
    width = 2 * LANES

    @pl.when(pl.program_id(1) == 0)
    def _():
        carry[...] = h0_ref[0]
        coef[...] = jnp.broadcast_to(a_ref[0], coef.shape)

    for d, u_ref in enumerate((uf_ref, ub_ref)):
        buf[d] = jnp.dot(u_ref[...].astype(BF16), bw_ref[0, d], preferred_element_type=F32)

    chains = [(d, b) for d in range(2) for b in range(nb)]
    for g in range(r // 8):
        rows8 = slice(g * 8, (g + 1) * 8)

        def body(k, hs, rows8=rows8):
            out = []
            for (d, b), (h_re, h_im) in zip(chains, hs):
                t = k if d == 0 else steps - 1 - k
                r0 = pl.multiple_of(t * r + g * 8, 8)
                a_re, a_im = coef[d, b, 0], coef[d, b, 1]
                re_l, im_l = slice(b * width, b * width + LANES), slice(b * width + LANES, (b + 1) * width)
                n_re = a_re * h_re - a_im * h_im + buf[d, pl.ds(r0, 8), re_l]
                n_im = a_re * h_im + a_im * h_re + buf[d, pl.ds(r0, 8), im_l]
                buf[d, pl.ds(r0, 8), re_l] = n_re
                buf[d, pl.ds(r0, 8), im_l] = n_im
                out.append((n_re, n_im))
            return tuple(out)

        init = tuple((carry[d, b, 0, rows8], carry[d, b, 1, rows8]) for d, b in chains)
        final = lax.fori_loop(0, steps, body, init, unroll=8)
        for (d, b), (h_re, h_im) in zip(chains, final):
            carry[d, b, 0, rows8] = h_re
            carry[d, b, 1, rows8] = h_im

    for d, y_ref in enumerate((yf_ref, yb_ref)):
        y_ref[...] = jnp.dot(buf[d].astype(BF16), cw_ref[0, d], preferred_element_type=F32)
    st_ref[0] = carry[...]


def _s5_scan(u_tm, bw, cw, a, h0, rows_per_step, seq, steps):
    r = rows_per_step
    n_t = seq // steps
    rows = steps * r
    nb = S5_BLOCKS_PER_HALF
    wide = nb * 2 * LANES
    half4 = lambda h, t: (h, 0, 0, 0)
    half6 = lambda h, t: (h, 0, 0, 0, 0, 0)
    state = (2, nb, 2, r, LANES)
    in_specs = [pl.BlockSpec((rows, S5_HALF), lambda h, t: (t, h)),
                pl.BlockSpec((rows, S5_HALF), lambda h, t: (n_t - 1 - t, h)),
                pl.BlockSpec((1, 2, S5_HALF, wide), half4),
                pl.BlockSpec((1, 2, wide, S5_HALF), half4),
                pl.BlockSpec((1, 2, nb, 2, 1, LANES), half6),
                pl.BlockSpec((1,) + state, half6)]
    out_specs = [pl.BlockSpec((rows, S5_HALF), lambda h, t: (t, h)),
                 pl.BlockSpec((rows, S5_HALF), lambda h, t: (n_t - 1 - t, h)),
                 pl.BlockSpec((1,) + state, half6)]
    out_shape = [jax.ShapeDtypeStruct(u_tm.shape, F32), jax.ShapeDtypeStruct(u_tm.shape, F32),
                 jax.ShapeDtypeStruct((2,) + state, F32)]
    return pl.pallas_call(
        functools.partial(_s5_kernel, r, steps),
        grid=(2, n_t),
        in_specs=in_specs, out_specs=out_specs, out_shape=out_shape,
        scratch_shapes=[pltpu.VMEM((2, rows, wide), F32), pltpu.VMEM((2, nb, 2, 8, LANES), F32),
                        pltpu.VMEM(state, F32)],
        compiler_params=_cparams(("arbitrary", "arbitrary")),
        name="s5_scan",
    )(u_tm, u_tm, bw, cw, a, h0)


def _attn_kernel(q_ref, ckv_ref, kr_ref, wuk_ref, wuv_ref, o_ref, k_s, v_s):
    @pl.when(pl.program_id(1) == 0)
    def _():
        kv = ckv_ref[0].astype(BF16)
        kn = jnp.dot(kv, wuk_ref[...], preferred_element_type=F32)
        kr = kr_ref[0]
        for hd in range(MLA_HEADS):
            k_s[hd] = (kn[:, hd * HEAD_PAD:(hd + 1) * HEAD_PAD] + kr).astype(BF16)
        v_s[...] = jnp.dot(kv, wuv_ref[...], preferred_element_type=F32).astype(BF16)

    for hd in range(MLA_HEADS):
        qh = q_ref[:, hd * HEAD_PAD:(hd + 1) * HEAD_PAD]
        s = lax.dot_general(qh, k_s[hd], (((1,), (1,)), ((), ())), preferred_element_type=F32)
        p = jnp.exp(s - jnp.max(s, axis=-1, keepdims=True))
        l = jnp.sum(p, axis=-1, keepdims=True)
        o = jnp.dot(p.astype(BF16), v_s[:, hd * V_DIM:(hd + 1) * V_DIM], preferred_element_type=F32)
        o_ref[:, hd * V_DIM:(hd + 1) * V_DIM] = (o / l).astype(BF16)


def _attention(q, ckv_all, kr_all, w_uk_ext, w_uv, batch, seq):
    s_len = ckv_all.shape[1]
    tq = min(seq, ATTN_TILE)
    n_q = seq // tq
    return pl.pallas_call(
        _attn_kernel,
        grid=(batch, n_q),
        in_specs=[pl.BlockSpec((tq, MLA_HEADS * HEAD_PAD), lambda b, i: (b * n_q + i, 0)),
                  pl.BlockSpec((1, s_len, KV_RANK), lambda b, i: (b, 0, 0)),
                  pl.BlockSpec((1, s_len, HEAD_PAD), lambda b, i: (b, 0, 0)),
                  pl.BlockSpec(w_uk_ext.shape, lambda b, i: (0, 0)),
                  pl.BlockSpec(w_uv.shape, lambda b, i: (0, 0))],
        out_specs=pl.BlockSpec((tq, MLA_HEADS * V_DIM), lambda b, i: (b * n_q + i, 0)),
        out_shape=jax.ShapeDtypeStruct((batch * seq, MLA_HEADS * V_DIM), BF16),
        scratch_shapes=[pltpu.VMEM((MLA_HEADS, s_len, HEAD_PAD), BF16),
                        pltpu.VMEM((s_len, MLA_HEADS * V_DIM), BF16)],
        compiler_params=_cparams(("parallel", "arbitrary")),
        name="attention",
    )(q, ckv_all, kr_all, w_uk_ext, w_uv)


def _post_kernel(n_ctx, xc_ref, atc_ref, uc_ref, yfc_ref, ybc_ref, xl_ref, atl_ref, ul_ref, yfl_ref, ybl_ref,
                 m_ref, d_ref, wglu_ref, wos_ref, woa_ref, n2_ref, wq_ref, sk_ref,
                 x1_ref, h2_ref, s1_ref, s2_ref):
    is_ctx = pl.program_id(0) < n_ctx
    pick = lambda c_ref, l_ref: jnp.where(is_ctx, c_ref[...], l_ref[...])
    m = m_ref[0]
    y = pick(yfc_ref, yfl_ref) + pick(ybc_ref, ybl_ref) + pick(uc_ref, ul_ref) * d_ref[...]
    yg = jax.nn.gelu(y)
    gate = jax.nn.sigmoid(jnp.dot(yg.astype(BF16), wglu_ref[...], preferred_element_type=F32))
    mix = (jnp.dot((yg * gate).astype(BF16), wos_ref[...], preferred_element_type=F32)
           + jnp.dot(pick(atc_ref, atl_ref), woa_ref[...], preferred_element_type=F32))
    x1 = pick(xc_ref, xl_ref) + m[2:3] * mix
    x1_ref[...] = x1
    h2 = (_rms(x1, n2_ref[...]) * (1.0 + m[4:5]) + m[3:4]).astype(BF16)
    h2_ref[...] = h2
    qp = jnp.dot(h2, wq_ref[...], preferred_element_type=F32).astype(BF16)
    for hd in range(PEER_HEADS):
        for half, s_ref in enumerate((s1_ref, s2_ref)):
            c0 = (hd * 2 + half) * KEY_DIM
            s_ref[hd] = lax.dot_general(sk_ref[hd, half], qp[:, c0:c0 + KEY_DIM],
                                        (((1,), (1,)), ((), ())), preferred_element_type=F32)


def _post(ctx, lat, m6, ssm_d, w_glu, w_out_s, w_out_a, norm2, w_query, sub_keys):
    specs, args = [], []
    n_ctx, n_lat = (p[5] * p[6] // TOK_TILE for p in (ctx, lat))
    for pass_id, (x2d, attn, u_tm, yf, yb, batch, seq, stride) in enumerate((ctx, lat)):
        n_t = seq // TOK_TILE
        loc = (lambda i: jnp.minimum(i, n_ctx - 1)) if pass_id == 0 else (lambda i: jnp.maximum(i - n_ctx, 0))
        tok = lambda w, loc=loc: pl.BlockSpec((TOK_TILE, w), lambda i: (loc(i), 0))
        tm = pl.BlockSpec((TOK_TILE, SSM_WIDTH),
                          lambda i, loc=loc, n_t=n_t, stride=stride: (loc(i) % n_t, (loc(i) // n_t) * stride))
        specs += [tok(D_MODEL), tok(MLA_HEADS * V_DIM), tm, tm, tm]
        args += [x2d, attn, u_tm, yf, yb]
    lat_tiles_per_row = lat[6] // TOK_TILE
    mrow = lambda i: jnp.where(i < n_ctx, 0, 1 + (i - n_ctx) // lat_tiles_per_row)
    full = lambda a: pl.BlockSpec(a.shape, lambda i: (0,) * a.ndim)
    weights = [ssm_d, w_glu, w_out_s, w_out_a, norm2, w_query, sub_keys]
    total = (n_ctx + n_lat) * TOK_TILE
    tok_out = lambda w: pl.BlockSpec((TOK_TILE, w), lambda i: (i, 0))
    sc = pl.BlockSpec((PEER_HEADS, N_KEYS, TOK_TILE), lambda i: (0, 0, i))
    return pl.pallas_call(
        functools.partial(_post_kernel, n_ctx),
        grid=(n_ctx + n_lat,),
        in_specs=specs + [pl.BlockSpec((1, 6, D_MODEL), lambda i: (mrow(i), 0, 0))] + [full(w) for w in weights],
        out_specs=[tok_out(D_MODEL), tok_out(D_MODEL), sc, sc],
        out_shape=[jax.ShapeDtypeStruct((total, D_MODEL), F32), jax.ShapeDtypeStruct((total, D_MODEL), BF16),
                   jax.ShapeDtypeStruct((PEER_HEADS, N_KEYS, total), F32),
                   jax.ShapeDtypeStruct((PEER_HEADS, N_KEYS, total), F32)],
        compiler_params=_cparams(("arbitrary",)),
        name="post",
    )(*args, m6, *weights)


def _sort_pairs(lo, hi):
    def merge(lo, hi, r):
        step = r * 2
        if step < hi - lo:
            yield from merge(lo, hi, step)
            yield from merge(lo + r, hi, step)
            for i in range(lo + r, hi - r, step):
                yield (i, i + r)
        else:
            yield (lo, lo + r)

    if hi - lo >= 1:
        mid = lo + (hi - lo) // 2
        yield from _sort_pairs(lo, mid)
        yield from _sort_pairs(mid + 1, hi)
        yield from merge(lo, hi, 1)


def _sort_desc(vals):
    vals = list(vals)
    for i, j in _sort_pairs(0, len(vals) - 1):
        vals[i], vals[j] = jnp.maximum(vals[i], vals[j]), jnp.minimum(vals[i], vals[j])
    return vals


def _merge_top(a, b):
    n = len(a)
    c = [jnp.maximum(a[k], b[n - 1 - k]) for k in range(n)]
    stride = n // 2
    while stride:
        for i in range(n):
            if not i & stride:
                c[i], c[i + stride] = jnp.maximum(c[i], c[i + stride]), jnp.minimum(c[i], c[i + stride])
        stride //= 2
    return c


def _top_keys(ref, hd):
    k = PEER_TOPK
    tile = lambda key: ref[hd, key:key + 1, :].reshape(8, LANES)
    groups = [_sort_desc([tile(g * k + r) for r in range(k)]) for g in range(N_KEYS // k)]
    while len(groups) > 1:
        groups = [_merge_top(groups[i], groups[i + 1]) for i in range(0, len(groups), 2)]
    return groups[0]


def _route_kernel(s1_ref, s2_ref, o_ref, t2_ref):
    k = PEER_TOPK

    def head(hd, _):
        t1, t2 = _top_keys(s1_ref, hd), _top_keys(s2_ref, hd)
        cands = [t1[a] + t2[b] for a in range(k) for b in range(k) if (a + 1) * (b + 1) <= k]
        size = 1 << (len(cands) - 1).bit_length()
        cands += [jnp.full_like(t1[0], -jnp.inf)] * (size - len(cands))
        best = _sort_desc(cands)
        z = jnp.ones_like(best[0])
        for v in best[1:k]:
            z = z + jnp.exp(v - best[0])
        stats = (best[k - 1], t1[0], t2[0], 1.0 / z)
        for c in range(8):
            lanes = slice(c * LANES, (c + 1) * LANES)
            for n, v in enumerate(stats):
                o_ref[n, hd, :, lanes] = v[c:c + 1]
            for b in range(k):
                t2_ref[hd, b:b + 1, lanes] = t2[b][c:c + 1]
        return 0

    lax.fori_loop(0, PEER_HEADS, head, 0)


def _route(s1d, s2d):
    ntok = s1d.shape[2]
    n_rows = ntok // LANES
    blk = 8 * LANES
    spec = pl.BlockSpec((PEER_HEADS, N_KEYS, 8 * LANES), lambda i: (0, 0, i))
    return pl.pallas_call(
        _route_kernel,
        grid=(n_rows // 8,),
        in_specs=[spec, spec],
        out_specs=[pl.BlockSpec((4, PEER_HEADS, 1, blk), lambda i: (0, 0, 0, i)),
                   pl.BlockSpec((PEER_HEADS, PEER_TOPK, blk), lambda i: (0, 0, i))],
        out_shape=[jax.ShapeDtypeStruct((4, PEER_HEADS, 1, ntok), F32),
                   jax.ShapeDtypeStruct((PEER_HEADS, PEER_TOPK, ntok), F32)],
        compiler_params=_cparams(("parallel",)),
        name="peer_route",
    )(s1d, s2d)


def _peer_kernel(h2_ref, s1_ref, s2_ref, t2_ref, st_ref, u_ref, v_ref, x1_ref, m_ref, fn_ref, yc_ref, yl_ref,
                 e2_s, act_s, aw_s, acc_s, *, ctx_tiles):
    k = pl.program_id(1)
    chunks = [slice(c * LANES, (c + 1) * LANES) for c in range(PEER_TN // LANES)]

    @pl.when(k == 0)
    def _():
        for hd in range(PEER_HEADS):
            e2_s[hd] = jnp.exp(s2_ref[hd] - st_ref[2, hd])
        acc_s[...] = jnp.zeros_like(acc_s)

    floor, gain = [], []
    for hd in range(PEER_HEADS):
        s1 = s1_ref[hd]
        low = jnp.full_like(s1, jnp.inf)
        for b in range(PEER_TOPK):
            t2b = t2_ref[hd, b:b + 1, :]
            low = jnp.where(s1 + t2b >= st_ref[0, hd], t2b, low)
        floor.append(low)
        gain.append(jnp.exp(s1 - st_ref[1, hd]) * (0.5 * st_ref[3, hd]))

    act_s[...] = lax.dot_general(u_ref[...].astype(BF16), h2_ref[...], (((1,), (1,)), ((), ())),
                                 preferred_element_type=F32)
    for cs in chunks:
        for ii in range(PEER_ROWS):
            rows = slice(ii * N_KEYS, (ii + 1) * N_KEYS)
            w = None
            for hd in range(PEER_HEADS):
                hit = s2_ref[hd, :, cs] >= floor[hd][ii:ii + 1, cs]
                term = jnp.where(hit, e2_s[hd, :, cs], 0.0) * gain[hd][ii:ii + 1, cs]
                w = term if w is None else w + term
            x = act_s[rows, cs]
            t = jnp.tanh(x * (GELU_C0 + GELU_C1 * (x * x)))
            aw_s[rows, cs] = ((x * w) * (1.0 + t)).astype(BF16)
    acc_s[...] += lax.dot_general(aw_s[...], v_ref[...].astype(BF16), (((0,), (0,)), ((), ())),
                                  preferred_element_type=F32)

    last = k == pl.num_programs(1) - 1
    is_ctx = pl.program_id(0) < ctx_tiles
    for cond, y_ref in ((is_ctx, yc_ref), (jnp.logical_not(is_ctx), yl_ref)):
        @pl.when(last & cond)
        def _(y_ref=y_ref):
            y_ref[...] = _rms(x1_ref[...] + m_ref[0][5:6] * acc_s[...], fn_ref[...])


def _peer(h2, s1t, s2t, t2, stats, u_tab, v_tab, x1, m6, mrow, final_norm, ctx_tokens):
    ntok = h2.shape[0]
    tn, te = PEER_TN, PEER_TE
    nc = ctx_tokens // tn
    tok = pl.BlockSpec((tn, D_MODEL), lambda t, k: (t, 0))
    tab = pl.BlockSpec((te, D_MODEL), lambda t, k: (k, 0))
    keys = pltpu.VMEM((PEER_HEADS, N_KEYS, tn), F32)
    return pl.pallas_call(
        functools.partial(_peer_kernel, ctx_tiles=nc),
        grid=(ntok // tn, N_EXPERTS // te),
        in_specs=[tok,
                  pl.BlockSpec((PEER_HEADS, PEER_ROWS, tn), lambda t, k: (0, k, t)),
                  pl.BlockSpec((PEER_HEADS, N_KEYS, tn), lambda t, k: (0, 0, t)),
                  pl.BlockSpec((PEER_HEADS, PEER_TOPK, tn), lambda t, k: (0, 0, t)),
                  pl.BlockSpec((4, PEER_HEADS, 1, tn), lambda t, k: (0, 0, 0, t)),
                  tab, tab, tok,
                  pl.BlockSpec((1, 6, D_MODEL), lambda t, k: (mrow(t), 0, 0)),
                  pl.BlockSpec((1, D_MODEL), lambda t, k: (0, 0))],
        out_specs=[pl.BlockSpec((tn, D_MODEL), lambda t, k: (jnp.minimum(t, nc - 1), 0)),
                   pl.BlockSpec((tn, D_MODEL), lambda t, k: (jnp.maximum(t - nc, 0), 0))],
        out_shape=[jax.ShapeDtypeStruct((ctx_tokens, D_MODEL), F32),
                   jax.ShapeDtypeStruct((ntok - ctx_tokens, D_MODEL), F32)],
        scratch_shapes=[keys, pltpu.VMEM((te, tn), F32), pltpu.VMEM((te, tn), BF16),
                        pltpu.VMEM((tn, D_MODEL), F32)],
        compiler_params=_cparams(("arbitrary", "arbitrary")),
        name="peer_dense",
    )(h2, s1t, s2t, t2, stats, u_tab, v_tab, x1, m6, final_norm)


def _rot_cols(w):
    j = np.arange(QK_ROPE)
    first = (j % (QK_ROPE // 2)) < (QK_ROPE // 4)
    perm = np.where(first, j + QK_ROPE // 4, j - QK_ROPE // 4)
    sign = np.where(first, -1.0, 1.0).astype(np.float32)
    return w[..., perm] * sign


def _rope_slot(w_rope):
    pad = [(0, 0)] * (w_rope.ndim - 1)
    return jnp.pad(w_rope, pad + [(QK_NOPE, HEAD_PAD - QK_NOPE - QK_ROPE)])


def _rope_tables(n_tokens):
    n_rows = n_tokens // GRID_W
    rows = jnp.repeat(jnp.arange(n_rows, dtype=F32), GRID_W)
    cols = jnp.tile(jnp.arange(GRID_W, dtype=F32), n_rows)
    half = QK_ROPE // 2
    inv_freq = 1.0 / (ROPE_BASE ** (jnp.arange(0, half, 2, dtype=F32) / half))
    ang_r = rows[:, None] * inv_freq
    ang_c = cols[:, None] * inv_freq
    ang = jnp.concatenate([ang_r, ang_r, ang_c, ang_c], axis=-1)
    lead = (QK_NOPE, HEAD_PAD - QK_NOPE - QK_ROPE)
    cos = jnp.pad(jnp.cos(ang), [(0, 0), lead], constant_values=1.0)
    sin = jnp.pad(jnp.sin(ang), [(0, 0), lead])
    return cos, sin


def kernel(x_prompt, x_sample, c, cache_ckv, cache_krope, state_ssm, c_ctx, w_mod, b_mod, norm1, w_in,
           ssm_lam_re, ssm_lam_im, ssm_log_dt, ssm_b_re, ssm_b_im, ssm_c_re, ssm_c_im, ssm_d, w_glu,
           q_norm, w_uq, kv_norm, w_uk, w_uv, w_out, norm2, w_query, sub_keys, u_table, v_table,
           final_norm):
    bc, tc_len, _ = x_prompt.shape
    bl, tl_len, _ = x_sample.shape
    l = 0
    row = lambda a: a.reshape(1, -1)
    assert w_mod.shape[0] == 1
    big = lambda a: a.reshape(a.shape[1:])

    n_mod = 8
    cvec = jnp.concatenate([c_ctx[None], c, jnp.zeros((n_mod - 1 - bl, D_MODEL), F32)], 0)
    m6 = _modulation(cvec, big(w_mod), row(b_mod[l])).reshape(n_mod, 6, D_MODEL)

    wi = w_in[l]
    o_kr = SSM_WIDTH + Q_RANK + KV_RANK
    w_kr = wi[:, o_kr:]
    w_in_ext = jnp.concatenate([wi[:, :o_kr], _rope_slot(w_kr), _rope_slot(_rot_cols(w_kr))], 1).astype(BF16)
    wq3 = w_uq[l].reshape(Q_RANK, MLA_HEADS, QK_NOPE + QK_ROPE)
    wq_main = jnp.pad(wq3, [(0, 0), (0, 0), (0, HEAD_PAD - QK_NOPE - QK_ROPE)])
    wq_rot = _rope_slot(_rot_cols(wq3[..., QK_NOPE:]))
    w_uq_ext = jnp.concatenate([wq_main.reshape(Q_RANK, -1), wq_rot.reshape(Q_RANK, -1)], 1).astype(BF16)
    w_uk_ext = jnp.pad(w_uk[l].reshape(KV_RANK, MLA_HEADS, QK_NOPE),
                       [(0, 0), (0, 0), (0, HEAD_PAD - QK_NOPE)]).reshape(KV_RANK, -1).astype(BF16)
    w_uv_b = w_uv[l].astype(BF16)
    w_glu_b = w_glu[l].astype(BF16)
    w_out_s = w_out[l][:SSM_WIDTH].astype(BF16)
    w_out_a = w_out[l][SSM_WIDTH:].astype(BF16)
    w_query_b = w_query[l].astype(BF16)
    sub_keys_b = sub_keys[l].astype(BF16)
    u_tab, v_tab = big(u_table), big(v_table)
    cos, sin = _rope_tables(tl_len)

    bw, cw, a_blk = _s5_params(ssm_lam_re[l], ssm_lam_im[l], ssm_log_dt[l], ssm_b_re[l], ssm_b_im[l],
                               ssm_c_re[l], ssm_c_im[l])

    ctx_row = lambda b: 0
    lat_row = lambda b: 1 + b
    xc = x_prompt.reshape(bc * tc_len, D_MODEL)
    xl = x_sample.reshape(bl * tl_len, D_MODEL)

    u_c, q_c, ckv_c, kr_c = _pre(xc, m6, ctx_row, row(norm1[l]), w_in_ext, row(q_norm[l]), row(kv_norm[l]),
                                 w_uq_ext[:, :MLA_HEADS * HEAD_PAD], bc, tc_len, None)
    u_l, q_l, ckv_l, kr_l = _pre(xl, m6, lat_row, row(norm1[l]), w_in_ext, row(q_norm[l]), row(kv_norm[l]),
                                 w_uq_ext, bl, tl_len, (cos, sin), pad_rows=True)

    rl = 2 * bl
    h0_c = jnp.zeros((2, 2, S5_BLOCKS_PER_HALF, 2, bc, LANES), F32)
    yf_c, yb_c, st_c = _s5_scan(u_c.reshape(tc_len * bc, SSM_WIDTH), bw, cw, a_blk, h0_c, bc, tc_len,
                                S5_TILE_ROWS // bc)
    h0_l = _s5_state_in(state_ssm[:, l])
    h0_l = jnp.stack([h0_l, jnp.zeros_like(h0_l)], axis=-2).reshape(h0_l.shape[:4] + (rl, LANES))
    yf_l, yb_l, _ = _s5_scan(u_l.reshape(tl_len * rl, SSM_WIDTH), bw, cw, a_blk, h0_l, rl, tl_len,
                             S5_TILE_ROWS // rl)

    at_c = _attention(q_c, ckv_c.reshape(bc, tc_len, KV_RANK), kr_c.reshape(bc, tc_len, HEAD_PAD),
                      w_uk_ext, w_uv_b, bc, tc_len)
    ckv_all = jnp.concatenate([cache_ckv[:, l], ckv_l.reshape(bl, tl_len, KV_RANK)], 1)
    kr_all = jnp.concatenate([_rope_slot(cache_krope[:, l]), kr_l.reshape(bl, tl_len, HEAD_PAD)], 1)
    at_l = _attention(q_l, ckv_all, kr_all, w_uk_ext, w_uv_b, bl, tl_len)

    tm_c = lambda a: a.reshape(tc_len, bc * SSM_WIDTH)
    tm_l = lambda a: a.reshape(tl_len, rl * SSM_WIDTH)
    ntok = bc * tc_len + bl * tl_len
    x1, h2, s1t, s2t = _post((xc, at_c, tm_c(u_c), tm_c(yf_c), tm_c(yb_c), bc, tc_len, 1),
                             (xl, at_l, u_l, tm_l(yf_l), tm_l(yb_l), bl, tl_len, 2),
                             m6, row(ssm_d[l]), w_glu_b, w_out_s, w_out_a, row(norm2[l]), w_query_b, sub_keys_b)

    dense = lambda a: a.reshape(PEER_HEADS, N_KEYS, ntok // LANES, LANES)
    stats, t2 = _route(s1t, s2t)
    ctx_peer_tiles = bc * tc_len // PEER_TN
    lat_peer_tiles = tl_len // PEER_TN
    peer_row = lambda t: jnp.where(t < ctx_peer_tiles, 0, 1 + (t - ctx_peer_tiles) // lat_peer_tiles)
    y_c, y_l = _peer(h2, s1t, s2t, t2, stats, u_tab, v_tab, x1, m6, peer_row, row(final_norm), bc * tc_len)

    y_prompt = y_c.reshape(bc, tc_len, D_MODEL)
    y_sample = y_l.reshape(bl, tl_len, D_MODEL)
    new_ckv = ckv_c.reshape(bc, 1, tc_len, KV_RANK)
    new_krope = kr_c[:, QK_NOPE:QK_NOPE + QK_ROPE].reshape(bc, 1, tc_len, QK_ROPE)
    new_ssm = _s5_state_out(st_c)[:, None]
    return (y_prompt, y_sample, new_ckv, new_krope, new_ssm)
```

```python
import functools
import math

import jax
import jax.numpy as jnp
import numpy as np
from jax import lax
from jax.experimental import pallas as pl
from jax.experimental.pallas import tpu as pltpu

F32 = jnp.float32
BF16 = jnp.bfloat16

D_MODEL = 1024
GRID_W = 64
EPS = 1e-6
SSM_WIDTH = 512
SSM_GROUP = 16
SSM_GROUPS = 32
SSM_STATE = 64
MLA_HEADS = 8
QK_NOPE = 64
QK_ROPE = 32
V_DIM = 64
Q_RANK = 384
KV_RANK = 256
ROPE_BASE = 10000.0
N_KEYS = 128
N_EXPERTS = N_KEYS * N_KEYS
PEER_HEADS = 8
PEER_TOPK = 16
KEY_DIM = 128

LANES = 128
HEAD_PAD = 128
TOK_TILE = 256
ATTN_TILE = 512
PRE_TILE = 512
S5_PAIR = 2
S5_BLOCKS = SSM_GROUPS // S5_PAIR
S5_HALF = SSM_WIDTH // 2
S5_BLOCKS_PER_HALF = S5_BLOCKS // 2
S5_TILE_ROWS = 1024
PEER_TN = 512
PEER_ROWS = 8
PEER_TE = PEER_ROWS * N_KEYS
GELU_C0 = math.sqrt(2.0 / math.pi)
GELU_C1 = 0.044715 * GELU_C0
VMEM_LIMIT = 54 * 1024 * 1024


def _cparams(sem):
    return pltpu.CompilerParams(dimension_semantics=sem, vmem_limit_bytes=VMEM_LIMIT)


def _rms(x, g):
    return x * lax.rsqrt(jnp.mean(x * x, axis=-1, keepdims=True) + EPS) * g


def _mod_kernel(c_ref, w_ref, b_ref, o_ref):
    o_ref[...] = jnp.dot(jax.nn.silu(c_ref[...]), w_ref[...], preferred_element_type=F32) + b_ref[...]


def _modulation(cvec, w_mod, b_mod):
    rows, d = cvec.shape
    n = w_mod.shape[1]
    tn = 1536
    return pl.pallas_call(
        _mod_kernel,
        grid=(n // tn,),
        in_specs=[pl.BlockSpec((rows, d), lambda j: (0, 0)),
                  pl.BlockSpec((d, tn), lambda j: (0, j)),
                  pl.BlockSpec((1, tn), lambda j: (0, j))],
        out_specs=pl.BlockSpec((rows, tn), lambda j: (0, j)),
        out_shape=jax.ShapeDtypeStruct((rows, n), F32),
        compiler_params=_cparams(("parallel",)),
        name="modulation",
    )(cvec, w_mod, b_mod)


def _pre_kernel(use_rope, pad_rows, x_ref, m_ref, n1_ref, win_ref, qn_ref, kvn_ref, wuq_ref, *rest):
    if use_rope:
        cos_ref, sin_ref, u_ref, q_ref, ckv_ref, kr_ref = rest
    else:
        u_ref, q_ref, ckv_ref, kr_ref = rest
    m = m_ref[0]
    h = _rms(x_ref[...], n1_ref[...]) * (1.0 + m[1:2]) + m[0:1]
    z = jnp.dot(h.astype(BF16), win_ref[...], preferred_element_type=F32)
    u_ref[:, :SSM_WIDTH] = z[:, :SSM_WIDTH]
    if pad_rows:
        u_ref[:, SSM_WIDTH:] = jnp.zeros((u_ref.shape[0], SSM_WIDTH), F32)
    o_q, o_kv, o_kr = SSM_WIDTH, SSM_WIDTH + Q_RANK, SSM_WIDTH + Q_RANK + KV_RANK
    cqn = _rms(z[:, o_q:o_kv], qn_ref[...])
    qq = jnp.dot(cqn.astype(BF16), wuq_ref[...], preferred_element_type=F32)
    ckv_ref[...] = _rms(z[:, o_kv:o_kr], kvn_ref[...])
    kr = z[:, o_kr:o_kr + HEAD_PAD]
    scale = (QK_NOPE + QK_ROPE) ** -0.5
    nq = MLA_HEADS * HEAD_PAD
    if use_rope:
        cos, sin = cos_ref[...], sin_ref[...]
        kr = kr * cos + z[:, o_kr + HEAD_PAD:o_kr + 2 * HEAD_PAD] * sin
        for hd in range(MLA_HEADS):
            sl = slice(hd * HEAD_PAD, (hd + 1) * HEAD_PAD)
            qh = qq[:, sl] * cos + qq[:, nq + hd * HEAD_PAD:nq + (hd + 1) * HEAD_PAD] * sin
            q_ref[:, sl] = (qh * scale).astype(BF16)
    else:
        q_ref[...] = (qq[:, :nq] * scale).astype(BF16)
    kr_ref[...] = kr


def _pre(x2d, m6, mrow, norm1, w_in_ext, q_norm, kv_norm, w_uq_ext, batch, seq, rope, pad_rows=False):
    ntok = batch * seq
    u_cols = SSM_WIDTH * (2 if pad_rows else 1)
    tile = min(seq, PRE_TILE)
    n_t = seq // tile
    use_rope = rope is not None
    full = lambda a: pl.BlockSpec(a.shape, lambda i: (0,) * a.ndim)
    in_specs = [pl.BlockSpec((tile, D_MODEL), lambda i: (i, 0)),
                pl.BlockSpec((1, 6, D_MODEL), lambda i: (mrow(i // n_t), 0, 0)),
                full(norm1), full(w_in_ext), full(q_norm), full(kv_norm), full(w_uq_ext)]
    args = [x2d, m6, norm1, w_in_ext, q_norm, kv_norm, w_uq_ext]
    if use_rope:
        in_specs += [pl.BlockSpec((tile, HEAD_PAD), lambda i: (i % n_t, 0))] * 2
        args += list(rope)
    out_specs = [pl.BlockSpec((tile, u_cols), lambda i: (i % n_t, i // n_t)),
                 pl.BlockSpec((tile, MLA_HEADS * HEAD_PAD), lambda i: (i, 0)),
                 pl.BlockSpec((tile, KV_RANK), lambda i: (i, 0)),
                 pl.BlockSpec((tile, HEAD_PAD), lambda i: (i, 0))]
    out_shape = [jax.ShapeDtypeStruct((seq, batch * u_cols), F32),
                 jax.ShapeDtypeStruct((ntok, MLA_HEADS * HEAD_PAD), BF16),
                 jax.ShapeDtypeStruct((ntok, KV_RANK), F32),
                 jax.ShapeDtypeStruct((ntok, HEAD_PAD), F32)]
    return pl.pallas_call(
        functools.partial(_pre_kernel, use_rope, pad_rows),
        grid=(ntok // tile,),
        in_specs=in_specs, out_specs=out_specs, out_shape=out_shape,
        compiler_params=_cparams(("parallel",)),
        name="pre_rope" if use_rope else "pre",
    )(*args)


def _s5_param_kernel(lr_ref, li_ref, ldt_ref, lrx_ref, lix_ref, ldtx_ref, bre_ref, bim_ref, cim_ref,
                     abr_ref, abi_ref, bfr_ref, bfi_ref, ncim_ref):
    def disc(lr, li, ldt):
        dt = jnp.exp(ldt)
        mag = jnp.exp(lr * dt)
        ab_re, ab_im = mag * jnp.cos(li * dt), mag * jnp.sin(li * dt)
        den = lr * lr + li * li
        br, bi = lr / den, -li / den
        ar = ab_re - 1.0
        return ab_re, ab_im, ar * br - ab_im * bi, ar * bi + ab_im * br

    ab_re, ab_im, _, _ = disc(lr_ref[...], li_ref[...], ldt_ref[...])
    abr_ref[...] = ab_re
    abi_ref[...] = ab_im
    _, _, f_re, f_im = disc(lrx_ref[...], lix_ref[...], ldtx_ref[...])
    b_re, b_im = bre_ref[...], bim_ref[...]
    bfr_ref[...] = f_re * b_re - f_im * b_im
    bfi_ref[...] = f_re * b_im + f_im * b_re
    ncim_ref[...] = -cim_ref[...]


def _s5_params(lam_re, lam_im, log_dt, b_re, b_im, c_re, c_im):
    dg = 2 * SSM_GROUPS
    n, p = SSM_STATE, SSM_GROUP
    lr = lam_re.reshape(dg, n)
    li = lam_im.reshape(dg, n)
    ldt = jnp.broadcast_to(log_dt.reshape(dg, 1), (dg, n))
    rep = lambda a: jnp.repeat(a, p, axis=1)
    args = [lr, li, ldt, rep(lr), rep(li), rep(ldt),
            b_re.reshape(dg, n * p), b_im.reshape(dg, n * p), c_im.reshape(dg, p * n)]
    small = jax.ShapeDtypeStruct((dg, n), F32)
    big = jax.ShapeDtypeStruct((dg, n * p), F32)
    ab_re, ab_im, bf_re, bf_im, ncim = pl.pallas_call(
        _s5_param_kernel, out_shape=[small, small, big, big, big], name="s5_params")(*args)

    nb, gp = S5_BLOCKS, S5_PAIR
    eye = jnp.eye(gp, dtype=F32)
    pos = np.arange(nb) % S5_BLOCKS_PER_HALF
    bf = jnp.stack([bf_re, bf_im], 0).reshape(2, 2, nb, gp, n, p)
    bc = jnp.einsum('adbgnp,gh->bdgpahn', bf, eye).reshape(nb, 2, gp * p, 2 * gp * n)
    bw = jnp.zeros((nb, 2, S5_BLOCKS_PER_HALF, gp * p, 2 * gp * n), F32)
    bw = bw.at[np.arange(nb), :, pos].set(bc).reshape(nb, 2, S5_HALF, 2 * gp * n)
    cc = jnp.stack([c_re.reshape(2, nb, gp, p, n), ncim.reshape(2, nb, gp, p, n)], 0)
    cc = jnp.einsum('adbgpn,gh->bdagnhp', cc, eye).reshape(nb, 2, 2 * gp * n, gp * p)
    cw = jnp.zeros((nb, 2, 2 * gp * n, S5_BLOCKS_PER_HALF, gp * p), F32)
    cw = cw.at[np.arange(nb), :, :, pos].set(cc).reshape(nb, 2, 2 * gp * n, S5_HALF)
    a = jnp.stack([ab_re, ab_im], 0).reshape(2, 2, nb, gp * n).transpose(2, 1, 0, 3)
    hb = S5_BLOCKS_PER_HALF
    bw = bw.reshape(2, hb, 2, S5_HALF, 2 * gp * n).transpose(0, 2, 3, 1, 4).reshape(2, 2, S5_HALF, -1)
    cw = cw.reshape(2, hb, 2, 2 * gp * n, S5_HALF).transpose(0, 2, 1, 3, 4).reshape(2, 2, -1, S5_HALF)
    a = a.reshape(2, hb, 2, 2, 1, gp * n).transpose(0, 2, 1, 3, 4, 5)
    return bw.astype(BF16), cw.astype(BF16), a


def _s5_state_in(h):
    bsz = h.shape[0]
    return h.reshape(bsz, 2, 2, 2, S5_BLOCKS_PER_HALF, LANES).transpose(3, 1, 4, 2, 0, 5)


def _s5_state_out(st):
    bsz = st.shape[4]
    return st.transpose(4, 1, 3, 0, 2, 5).reshape(bsz, 2, 2, SSM_GROUPS, SSM_STATE)


def _s5_kernel(rows_per_step, steps, uf_ref, ub_ref, bw_ref, cw_ref, a_ref, h0_ref,
               yf_ref, yb_ref, st_ref, buf, coef, carry):
    r = rows_per_step
    nb = S5_BLOCKS_PER_HALF
    width = 2 * LANES

    @pl.when(pl.program_id(1) == 0)
    def _():
        carry[...] = h0_ref[0]
        coef[...] = jnp.broadcast_to(a_ref[0], coef.shape)

    for d, u_ref in enumerate((uf_ref, ub_ref)):
        buf[d] = jnp.dot(u_ref[...].astype(BF16), bw_ref[0, d], preferred_element_type=F32)

    chains = [(d, b) for d in range(2) for b in range(nb)]
    for g in range(r // 8):
        rows8 = slice(g * 8, (g + 1) * 8)

        def body(k, hs, rows8=rows8):
            out = []
            for (d, b), (h_re, h_im) in zip(chains, hs):
                t = k if d == 0 else steps - 1 - k
                r0 = pl.multiple_of(t * r + g * 8, 8)
                a_re, a_im = coef[d, b, 0], coef[d, b, 1]
                re_l, im_l = slice(b * width, b * width + LANES), slice(b * width + LANES, (b + 1) * width)
                n_re = a_re * h_re - a_im * h_im + buf[d, pl.ds(r0, 8), re_l]
                n_im = a_re * h_im + a_im * h_re + buf[d, pl.ds(r0, 8), im_l]
                buf[d, pl.ds(r0, 8), re_l] = n_re
                buf[d, pl.ds(r0, 8), im_l] = n_im
                out.append((n_re, n_im))
            return tuple(out)

        init = tuple((carry[d, b, 0, rows8], carry[d, b, 1, rows8]) for d, b in chains)
        final = lax.fori_loop(0, steps, body, init, unroll=8)
        for (d, b), (h_re, h_im) in zip(chains, final):
            carry[d, b, 0, rows8] = h_re
            carry[d, b, 1, rows8] = h_im

    for d, y_ref in enumerate((yf_ref, yb_ref)):
        y = jnp.dot(buf[d].astype(BF16), cw_ref[0, d], preferred_element_type=F32)
        y_ref[...] = y.reshape(steps, r * S5_HALF)
    st_ref[0] = carry[...]


def _s5_scan(u_tm, bw, cw, a, h0, rows_per_step, seq, steps):
    r = rows_per_step
    n_t = seq // steps
    rows = steps * r
    nb = S5_BLOCKS_PER_HALF
    wide = nb * 2 * LANES
    half4 = lambda h, t: (h, 0, 0, 0)
    half6 = lambda h, t: (h, 0, 0, 0, 0, 0)
    state = (2, nb, 2, r, LANES)
    in_specs = [pl.BlockSpec((rows, S5_HALF), lambda h, t: (t, h)),
                pl.BlockSpec((rows, S5_HALF), lambda h, t: (n_t - 1 - t, h)),
                pl.BlockSpec((1, 2, S5_HALF, wide), half4),
                pl.BlockSpec((1, 2, wide, S5_HALF), half4),
                pl.BlockSpec((1, 2, nb, 2, 1, LANES), half6),
                pl.BlockSpec((1,) + state, half6)]
    y_shape = (seq, 2 * r * S5_HALF)
    out_specs = [pl.BlockSpec((steps, r * S5_HALF), lambda h, t: (t, h)),
                 pl.BlockSpec((steps, r * S5_HALF), lambda h, t: (n_t - 1 - t, h)),
                 pl.BlockSpec((1,) + state, half6)]
    out_shape = [jax.ShapeDtypeStruct(y_shape, F32), jax.ShapeDtypeStruct(y_shape, F32),
                 jax.ShapeDtypeStruct((2,) + state, F32)]
    return pl.pallas_call(
        functools.partial(_s5_kernel, r, steps),
        grid=(2, n_t),
        in_specs=in_specs, out_specs=out_specs, out_shape=out_shape,
        scratch_shapes=[pltpu.VMEM((2, rows, wide), F32), pltpu.VMEM((2, nb, 2, 8, LANES), F32),
                        pltpu.VMEM(state, F32)],
        compiler_params=_cparams(("arbitrary", "arbitrary")),
        name="s5_scan",
    )(u_tm, u_tm, bw, cw, a, h0)


def _attn_kernel(q_ref, ckv_ref, kr_ref, wuk_ref, wuv_ref, o_ref, k_s, v_s):
    @pl.when(pl.program_id(1) == 0)
    def _():
        kv = ckv_ref[0].astype(BF16)
        kn = jnp.dot(kv, wuk_ref[...], preferred_element_type=F32)
        kr = kr_ref[0]
        for hd in range(MLA_HEADS):
            k_s[hd] = (kn[:, hd * HEAD_PAD:(hd + 1) * HEAD_PAD] + kr).astype(BF16)
        v_s[...] = jnp.dot(kv, wuv_ref[...], preferred_element_type=F32).astype(BF16)

    for hd in range(MLA_HEADS):
        qh = q_ref[:, hd * HEAD_PAD:(hd + 1) * HEAD_PAD]
        s = lax.dot_general(qh, k_s[hd], (((1,), (1,)), ((), ())), preferred_element_type=F32)
        p = jnp.exp(s - jnp.max(s, axis=-1, keepdims=True))
        l = jnp.sum(p, axis=-1, keepdims=True)
        o = jnp.dot(p.astype(BF16), v_s[:, hd * V_DIM:(hd + 1) * V_DIM], preferred_element_type=F32)
        o_ref[:, hd * V_DIM:(hd + 1) * V_DIM] = (o / l).astype(BF16)


def _attention(q, ckv_all, kr_all, w_uk_ext, w_uv, batch, seq):
    s_len = ckv_all.shape[1]
    tq = min(seq, ATTN_TILE)
    n_q = seq // tq
    return pl.pallas_call(
        _attn_kernel,
        grid=(batch, n_q),
        in_specs=[pl.BlockSpec((tq, MLA_HEADS * HEAD_PAD), lambda b, i: (b * n_q + i, 0)),
                  pl.BlockSpec((1, s_len, KV_RANK), lambda b, i: (b, 0, 0)),
                  pl.BlockSpec((1, s_len, HEAD_PAD), lambda b, i: (b, 0, 0)),
                  pl.BlockSpec(w_uk_ext.shape, lambda b, i: (0, 0)),
                  pl.BlockSpec(w_uv.shape, lambda b, i: (0, 0))],
        out_specs=pl.BlockSpec((tq, MLA_HEADS * V_DIM), lambda b, i: (b * n_q + i, 0)),
        out_shape=jax.ShapeDtypeStruct((batch * seq, MLA_HEADS * V_DIM), BF16),
        scratch_shapes=[pltpu.VMEM((MLA_HEADS, s_len, HEAD_PAD), BF16),
                        pltpu.VMEM((s_len, MLA_HEADS * V_DIM), BF16)],
        compiler_params=_cparams(("parallel", "arbitrary")),
        name="attention",
    )(q, ckv_all, kr_all, w_uk_ext, w_uv)


def _post_kernel(n_ctx, xc_ref, atc_ref, uc_ref, yc0_ref, yc1_ref, yc2_ref, yc3_ref,
                 xl_ref, atl_ref, ul_ref, yl0_ref, yl1_ref, yl2_ref, yl3_ref,
                 m_ref, d_ref, wglu_ref, wos_ref, woa_ref, n2_ref, wq_ref, sk_ref,
                 x1_ref, h2_ref, s1_ref, s2_ref):
    is_ctx = pl.program_id(0) < n_ctx
    pick = lambda c_ref, l_ref: jnp.where(is_ctx, c_ref[...], l_ref[...])
    m = m_ref[0]
    y_lo = pick(yc0_ref, yl0_ref) + pick(yc2_ref, yl2_ref)
    y_hi = pick(yc1_ref, yl1_ref) + pick(yc3_ref, yl3_ref)
    y = jnp.concatenate([y_lo, y_hi], axis=1) + pick(uc_ref, ul_ref) * d_ref[...]
    yg = jax.nn.gelu(y)
    gate = jax.nn.sigmoid(jnp.dot(yg.astype(BF16), wglu_ref[...], preferred_element_type=F32))
    mix = (jnp.dot((yg * gate).astype(BF16), wos_ref[...], preferred_element_type=F32)
           + jnp.dot(pick(atc_ref, atl_ref), woa_ref[...], preferred_element_type=F32))
    x1 = pick(xc_ref, xl_ref) + m[2:3] * mix
    x1_ref[...] = x1
    h2 = (_rms(x1, n2_ref[...]) * (1.0 + m[4:5]) + m[3:4]).astype(BF16)
    h2_ref[...] = h2
    qp = jnp.dot(h2, wq_ref[...], preferred_element_type=F32).astype(BF16)
    for hd in range(PEER_HEADS):
        for half, s_ref in enumerate((s1_ref, s2_ref)):
            c0 = (hd * 2 + half) * KEY_DIM
            s_ref[hd] = lax.dot_general(sk_ref[hd, half], qp[:, c0:c0 + KEY_DIM],
                                        (((1,), (1,)), ((), ())), preferred_element_type=F32)


def _post(ctx, lat, m6, ssm_d, w_glu, w_out_s, w_out_a, norm2, w_query, sub_keys):
    specs, args = [], []
    n_ctx, n_lat = (p[5] * p[6] // TOK_TILE for p in (ctx, lat))
    for pass_id, (x2d, attn, u_tm, yf, yb, batch, seq, stride) in enumerate((ctx, lat)):
        n_t = seq // TOK_TILE
        loc = (lambda i: jnp.minimum(i, n_ctx - 1)) if pass_id == 0 else (lambda i: jnp.maximum(i - n_ctx, 0))
        tok = lambda w, loc=loc: pl.BlockSpec((TOK_TILE, w), lambda i: (loc(i), 0))
        tm = pl.BlockSpec((TOK_TILE, SSM_WIDTH),
                          lambda i, loc=loc, n_t=n_t, stride=stride: (loc(i) % n_t, (loc(i) // n_t) * stride))
        rows_t = yf.shape[1] // (2 * S5_HALF)
        halves = [pl.BlockSpec((TOK_TILE, S5_HALF),
                               lambda i, loc=loc, n_t=n_t, stride=stride, hf=hf, rows_t=rows_t:
                               (loc(i) % n_t, hf * rows_t + (loc(i) // n_t) * stride)) for hf in range(2)]
        specs += [tok(D_MODEL), tok(MLA_HEADS * V_DIM), tm] + halves + halves
        args += [x2d, attn, u_tm, yf, yf, yb, yb]
    lat_tiles_per_row = lat[6] // TOK_TILE
    mrow = lambda i: jnp.where(i < n_ctx, 0, 1 + (i - n_ctx) // lat_tiles_per_row)
    full = lambda a: pl.BlockSpec(a.shape, lambda i: (0,) * a.ndim)
    weights = [ssm_d, w_glu, w_out_s, w_out_a, norm2, w_query, sub_keys]
    total = (n_ctx + n_lat) * TOK_TILE
    tok_out = lambda w: pl.BlockSpec((TOK_TILE, w), lambda i: (i, 0))
    sc = pl.BlockSpec((PEER_HEADS, N_KEYS, TOK_TILE), lambda i: (0, 0, i))
    return pl.pallas_call(
        functools.partial(_post_kernel, n_ctx),
        grid=(n_ctx + n_lat,),
        in_specs=specs + [pl.BlockSpec((1, 6, D_MODEL), lambda i: (mrow(i), 0, 0))] + [full(w) for w in weights],
        out_specs=[tok_out(D_MODEL), tok_out(D_MODEL), sc, sc],
        out_shape=[jax.ShapeDtypeStruct((total, D_MODEL), F32), jax.ShapeDtypeStruct((total, D_MODEL), BF16),
                   jax.ShapeDtypeStruct((PEER_HEADS, N_KEYS, total), F32),
                   jax.ShapeDtypeStruct((PEER_HEADS, N_KEYS, total), F32)],
        compiler_params=_cparams(("arbitrary",)),
        name="post",
    )(*args, m6, *weights)


def _sort_pairs(lo, hi):
    def merge(lo, hi, r):
        step = r * 2
        if step < hi - lo:
            yield from merge(lo, hi, step)
            yield from merge(lo + r, hi, step)
            for i in range(lo + r, hi - r, step):
                yield (i, i + r)
        else:
            yield (lo, lo + r)

    if hi - lo >= 1:
        mid = lo + (hi - lo) // 2
        yield from _sort_pairs(lo, mid)
        yield from _sort_pairs(mid + 1, hi)
        yield from merge(lo, hi, 1)


def _sort_desc(vals):
    vals = list(vals)
    for i, j in _sort_pairs(0, len(vals) - 1):
        vals[i], vals[j] = jnp.maximum(vals[i], vals[j]), jnp.minimum(vals[i], vals[j])
    return vals


def _merge_top(a, b):
    n = len(a)
    c = [jnp.maximum(a[k], b[n - 1 - k]) for k in range(n)]
    stride = n // 2
    while stride:
        for i in range(n):
            if not i & stride:
                c[i], c[i + stride] = jnp.maximum(c[i], c[i + stride]), jnp.minimum(c[i], c[i + stride])
        stride //= 2
    return c


def _top_keys(ref, hd):
    k = PEER_TOPK
    tile = lambda key: ref[hd, key:key + 1, :].reshape(8, LANES)
    groups = [_sort_desc([tile(g * k + r) for r in range(k)]) for g in range(N_KEYS // k)]
    while len(groups) > 1:
        groups = [_merge_top(groups[i], groups[i + 1]) for i in range(0, len(groups), 2)]
    return groups[0]


def _route_kernel(s1_ref, s2_ref, o_ref, t2_ref):
    k = PEER_TOPK

    def head(hd, _):
        t1, t2 = _top_keys(s1_ref, hd), _top_keys(s2_ref, hd)
        cands = [t1[a] + t2[b] for a in range(k) for b in range(k) if (a + 1) * (b + 1) <= k]
        size = 1 << (len(cands) - 1).bit_length()
        cands += [jnp.full_like(t1[0], -jnp.inf)] * (size - len(cands))
        best = _sort_desc(cands)
        z = jnp.ones_like(best[0])
        for v in best[1:k]:
            z = z + jnp.exp(v - best[0])
        stats = (best[k - 1], t1[0], t2[0], 1.0 / z)
        for c in range(8):
            lanes = slice(c * LANES, (c + 1) * LANES)
            for n, v in enumerate(stats):
                o_ref[n, hd, :, lanes] = v[c:c + 1]
            for b in range(k):
                t2_ref[hd, b:b + 1, lanes] = t2[b][c:c + 1]
        return 0

    lax.fori_loop(0, PEER_HEADS, head, 0)


def _route(s1d, s2d):
    ntok = s1d.shape[2]
    n_rows = ntok // LANES
    blk = 8 * LANES
    spec = pl.BlockSpec((PEER_HEADS, N_KEYS, 8 * LANES), lambda i: (0, 0, i))
    return pl.pallas_call(
        _route_kernel,
        grid=(n_rows // 8,),
        in_specs=[spec, spec],
        out_specs=[pl.BlockSpec((4, PEER_HEADS, 1, blk), lambda i: (0, 0, 0, i)),
                   pl.BlockSpec((PEER_HEADS, PEER_TOPK, blk), lambda i: (0, 0, i))],
        out_shape=[jax.ShapeDtypeStruct((4, PEER_HEADS, 1, ntok), F32),
                   jax.ShapeDtypeStruct((PEER_HEADS, PEER_TOPK, ntok), F32)],
        compiler_params=_cparams(("parallel",)),
        name="peer_route",
    )(s1d, s2d)


def _peer_kernel(h2_ref, s1_ref, s2_ref, t2_ref, st_ref, u_ref, v_ref, x1_ref, m_ref, fn_ref, yc_ref, yl_ref,
                 e2_s, act_s, aw_s, acc_s, *, ctx_tiles):
    k = pl.program_id(1)
    chunks = [slice(c * LANES, (c + 1) * LANES) for c in range(PEER_TN // LANES)]

    @pl.when(k == 0)
    def _():
        for hd in range(PEER_HEADS):
            e2_s[hd] = jnp.exp(s2_ref[hd] - st_ref[2, hd])
        acc_s[...] = jnp.zeros_like(acc_s)

    floor, gain = [], []
    for hd in range(PEER_HEADS):
        s1 = s1_ref[hd]
        low = jnp.full_like(s1, jnp.inf)
        for b in range(PEER_TOPK):
            t2b = t2_ref[hd, b:b + 1, :]
            low = jnp.where(s1 + t2b >= st_ref[0, hd], t2b, low)
        floor.append(low)
        gain.append(jnp.exp(s1 - st_ref[1, hd]) * (0.5 * st_ref[3, hd]))

    act_s[...] = lax.dot_general(u_ref[...].astype(BF16), h2_ref[...], (((1,), (1,)), ((), ())),
                                 preferred_element_type=F32)
    for cs in chunks:
        for ii in range(PEER_ROWS):
            rows = slice(ii * N_KEYS, (ii + 1) * N_KEYS)
            w = None
            for hd in range(PEER_HEADS):
                hit = s2_ref[hd, :, cs] >= floor[hd][ii:ii + 1, cs]
                term = jnp.where(hit, e2_s[hd, :, cs], 0.0) * gain[hd][ii:ii + 1, cs]
                w = term if w is None else w + term
            x = act_s[rows, cs]
            t = jnp.tanh(x * (GELU_C0 + GELU_C1 * (x * x)))
            aw_s[rows, cs] = ((x * w) * (1.0 + t)).astype(BF16)
    acc_s[...] += lax.dot_general(aw_s[...], v_ref[...].astype(BF16), (((0,), (0,)), ((), ())),
                                  preferred_element_type=F32)

    last = k == pl.num_programs(1) - 1
    is_ctx = pl.program_id(0) < ctx_tiles
    for cond, y_ref in ((is_ctx, yc_ref), (jnp.logical_not(is_ctx), yl_ref)):
        @pl.when(last & cond)
        def _(y_ref=y_ref):
            y_ref[...] = _rms(x1_ref[...] + m_ref[0][5:6] * acc_s[...], fn_ref[...])


def _peer(h2, s1t, s2t, t2, stats, u_tab, v_tab, x1, m6, mrow, final_norm, ctx_tokens):
    ntok = h2.shape[0]
    tn, te = PEER_TN, PEER_TE
    nc = ctx_tokens // tn
    tok = pl.BlockSpec((tn, D_MODEL), lambda t, k: (t, 0))
    tab = pl.BlockSpec((te, D_MODEL), lambda t, k: (k, 0))
    keys = pltpu.VMEM((PEER_HEADS, N_KEYS, tn), F32)
    return pl.pallas_call(
        functools.partial(_peer_kernel, ctx_tiles=nc),
        grid=(ntok // tn, N_EXPERTS // te),
        in_specs=[tok,
                  pl.BlockSpec((PEER_HEADS, PEER_ROWS, tn), lambda t, k: (0, k, t)),
                  pl.BlockSpec((PEER_HEADS, N_KEYS, tn), lambda t, k: (0, 0, t)),
                  pl.BlockSpec((PEER_HEADS, PEER_TOPK, tn), lambda t, k: (0, 0, t)),
                  pl.BlockSpec((4, PEER_HEADS, 1, tn), lambda t, k: (0, 0, 0, t)),
                  tab, tab, tok,
                  pl.BlockSpec((1, 6, D_MODEL), lambda t, k: (mrow(t), 0, 0)),
                  pl.BlockSpec((1, D_MODEL), lambda t, k: (0, 0))],
        out_specs=[pl.BlockSpec((tn, D_MODEL), lambda t, k: (jnp.minimum(t, nc - 1), 0)),
                   pl.BlockSpec((tn, D_MODEL), lambda t, k: (jnp.maximum(t - nc, 0), 0))],
        out_shape=[jax.ShapeDtypeStruct((ctx_tokens, D_MODEL), F32),
                   jax.ShapeDtypeStruct((ntok - ctx_tokens, D_MODEL), F32)],
        scratch_shapes=[keys, pltpu.VMEM((te, tn), F32), pltpu.VMEM((te, tn), BF16),
                        pltpu.VMEM((tn, D_MODEL), F32)],
        compiler_params=_cparams(("arbitrary", "arbitrary")),
        name="peer_dense",
    )(h2, s1t, s2t, t2, stats, u_tab, v_tab, x1, m6, final_norm)


def _rot_cols(w):
    j = np.arange(QK_ROPE)
    first = (j % (QK_ROPE // 2)) < (QK_ROPE // 4)
    perm = np.where(first, j + QK_ROPE // 4, j - QK_ROPE // 4)
    sign = np.where(first, -1.0, 1.0).astype(np.float32)
    return w[..., perm] * sign


def _rope_slot(w_rope):
    pad = [(0, 0)] * (w_rope.ndim - 1)
    return jnp.pad(w_rope, pad + [(QK_NOPE, HEAD_PAD - QK_NOPE - QK_ROPE)])


def _rope_tables(n_tokens):
    n_rows = n_tokens // GRID_W
    rows = jnp.repeat(jnp.arange(n_rows, dtype=F32), GRID_W)
    cols = jnp.tile(jnp.arange(GRID_W, dtype=F32), n_rows)
    half = QK_ROPE // 2
    inv_freq = 1.0 / (ROPE_BASE ** (jnp.arange(0, half, 2, dtype=F32) / half))
    ang_r = rows[:, None] * inv_freq
    ang_c = cols[:, None] * inv_freq
    ang = jnp.concatenate([ang_r, ang_r, ang_c, ang_c], axis=-1)
    lead = (QK_NOPE, HEAD_PAD - QK_NOPE - QK_ROPE)
    cos = jnp.pad(jnp.cos(ang), [(0, 0), lead], constant_values=1.0)
    sin = jnp.pad(jnp.sin(ang), [(0, 0), lead])
    return cos, sin


def kernel(x_prompt, x_sample, c, cache_ckv, cache_krope, state_ssm, c_ctx, w_mod, b_mod, norm1, w_in,
           ssm_lam_re, ssm_lam_im, ssm_log_dt, ssm_b_re, ssm_b_im, ssm_c_re, ssm_c_im, ssm_d, w_glu,
           q_norm, w_uq, kv_norm, w_uk, w_uv, w_out, norm2, w_query, sub_keys, u_table, v_table,
           final_norm):
    bc, tc_len, _ = x_prompt.shape
    bl, tl_len, _ = x_sample.shape
    l = 0
    row = lambda a: a.reshape(1, -1)
    assert w_mod.shape[0] == 1
    big = lambda a: a.reshape(a.shape[1:])

    n_mod = 8
    cvec = jnp.concatenate([c_ctx[None], c, jnp.zeros((n_mod - 1 - bl, D_MODEL), F32)], 0)
    m6 = _modulation(cvec, big(w_mod), row(b_mod[l])).reshape(n_mod, 6, D_MODEL)

    wi = w_in[l]
    o_kr = SSM_WIDTH + Q_RANK + KV_RANK
    w_kr = wi[:, o_kr:]
    w_in_ext = jnp.concatenate([wi[:, :o_kr], _rope_slot(w_kr), _rope_slot(_rot_cols(w_kr))], 1).astype(BF16)
    wq3 = w_uq[l].reshape(Q_RANK, MLA_HEADS, QK_NOPE + QK_ROPE)
    wq_main = jnp.pad(wq3, [(0, 0), (0, 0), (0, HEAD_PAD - QK_NOPE - QK_ROPE)])
    wq_rot = _rope_slot(_rot_cols(wq3[..., QK_NOPE:]))
    w_uq_ext = jnp.concatenate([wq_main.reshape(Q_RANK, -1), wq_rot.reshape(Q_RANK, -1)], 1).astype(BF16)
    w_uk_ext = jnp.pad(w_uk[l].reshape(KV_RANK, MLA_HEADS, QK_NOPE),
                       [(0, 0), (0, 0), (0, HEAD_PAD - QK_NOPE)]).reshape(KV_RANK, -1).astype(BF16)
    w_uv_b = w_uv[l].astype(BF16)
    w_glu_b = w_glu[l].astype(BF16)
    w_out_s = w_out[l][:SSM_WIDTH].astype(BF16)
    w_out_a = w_out[l][SSM_WIDTH:].astype(BF16)
    w_query_b = w_query[l].astype(BF16)
    sub_keys_b = sub_keys[l].astype(BF16)
    u_tab, v_tab = big(u_table), big(v_table)
    cos, sin = _rope_tables(tl_len)

    bw, cw, a_blk = _s5_params(ssm_lam_re[l], ssm_lam_im[l], ssm_log_dt[l], ssm_b_re[l], ssm_b_im[l],
                               ssm_c_re[l], ssm_c_im[l])

    ctx_row = lambda b: 0
    lat_row = lambda b: 1 + b
    xc = x_prompt.reshape(bc * tc_len, D_MODEL)
    xl = x_sample.reshape(bl * tl_len, D_MODEL)

    u_c, q_c, ckv_c, kr_c = _pre(xc, m6, ctx_row, row(norm1[l]), w_in_ext, row(q_norm[l]), row(kv_norm[l]),
                                 w_uq_ext[:, :MLA_HEADS * HEAD_PAD], bc, tc_len, None)
    u_l, q_l, ckv_l, kr_l = _pre(xl, m6, lat_row, row(norm1[l]), w_in_ext, row(q_norm[l]), row(kv_norm[l]),
                                 w_uq_ext, bl, tl_len, (cos, sin), pad_rows=True)

    rl = 2 * bl
    h0_c = jnp.zeros((2, 2, S5_BLOCKS_PER_HALF, 2, bc, LANES), F32)
    yf_c, yb_c, st_c = _s5_scan(u_c.reshape(tc_len * bc, SSM_WIDTH), bw, cw, a_blk, h0_c, bc, tc_len,
                                S5_TILE_ROWS // bc)
    h0_l = _s5_state_in(state_ssm[:, l])
    h0_l = jnp.stack([h0_l, jnp.zeros_like(h0_l)], axis=-2).reshape(h0_l.shape[:4] + (rl, LANES))
    yf_l, yb_l, _ = _s5_scan(u_l.reshape(tl_len * rl, SSM_WIDTH), bw, cw, a_blk, h0_l, rl, tl_len,
                             S5_TILE_ROWS // rl)

    at_c = _attention(q_c, ckv_c.reshape(bc, tc_len, KV_RANK), kr_c.reshape(bc, tc_len, HEAD_PAD),
                      w_uk_ext, w_uv_b, bc, tc_len)
    ckv_all = jnp.concatenate([cache_ckv[:, l], ckv_l.reshape(bl, tl_len, KV_RANK)], 1)
    kr_all = jnp.concatenate([_rope_slot(cache_krope[:, l]), kr_l.reshape(bl, tl_len, HEAD_PAD)], 1)
    at_l = _attention(q_l, ckv_all, kr_all, w_uk_ext, w_uv_b, bl, tl_len)

    ntok = bc * tc_len + bl * tl_len
    x1, h2, s1t, s2t = _post((xc, at_c, u_c, yf_c, yb_c, bc, tc_len, 1),
                             (xl, at_l, u_l, yf_l, yb_l, bl, tl_len, 2),
                             m6, row(ssm_d[l]), w_glu_b, w_out_s, w_out_a, row(norm2[l]), w_query_b, sub_keys_b)

    dense = lambda a: a.reshape(PEER_HEADS, N_KEYS, ntok // LANES, LANES)
    stats, t2 = _route(s1t, s2t)
    ctx_peer_tiles = bc * tc_len // PEER_TN
    lat_peer_tiles = tl_len // PEER_TN
    peer_row = lambda t: jnp.where(t < ctx_peer_tiles, 0, 1 + (t - ctx_peer_tiles) // lat_peer_tiles)
    y_c, y_l = _peer(h2, s1t, s2t, t2, stats, u_tab, v_tab, x1, m6, peer_row, row(final_norm), bc * tc_len)

    y_prompt = y_c.reshape(bc, tc_len, D_MODEL)
    y_sample = y_l.reshape(bl, tl_len, D_MODEL)
    new_ckv = ckv_c.reshape(bc, 1, tc_len, KV_RANK)
    new_krope = kr_c[:, QK_NOPE:QK_NOPE + QK_ROPE].reshape(bc, 1, tc_len, QK_ROPE)
    new_ssm = _s5_state_out(st_c)[:, None]
    return (y_prompt, y_sample, new_ckv, new_krope, new_ssm)
```
